```python
import math
import jax, jax.numpy as jnp
from jax import lax
import numpy as np


D_MODEL = 1024
BATCH = 2
SEQ = 16384
DEPTH = 2

GRID_W = 64
CTX_LEN = 256
D_MIX = D_MODEL
D_POOL = D_MIX // 2
POOL_WINDOWS = (2, 4, 8, 16)
N_POOL_GROUPS = len(POOL_WINDOWS)
POOL_GC = D_POOL // N_POOL_GROUPS
D_ATTN = D_MIX - D_POOL
ATTN_HEAD_DIM = 64
N_ATTN_HEADS = D_ATTN // (2 * ATTN_HEAD_DIM)
ROPE_THETA = 10000.0
Q_BLOCK = 128
Q_OFF = D_POOL
K_OFF = Q_OFF + D_ATTN
V_OFF = K_OFF + D_ATTN
D_IN = V_OFF + D_ATTN
N_GROUPS = 4
EXPERTS_PER_GROUP = 8
N_EXPERTS = N_GROUPS * EXPERTS_PER_GROUP
TOP_K_FINE = 2
D_EXPERT = 256
EPS = 1e-6

kernel_name = 'hybrid_pool_diffattn_hmoe_dit'


def rmsnorm(x, g):
    xf = x.astype(jnp.float32)
    y = xf * lax.rsqrt(jnp.mean(xf * xf, axis=-1, keepdims=True) + EPS)
    return (y * g.astype(jnp.float32)).astype(x.dtype)


def modulate(h, shift, scale):
    return h * (1 + scale) + shift


def axial_rope_tables(n_tokens, dtype):
    rows = n_tokens // GRID_W
    row = jnp.repeat(jnp.arange(rows, dtype=jnp.int32), GRID_W)
    col = jnp.tile(jnp.arange(GRID_W, dtype=jnp.int32), rows)
    n_freq = ATTN_HEAD_DIM // 4
    inv = ROPE_THETA ** (-jnp.arange(n_freq, dtype=jnp.float32) / n_freq)
    ang_r = (row[:, None].astype(jnp.float32) * inv)[:, None, None, :]
    ang_c = (col[:, None].astype(jnp.float32) * inv)[:, None, None, :]
    return (jnp.cos(ang_r).astype(dtype), jnp.sin(ang_r).astype(dtype),
            jnp.cos(ang_c).astype(dtype), jnp.sin(ang_c).astype(dtype))


def _rope_half(x, cos, sin):
    x1, x2 = jnp.split(x, 2, axis=-1)
    return jnp.concatenate([x1 * cos - x2 * sin, x2 * cos + x1 * sin], axis=-1)


def apply_axial_rope(x, tables):
    cos_r, sin_r, cos_c, sin_c = tables
    half = ATTN_HEAD_DIM // 2
    return jnp.concatenate([_rope_half(x[..., :half], cos_r, sin_r),
                            _rope_half(x[..., half:], cos_c, sin_c)], axis=-1)


def heads_qk(p):
    return p.reshape(*p.shape[:-1], N_ATTN_HEADS, 2, ATTN_HEAD_DIM)


def heads_v(p):
    return p.reshape(*p.shape[:-1], N_ATTN_HEADS, 2 * ATTN_HEAD_DIM)


def pool_mixer(u, pool_w, pool_scale):
    b, n, _ = u.shape
    uf = u.reshape(b, n, N_POOL_GROUPS, POOL_GC).astype(jnp.float32)
    csum = jnp.concatenate([jnp.zeros_like(uf[:, :1]), jnp.cumsum(uf, axis=1)], axis=1)
    t = jnp.arange(n)
    outs = []
    for g, w in enumerate(POOL_WINDOWS):
        left = w // 2
        right = w - 1 - left
        lo = jnp.clip(t - left, 0, n)
        hi = jnp.clip(t + right + 1, 0, n)
        cnt = (hi - lo).astype(jnp.float32)[None, :, None]
        outs.append((csum[:, hi, g] - csum[:, lo, g]) / cnt - uf[:, :, g])
    pooled = jnp.stack(outs, axis=2).astype(u.dtype)
    mixed = jnp.einsum('blgc,gce->blge', pooled, pool_w)
    return mixed.reshape(b, n, D_POOL) * pool_scale


def diff_attend(q, k, v, lam):
    s = jnp.einsum('bqhmd,bkhmd->bhmqk', q, k).astype(jnp.float32) * (ATTN_HEAD_DIM ** -0.5)
    p = jax.nn.softmax(s, axis=-1)
    a = p[:, :, 0] - lam * p[:, :, 1]
    return jnp.einsum('bhqk,bkhe->bqhe', a.astype(v.dtype), v)


def diff_attention_latent(q, k, v, k_ctx, v_ctx, lam):
    keys = jnp.concatenate([k, k_ctx], axis=1)
    vals = jnp.concatenate([v, v_ctx], axis=1)
    b, n = q.shape[:2]
    nb = n // Q_BLOCK
    qb = q.reshape(b, nb, Q_BLOCK, N_ATTN_HEADS, 2, ATTN_HEAD_DIM).transpose(1, 0, 2, 3, 4, 5)
    out = lax.map(lambda qblk: diff_attend(qblk, keys, vals, lam), qb)
    return out.transpose(1, 0, 2, 3, 4).reshape(b, n, N_ATTN_HEADS, 2 * ATTN_HEAD_DIM)


def merge_head_groups(u, attn, pool_w, pool_scale, subln_g, lam_init, w_out):
    pool = pool_mixer(u, pool_w, pool_scale)
    attn = rmsnorm(attn, subln_g) * (1 - lam_init)
    attn = attn.reshape(*attn.shape[:2], D_ATTN)
    return jnp.concatenate([pool, attn], axis=-1) @ w_out


def hier_moe(h, rc_w, rc_b, rf_w, rf_b, w_gate, w_up, w_down):
    shp = h.shape
    t = h.reshape(-1, D_MODEL)
    n = t.shape[0]
    logits_c = (t @ rc_w + rc_b).astype(jnp.float32)
    probs_c = jax.nn.softmax(logits_c, axis=-1)
    _, top_g = lax.top_k(logits_c, 1)
    g_idx = top_g[:, 0]
    p_group = jnp.take_along_axis(probs_c, top_g, axis=-1)
    logits_f = (t @ rf_w + rf_b).astype(jnp.float32).reshape(n, N_GROUPS, EXPERTS_PER_GROUP)
    lf = logits_f[jnp.arange(n), g_idx]
    top_v, top_i = lax.top_k(lf, TOP_K_FINE)
    w_top = jax.nn.softmax(top_v, axis=-1) * p_group
    expert_id = g_idx[:, None] * EXPERTS_PER_GROUP + top_i
    gates = jnp.einsum('nk,nke->ne', w_top, jax.nn.one_hot(expert_id, N_EXPERTS, dtype=jnp.float32))
    gates = gates.reshape(n, N_GROUPS, EXPERTS_PER_GROUP).astype(t.dtype)
    out = jnp.zeros_like(t)
    for g in range(N_GROUPS):
        a = jnp.einsum('nd,edf->nef', t, w_gate[g])
        bu = jnp.einsum('nd,edf->nef', t, w_up[g])
        hid = jax.nn.silu(a) * bu * gates[:, g, :, None]
        out = out + jnp.einsum('nef,efd->nd', hid, w_down[g])
    return out.reshape(shp)


def setup_inputs(seed: int = 0) -> dict:
    key = jax.random.key(seed)
    ks = jax.random.split(key, 26)
    f32 = jnp.float32
    nrm = lambda k, s: jax.random.normal(k, s, f32)
    D = D_MODEL
    return {
        'x': nrm(ks[0], (BATCH, SEQ, D)),
        'c': nrm(ks[1], (BATCH, D)),
        'ctx': nrm(ks[2], (BATCH, CTX_LEN, D)),
        'c_ctx': nrm(ks[3], (D,)),
        'ada_w': nrm(ks[4], (DEPTH, D, 6 * D)) * (0.5 * D ** -0.5),
        'ada_b': nrm(ks[5], (DEPTH, 6 * D)) * 0.01,
        'norm1_g': 1.0 + 0.05 * nrm(ks[6], (DEPTH, D)),
        'norm2_g': 1.0 + 0.05 * nrm(ks[7], (DEPTH, D)),
        'w_in': nrm(ks[8], (DEPTH, D, D_IN)) * D ** -0.5,
        'pool_w': nrm(ks[9], (DEPTH, N_POOL_GROUPS, POOL_GC, POOL_GC)) * POOL_GC ** -0.5,
        'pool_scale': 1.0 + 0.1 * nrm(ks[10], (DEPTH, D_POOL)),
        'lambda_q1': 0.1 * nrm(ks[11], (DEPTH, ATTN_HEAD_DIM)),
        'lambda_k1': 0.1 * nrm(ks[12], (DEPTH, ATTN_HEAD_DIM)),
        'lambda_q2': 0.1 * nrm(ks[13], (DEPTH, ATTN_HEAD_DIM)),
        'lambda_k2': 0.1 * nrm(ks[14], (DEPTH, ATTN_HEAD_DIM)),
        'subln_g': 1.0 + 0.05 * nrm(ks[15], (DEPTH, 2 * ATTN_HEAD_DIM)),
        'w_out': nrm(ks[16], (DEPTH, D_MIX, D)) * D_MIX ** -0.5,
        'router_coarse_w': nrm(ks[17], (DEPTH, D, N_GROUPS)) * D ** -0.5,
        'router_coarse_b': 0.01 * nrm(ks[18], (DEPTH, N_GROUPS)),
        'router_fine_w': nrm(ks[19], (DEPTH, D, N_EXPERTS)) * D ** -0.5,
        'router_fine_b': 0.01 * nrm(ks[20], (DEPTH, N_EXPERTS)),
        'w_gate': nrm(ks[21], (DEPTH, N_GROUPS, EXPERTS_PER_GROUP, D, D_EXPERT)) * D ** -0.5,
        'w_up': nrm(ks[22], (DEPTH, N_GROUPS, EXPERTS_PER_GROUP, D, D_EXPERT)) * D ** -0.5,
        'w_down': nrm(ks[23], (DEPTH, N_GROUPS, EXPERTS_PER_GROUP, D_EXPERT, D)) * D_EXPERT ** -0.5,
        'final_g': 1.0 + 0.05 * nrm(ks[24], (D,)),
    }


def reference(x, c, ctx, c_ctx, ada_w, ada_b, norm1_g, norm2_g, w_in, pool_w, pool_scale,
              lambda_q1, lambda_k1, lambda_q2, lambda_k2, subln_g, w_out,
              router_coarse_w, router_coarse_b, router_fine_w, router_fine_b,
              w_gate, w_up, w_down, final_g):
    n_lat = x.shape[1]
    tables = axial_rope_tables(n_lat, x.dtype)
    xc = ctx
    s_x = jax.nn.silu(c)
    s_c = jax.nn.silu(c_ctx)
    for layer in range(DEPTH):
        last = layer == DEPTH - 1
        lam_init = 0.8 - 0.6 * math.exp(-0.3 * layer)
        mod_x = (s_x @ ada_w[layer] + ada_b[layer])[:, None, :]
        mod_c = (s_c @ ada_w[layer] + ada_b[layer])[None, None, :]
        sh1, sc1, g1, sh2, sc2, g2 = jnp.split(mod_x, 6, axis=-1)
        csh1, csc1, cg1, csh2, csc2, cg2 = jnp.split(mod_c, 6, axis=-1)
        lam = (jnp.exp(jnp.sum(lambda_q1[layer] * lambda_k1[layer]).astype(jnp.float32))
               - jnp.exp(jnp.sum(lambda_q2[layer] * lambda_k2[layer]).astype(jnp.float32))
               + lam_init)

        hx = modulate(rmsnorm(x, norm1_g[layer]), sh1, sc1)
        hc = modulate(rmsnorm(xc, norm1_g[layer]), csh1, csc1)
        px = hx @ w_in[layer]
        u_x = px[..., :Q_OFF]
        q_x = apply_axial_rope(heads_qk(px[..., Q_OFF:K_OFF]), tables)
        k_x = apply_axial_rope(heads_qk(px[..., K_OFF:V_OFF]), tables)
        v_x = heads_v(px[..., V_OFF:])
        col0 = K_OFF if last else 0
        pc = hc @ w_in[layer][:, col0:]
        k_c = heads_qk(pc[..., K_OFF - col0:V_OFF - col0])
        v_c = heads_v(pc[..., V_OFF - col0:])
        attn_x = diff_attention_latent(q_x, k_x, v_x, k_c, v_c, lam)
        mix_x = merge_head_groups(u_x, attn_x, pool_w[layer], pool_scale[layer], subln_g[layer],
                                  lam_init, w_out[layer])
        if not last:
            q_c = heads_qk(pc[..., Q_OFF:K_OFF])
            attn_c = diff_attend(q_c, k_c, v_c, lam)
            mix_c = merge_head_groups(pc[..., :Q_OFF], attn_c, pool_w[layer], pool_scale[layer],
                                      subln_g[layer], lam_init, w_out[layer])
            xc = xc + cg1 * mix_c
        x = x + g1 * mix_x

        hx2 = modulate(rmsnorm(x, norm2_g[layer]), sh2, sc2)
        moe_args = (router_coarse_w[layer], router_coarse_b[layer], router_fine_w[layer],
                    router_fine_b[layer], w_gate[layer], w_up[layer], w_down[layer])
        if last:
            x = x + g2 * hier_moe(hx2, *moe_args)
        else:
            hc2 = modulate(rmsnorm(xc, norm2_g[layer]), csh2, csc2)
            n_ctx = xc.shape[1]
            f = hier_moe(jnp.concatenate([hc2, hx2], axis=1), *moe_args)
            xc = xc + cg2 * f[:, :n_ctx]
            x = x + g2 * f[:, n_ctx:]
    return rmsnorm(x, final_g)
```

```python
import functools
import math

import jax
import jax.numpy as jnp
from jax import lax
from jax.experimental import pallas as pl
from jax.experimental.pallas import tpu as pltpu

F32 = jnp.float32
BF16 = jnp.bfloat16

GRID_W = 64
POOL_WINDOWS = (2, 4, 8, 16)
HEAD_DIM = 64
HEAD_PAIR = 2 * HEAD_DIM
ROPE_THETA = 10000.0
N_GROUPS = 4
EXPERTS_PER_GROUP = 8
N_EXPERTS = N_GROUPS * EXPERTS_PER_GROUP
EPS = 1e-6
HALO = 16
ROUTER_LANES = 128
MOD_ROWS = 8

VMEM_LIMIT = 48 * 1024 * 1024


def _params(sem):
    return pltpu.CompilerParams(dimension_semantics=sem, vmem_limit_bytes=VMEM_LIMIT)


def _ada_kernel(cc_ref, w_ref, b_ref, o_ref):
    s = cc_ref[...]
    s = s * jax.nn.sigmoid(s)
    o_ref[...] = jnp.dot(s, w_ref[0], precision=lax.Precision.HIGHEST,
                         preferred_element_type=F32) + b_ref[0]


def _ada(cc, ada_w, ada_b, layer):
    _, d, n = ada_w.shape
    bn = 1024
    return pl.pallas_call(
        _ada_kernel,
        grid=(n // bn,),
        in_specs=[
            pl.BlockSpec((MOD_ROWS, d), lambda j: (0, 0)),
            pl.BlockSpec((1, d, bn), lambda j: (layer, 0, j)),
            pl.BlockSpec((1, 1, bn), lambda j: (layer, 0, j)),
        ],
        out_specs=pl.BlockSpec((MOD_ROWS, bn), lambda j: (0, j)),
        out_shape=jax.ShapeDtypeStruct((MOD_ROWS, n), F32),
        compiler_params=_params(("arbitrary",)),
        name="ada",
    )(cc, ada_w, ada_b.reshape(ada_b.shape[0], 1, n))


def _rope_tables(n):
    pos = jnp.arange(n, dtype=jnp.int32)
    row = (pos // GRID_W).astype(F32)
    col = (pos % GRID_W).astype(F32)
    n_freq = HEAD_DIM // 4
    inv = ROPE_THETA ** (-jnp.arange(n_freq, dtype=F32) / n_freq)
    ang_r = row[:, None] * inv
    ang_c = col[:, None] * inv
    cos = jnp.concatenate([jnp.cos(ang_r), jnp.cos(ang_r), jnp.cos(ang_c), jnp.cos(ang_c)], axis=-1)
    sin = jnp.concatenate([-jnp.sin(ang_r), jnp.sin(ang_r), -jnp.sin(ang_c), jnp.sin(ang_c)], axis=-1)
    return jnp.tile(cos, (1, 2)), jnp.tile(sin, (1, 2))


def _rmsnorm(x, g):
    ms = jnp.mean(x * x, axis=-1, keepdims=True)
    return x * lax.rsqrt(ms + EPS) * g


def _inproj_kernel(*refs, rope, d_pool, d_attn, q_scale):
    if rope:
        x_ref, sh_ref, sc_ref, g_ref, w_ref, cos_ref, sin_ref, u_ref, q_ref, k_ref, v_ref = refs
    else:
        x_ref, sh_ref, sc_ref, g_ref, w_ref, u_ref, q_ref, k_ref, v_ref = refs
    h = _rmsnorm(x_ref[0], g_ref[...]) * (1.0 + sc_ref[0]) + sh_ref[0]
    p = jnp.dot(h.astype(BF16), w_ref[...], preferred_element_type=F32)
    u_ref[0] = p[:, :d_pool]
    v_ref[0] = p[:, d_pool + 2 * d_attn:].astype(BF16)
    if rope:
        cos = cos_ref[...]
        sin = sin_ref[...]
        lane = lax.broadcasted_iota(jnp.int32, cos.shape, 1)
        first_half = (lane % (HEAD_DIM // 2)) < (HEAD_DIM // 4)
    for dst, off, scale in ((q_ref, d_pool, q_scale), (k_ref, d_pool + d_attn, 1.0)):
        for h0 in range(0, d_attn, HEAD_PAIR):
            c = p[:, off + h0: off + h0 + HEAD_PAIR]
            if rope:
                partner = jnp.where(first_half,
                                    pltpu.roll(c, HEAD_PAIR - HEAD_DIM // 4, 1),
                                    pltpu.roll(c, HEAD_DIM // 4, 1))
                c = c * cos + partner * sin
            if scale != 1.0:
                c = c * scale
            dst[0, :, h0:h0 + HEAD_PAIR] = c.astype(BF16)


def _inproj(x, shift, scale, g, w_bf, tables):
    b, l, d = x.shape
    d_in = w_bf.shape[1]
    d_pool = d // 2
    d_attn = (d_in - d_pool) // 3
    t = min(512, l)
    rope = tables is not None
    row = lambda bi, i: (bi, i, 0)
    per_batch = lambda bi, i: (bi, 0, 0)
    const = lambda bi, i: (0, 0)
    in_specs = [
        pl.BlockSpec((1, t, d), row),
        pl.BlockSpec((1, 1, d), per_batch),
        pl.BlockSpec((1, 1, d), per_batch),
        pl.BlockSpec((1, d), const),
        pl.BlockSpec((d, d_in), const),
    ]
    args = [x, shift, scale, g.reshape(1, d), w_bf]
    if rope:
        in_specs += [pl.BlockSpec((t, HEAD_PAIR), lambda bi, i: (i, 0))] * 2
        args += list(tables)
    return pl.pallas_call(
        functools.partial(_inproj_kernel, rope=rope, d_pool=d_pool, d_attn=d_attn,
                          q_scale=HEAD_DIM ** -0.5),
        grid=(b, l // t),
        in_specs=in_specs,
        out_specs=[pl.BlockSpec((1, t, d_pool), row)] + [pl.BlockSpec((1, t, d_attn), row)] * 3,
        out_shape=[jax.ShapeDtypeStruct((b, l, d_pool), F32)]
        + [jax.ShapeDtypeStruct((b, l, d_attn), BF16)] * 3,
        compiler_params=_params(("parallel", "parallel")),
        name="inproj",
    )(*args)


def _attn_kernel(*refs, tq, tk, n_chunks, has_ctx, lam_init):
    if has_ctx:
        lam_ref, q_ref, k_ref, v_ref, kc_ref, vc_ref, g_ref, o_ref, m_scr, l_scr, acc_scr = refs
    else:
        lam_ref, q_ref, k_ref, v_ref, g_ref, o_ref, m_scr, l_scr, acc_scr = refs
    q = q_ref[0]
    lane = lax.broadcasted_iota(jnp.int32, q.shape, 1)
    zero = jnp.zeros_like(q)
    qs = jnp.concatenate([jnp.where(lane < HEAD_DIM, q, zero), jnp.where(lane >= HEAD_DIM, q, zero)], axis=0)

    m_scr[...] = jnp.full(m_scr.shape, -jnp.inf, F32)
    l_scr[...] = jnp.zeros(l_scr.shape, F32)
    acc_scr[...] = jnp.zeros(acc_scr.shape, F32)

    def chunk(kj, vj):
        s = lax.dot_general(qs, kj, (((1,), (1,)), ((), ())), preferred_element_type=F32)
        m_old = m_scr[...]
        m_new = jnp.maximum(m_old, jnp.max(s, axis=1, keepdims=True))
        alpha = jnp.exp(m_old - m_new)
        p = jnp.exp(s - m_new)
        l_scr[...] = alpha * l_scr[...] + jnp.sum(p, axis=1, keepdims=True)
        acc_scr[...] = alpha * acc_scr[...] + jnp.dot(p.astype(BF16), vj, preferred_element_type=F32)
        m_scr[...] = m_new

    def body(j, carry):
        start = pl.multiple_of(j * tk, tk)
        chunk(k_ref[0, pl.ds(start, tk), :], v_ref[0, pl.ds(start, tk), :])
        return carry

    lax.fori_loop(0, n_chunks, body, 0)
    if has_ctx:
        chunk(kc_ref[0], vc_ref[0])

    o = acc_scr[...] / l_scr[...]
    lv = lam_ref[...]
    lam = (jnp.exp(jnp.sum(lv[0:1] * lv[1:2], axis=1, keepdims=True))
           - jnp.exp(jnp.sum(lv[2:3] * lv[3:4], axis=1, keepdims=True)) + lam_init)
    a = o[:tq] - lam * o[tq:]
    o_ref[0] = (_rmsnorm(a, g_ref[...]) * (1.0 - lam_init)).astype(BF16)


def _attention(q, k, v, kc, vc, lam4, subln_g, lam_init):
    b, lq, d_attn = q.shape
    lk = k.shape[1]
    n_heads = d_attn // HEAD_PAIR
    tq = min(1024, lq)
    tk = min(512, lk)
    has_ctx = kc is not None
    qmap = lambda bi, h, i: (bi, i, h)
    kvmap = lambda bi, h, i: (bi, 0, h)
    const = lambda bi, h, i: (0, 0)
    in_specs = [
        pl.BlockSpec(lam4.shape, const),
        pl.BlockSpec((1, tq, HEAD_PAIR), qmap),
        pl.BlockSpec((1, lk, HEAD_PAIR), kvmap),
        pl.BlockSpec((1, lk, HEAD_PAIR), kvmap),
    ]
    args = [lam4, q, k, v]
    if has_ctx:
        lc = kc.shape[1]
        in_specs += [pl.BlockSpec((1, lc, HEAD_PAIR), kvmap)] * 2
        args += [kc, vc]
    in_specs.append(pl.BlockSpec((1, HEAD_PAIR), const))
    args.append(subln_g.reshape(1, HEAD_PAIR))
    return pl.pallas_call(
        functools.partial(_attn_kernel, tq=tq, tk=tk, n_chunks=lk // tk, has_ctx=has_ctx, lam_init=lam_init),
        grid=(b, n_heads, lq // tq),
        in_specs=in_specs,
        out_specs=pl.BlockSpec((1, tq, HEAD_PAIR), qmap),
        out_shape=jax.ShapeDtypeStruct((b, lq, d_attn), BF16),
        scratch_shapes=[
            pltpu.VMEM((2 * tq, 1), F32),
            pltpu.VMEM((2 * tq, 1), F32),
            pltpu.VMEM((2 * tq, HEAD_PAIR), F32),
        ],
        compiler_params=_params(("parallel", "parallel", "parallel")),
        name="diff_attn",
    )(*args)


def _route(logits):
    lane = lax.broadcasted_iota(jnp.int32, logits.shape, 1)
    big = jnp.int32(ROUTER_LANES)
    neg = jnp.float32(-jnp.inf)
    lc = jnp.where(lane < N_GROUPS, logits, neg)
    mc = jnp.max(lc, axis=1, keepdims=True)
    p_group = 1.0 / jnp.sum(jnp.exp(lc - mc), axis=1, keepdims=True)
    g_idx = jnp.min(jnp.where(lc == mc, lane, big), axis=1, keepdims=True)
    lo = N_GROUPS + EXPERTS_PER_GROUP * g_idx
    lf = jnp.where((lane >= lo) & (lane < lo + EXPERTS_PER_GROUP), logits, neg)
    v1 = jnp.max(lf, axis=1, keepdims=True)
    i1 = jnp.min(jnp.where(lf == v1, lane, big), axis=1, keepdims=True)
    lf2 = jnp.where(lane == i1, neg, lf)
    v2 = jnp.max(lf2, axis=1, keepdims=True)
    i2 = jnp.min(jnp.where(lf2 == v2, lane, big), axis=1, keepdims=True)
    e2 = jnp.exp(v2 - v1)
    w1 = p_group / (1.0 + e2)
    w2 = p_group * e2 / (1.0 + e2)
    return jnp.where(lane == i1, w1, 0.0) + jnp.where(lane == i2, w2, 0.0)


def _merge_kernel(u_ref, up_ref, un_ref, a_ref, x_ref, pw_ref, ps_ref, wo_ref, g1_ref, n2_ref, sh_ref,
                  sc_ref, wr_ref, rb_ref, xo_ref, h_ref, gate_ref, ext_scr, *, t, seq_len, d_pool):
    i = pl.program_id(1)
    n_tiles = pl.num_programs(1)
    u = u_ref[0]
    ext_scr[0:HALO] = jnp.where(i > 0, up_ref[0], 0.0)
    ext_scr[HALO:HALO + t] = u
    ext_scr[HALO + t:] = jnp.where(i < n_tiles - 1, un_ref[0], 0.0)

    pos = i * t + lax.broadcasted_iota(jnp.int32, (t, 1), 0)
    gc = d_pool // len(POOL_WINDOWS)
    mix = jnp.zeros((t, wo_ref.shape[1]), F32)
    for gi, w in enumerate(POOL_WINDOWS):
        left = w // 2
        right = w - 1 - left
        cols = slice(gi * gc, (gi + 1) * gc)
        s = ext_scr[pl.ds(HALO - left, t), cols]
        for o in range(-left + 1, right + 1):
            s = s + ext_scr[pl.ds(HALO + o, t), cols]
        cnt = (jnp.minimum(pos + right + 1, seq_len) - jnp.maximum(pos - left, 0)).astype(F32)
        pooled = s / cnt - u[:, cols]
        mixed = jnp.dot(pooled.astype(BF16), pw_ref[gi], preferred_element_type=F32) * ps_ref[:, cols]
        mix = mix + jnp.dot(mixed.astype(BF16), wo_ref[cols, :], preferred_element_type=F32)
    mix = mix + jnp.dot(a_ref[0], wo_ref[d_pool:, :], preferred_element_type=F32)

    x_new = x_ref[0] + g1_ref[0] * mix
    xo_ref[0] = x_new
    h2 = _rmsnorm(x_new, n2_ref[...]) * (1.0 + sc_ref[0]) + sh_ref[0]
    h_ref[0] = h2.astype(BF16)
    logits = jnp.dot(h2, wr_ref[...], precision=lax.Precision.HIGHEST,
                     preferred_element_type=F32) + rb_ref[...]
    gate_ref[0] = _route(logits)


def _merge(u, attn, x, pool_w_bf, pool_scale, w_out_bf, g1, norm2_g, sh2, sc2, wr, rb):
    b, l, d = x.shape
    d_pool = u.shape[2]
    t = min(512, l)
    hb = t // HALO
    n_halo = l // HALO
    row = lambda bi, i: (bi, i, 0)
    per_batch = lambda bi, i: (bi, 0, 0)
    const2 = lambda bi, i: (0, 0)
    const3 = lambda bi, i: (0, 0, 0)
    return pl.pallas_call(
        functools.partial(_merge_kernel, t=t, seq_len=l, d_pool=d_pool),
        grid=(b, l // t),
        in_specs=[
            pl.BlockSpec((1, t, d_pool), row),
            pl.BlockSpec((1, HALO, d_pool), lambda bi, i: (bi, jnp.maximum(i * hb - 1, 0), 0)),
            pl.BlockSpec((1, HALO, d_pool), lambda bi, i: (bi, jnp.minimum((i + 1) * hb, n_halo - 1), 0)),
            pl.BlockSpec((1, t, attn.shape[2]), row),
            pl.BlockSpec((1, t, d), row),
            pl.BlockSpec(pool_w_bf.shape, const3),
            pl.BlockSpec((1, d_pool), const2),
            pl.BlockSpec(w_out_bf.shape, const2),
            pl.BlockSpec((1, 1, d), per_batch),
            pl.BlockSpec((1, d), const2),
            pl.BlockSpec((1, 1, d), per_batch),
            pl.BlockSpec((1, 1, d), per_batch),
            pl.BlockSpec(wr.shape, const2),
            pl.BlockSpec((1, ROUTER_LANES), const2),
        ],
        out_specs=[
            pl.BlockSpec((1, t, d), row),
            pl.BlockSpec((1, t, d), row),
            pl.BlockSpec((1, t, ROUTER_LANES), row),
        ],
        out_shape=[
            jax.ShapeDtypeStruct((b, l, d), F32),
            jax.ShapeDtypeStruct((b, l, d), BF16),
            jax.ShapeDtypeStruct((b, l, ROUTER_LANES), F32),
        ],
        scratch_shapes=[pltpu.VMEM((t + 2 * HALO, d_pool), F32)],
        compiler_params=_params(("parallel", "parallel")),
        name="merge",
    )(u, u, u, attn, x, pool_w_bf, pool_scale.reshape(1, d_pool), w_out_bf, g1, norm2_g.reshape(1, d),
      sh2, sc2, wr, rb)


def _moe_kernel(*refs, final_norm):
    if final_norm:
        x_ref, h_ref, gate_ref, g2_ref, wg_ref, wu_ref, wd_ref, fg_ref, o_ref, acc_scr = refs
    else:
        x_ref, h_ref, gate_ref, g2_ref, wg_ref, wu_ref, wd_ref, o_ref, acc_scr = refs
    e = pl.program_id(2)

    @pl.when(e == 0)
    def _():
        acc_scr[...] = jnp.zeros(acc_scr.shape, F32)

    h = h_ref[0]
    a = jnp.dot(h, wg_ref[0, 0], preferred_element_type=F32)
    bu = jnp.dot(h, wu_ref[0, 0], preferred_element_type=F32)
    gates = gate_ref[0]
    lane = lax.broadcasted_iota(jnp.int32, gates.shape, 1)
    gcol = jnp.sum(jnp.where(lane == e + N_GROUPS, gates, 0.0), axis=1, keepdims=True)
    hid = (a * jax.nn.sigmoid(a)) * bu * gcol
    acc_scr[...] += jnp.dot(hid.astype(BF16), wd_ref[0, 0], preferred_element_type=F32)

    @pl.when(e == pl.num_programs(2) - 1)
    def _():
        y = x_ref[0] + g2_ref[0] * acc_scr[...]
        if final_norm:
            y = _rmsnorm(y, fg_ref[...])
        o_ref[0] = y


def _moe(x, h, gates, g2, wg_bf, wu_bf, wd_bf, layer, final_g):
    b, l, d = x.shape
    n_e, f = wg_bf.shape[1], wg_bf.shape[3]
    t = min(1024, l)
    row = lambda bi, i, e: (bi, i, 0)
    per_batch = lambda bi, i, e: (bi, 0, 0)
    final_norm = final_g is not None
    in_specs = [
        pl.BlockSpec((1, t, d), row),
        pl.BlockSpec((1, t, d), row),
        pl.BlockSpec((1, t, ROUTER_LANES), row),
        pl.BlockSpec((1, 1, d), per_batch),
        pl.BlockSpec((1, 1, d, f), lambda bi, i, e: (layer, e, 0, 0)),
        pl.BlockSpec((1, 1, d, f), lambda bi, i, e: (layer, e, 0, 0)),
        pl.BlockSpec((1, 1, f, d), lambda bi, i, e: (layer, e, 0, 0)),
    ]
    args = [x, h, gates, g2, wg_bf, wu_bf, wd_bf]
    if final_norm:
        in_specs.append(pl.BlockSpec((1, d), lambda bi, i, e: (0, 0)))
        args.append(final_g.reshape(1, d))
    return pl.pallas_call(
        functools.partial(_moe_kernel, final_norm=final_norm),
        grid=(b, l // t, n_e),
        in_specs=in_specs,
        out_specs=pl.BlockSpec((1, t, d), row),
        out_shape=jax.ShapeDtypeStruct((b, l, d), F32),
        scratch_shapes=[pltpu.VMEM((t, d), F32)],
        compiler_params=_params(("parallel", "parallel", "arbitrary")),
        name="moe",
    )(*args)


def kernel(x, c, ctx, c_ctx, ada_w, ada_b, norm1_g, norm2_g, w_in, pool_w, pool_scale, lambda_q1, lambda_k1,
           lambda_q2, lambda_k2, subln_g, w_out, router_coarse_w, router_coarse_b, router_fine_w,
           router_fine_b, w_gate, w_up, w_down, final_g):
    b, l, d = x.shape
    depth = ada_w.shape[0]
    assert b + 1 <= MOD_ROWS and l % GRID_W == 0

    tables = _rope_tables(l)
    cc = jnp.zeros((MOD_ROWS, d), F32).at[:b].set(c).at[b].set(c_ctx)
    w_in_bf = w_in.astype(BF16)
    pool_w_bf = pool_w.astype(BF16)
    w_out_bf = w_out.astype(BF16)
    n_e = w_gate.shape[1] * w_gate.shape[2]
    f = w_gate.shape[-1]
    wg_bf = w_gate.astype(BF16).reshape(depth, n_e, d, f)
    wu_bf = w_up.astype(BF16).reshape(depth, n_e, d, f)
    wd_bf = w_down.astype(BF16).reshape(depth, n_e, f, d)
    pad = ROUTER_LANES - N_GROUPS - N_EXPERTS
    wr = jnp.concatenate([router_coarse_w, router_fine_w, jnp.zeros((depth, d, pad), F32)], axis=-1)
    rb = jnp.concatenate([router_coarse_b, router_fine_b, jnp.zeros((depth, pad), F32)], axis=-1)

    xc = ctx
    for layer in range(depth):
        last = layer == depth - 1
        lam_init = 0.8 - 0.6 * math.exp(-0.3 * layer)
        mod = _ada(cc, ada_w, ada_b, layer)
        sh1, sc1, g1, sh2, sc2, g2 = jnp.split(mod[:b, None, :], 6, axis=-1)
        csh1, csc1, cg1, csh2, csc2, cg2 = jnp.split(
            jnp.broadcast_to(mod[b][None, None, :], (b, 1, mod.shape[1])), 6, axis=-1)
        lam4 = jnp.stack([lambda_q1[layer], lambda_k1[layer], lambda_q2[layer], lambda_k2[layer]])

        u, q, k, v = _inproj(x, sh1, sc1, norm1_g[layer], w_in_bf[layer], tables)
        uc, qc, kc, vc = _inproj(xc, csh1, csc1, norm1_g[layer], w_in_bf[layer], None)
        attn = _attention(q, k, v, kc, vc, lam4, subln_g[layer], lam_init)
        merge_w = (pool_w_bf[layer], pool_scale[layer], w_out_bf[layer])
        route_w = (wr[layer], rb[layer].reshape(1, ROUTER_LANES))
        x, hx2, gates = _merge(u, attn, x, *merge_w, g1, norm2_g[layer], sh2, sc2, *route_w)
        if not last:
            attn_c = _attention(qc, kc, vc, None, None, lam4, subln_g[layer], lam_init)
            xc, hc2, gates_c = _merge(uc, attn_c, xc, *merge_w, cg1, norm2_g[layer], csh2, csc2, *route_w)
            xc = _moe(xc, hc2, gates_c, cg2, wg_bf, wu_bf, wd_bf, layer, None)
        x = _moe(x, hx2, gates, g2, wg_bf, wu_bf, wd_bf, layer, final_g if last else None)
    return x
```

```python
import functools
import math

import jax
import jax.numpy as jnp
from jax import lax
from jax.experimental import pallas as pl
from jax.experimental.pallas import tpu as pltpu

F32 = jnp.float32
BF16 = jnp.bfloat16

GRID_W = 64
POOL_WINDOWS = (2, 4, 8, 16)
HEAD_DIM = 64
HEAD_PAIR = 2 * HEAD_DIM
ROPE_THETA = 10000.0
N_GROUPS = 4
EXPERTS_PER_GROUP = 8
N_EXPERTS = N_GROUPS * EXPERTS_PER_GROUP
EPS = 1e-6
HALO = 16
ROUTER_LANES = 128
MOD_ROWS = 8
ONES_ROWS = 16

VMEM_LIMIT = 48 * 1024 * 1024


def _params(sem):
    return pltpu.CompilerParams(dimension_semantics=sem, vmem_limit_bytes=VMEM_LIMIT)


def _ada_kernel(cc_ref, w_ref, b_ref, o_ref):
    s = cc_ref[...]
    s = s * jax.nn.sigmoid(s)
    o_ref[...] = jnp.dot(s, w_ref[0], precision=lax.Precision.HIGHEST,
                         preferred_element_type=F32) + b_ref[0]


def _ada(cc, ada_w, ada_b, layer):
    _, d, n = ada_w.shape
    bn = 1024
    return pl.pallas_call(
        _ada_kernel,
        grid=(n // bn,),
        in_specs=[
            pl.BlockSpec((MOD_ROWS, d), lambda j: (0, 0)),
            pl.BlockSpec((1, d, bn), lambda j: (layer, 0, j)),
            pl.BlockSpec((1, 1, bn), lambda j: (layer, 0, j)),
        ],
        out_specs=pl.BlockSpec((MOD_ROWS, bn), lambda j: (0, j)),
        out_shape=jax.ShapeDtypeStruct((MOD_ROWS, n), F32),
        compiler_params=_params(("arbitrary",)),
        name="ada",
    )(cc, ada_w, ada_b.reshape(ada_b.shape[0], 1, n))


def _rope_tables(n):
    pos = jnp.arange(n, dtype=jnp.int32)
    row = (pos // GRID_W).astype(F32)
    col = (pos % GRID_W).astype(F32)
    n_freq = HEAD_DIM // 4
    inv = ROPE_THETA ** (-jnp.arange(n_freq, dtype=F32) / n_freq)
    ang_r = row[:, None] * inv
    ang_c = col[:, None] * inv
    cos = jnp.concatenate([jnp.cos(ang_r), jnp.cos(ang_r), jnp.cos(ang_c), jnp.cos(ang_c)], axis=-1)
    sin = jnp.concatenate([-jnp.sin(ang_r), jnp.sin(ang_r), -jnp.sin(ang_c), jnp.sin(ang_c)], axis=-1)
    cos, sin = jnp.tile(cos, (1, 2)), jnp.tile(sin, (1, 2))
    return cos, sin, cos.T, sin.T


def _rmsnorm(x, g):
    ms = jnp.mean(x * x, axis=-1, keepdims=True)
    return x * lax.rsqrt(ms + EPS) * g


def _inproj_kernel(*refs, rope, d_pool, d_attn, q_scale):
    if rope:
        (x_ref, sh_ref, sc_ref, g_ref, wuk_ref, wqv_ref, cos_ref, sin_ref, cost_ref, sint_ref,
         u_ref, qt_ref, k_ref, vt_ref) = refs
    else:
        x_ref, sh_ref, sc_ref, g_ref, wuk_ref, wqv_ref, u_ref, qt_ref, k_ref, vt_ref = refs
    h = (_rmsnorm(x_ref[0], g_ref[...]) * (1.0 + sc_ref[0]) + sh_ref[0]).astype(BF16)
    p = jnp.dot(h, wuk_ref[...], preferred_element_type=F32)
    pt = lax.dot_general(wqv_ref[...], h, (((1,), (1,)), ((), ())), preferred_element_type=F32)
    u_ref[0] = p[:, :d_pool]
    vt_ref[0] = pt[d_attn:].astype(BF16)
    half = HEAD_DIM // 4
    if rope:
        cos, sin = cos_ref[...], sin_ref[...]
        cost, sint = cost_ref[...], sint_ref[...]
        lane = lax.broadcasted_iota(jnp.int32, cos.shape, 1)
        first_half = (lane % (2 * half)) < half
    for h0 in range(0, d_attn, HEAD_PAIR):
        c = p[:, d_pool + h0: d_pool + h0 + HEAD_PAIR]
        ct = pt[h0:h0 + HEAD_PAIR]
        if rope:
            partner = jnp.where(first_half, pltpu.roll(c, HEAD_PAIR - half, 1), pltpu.roll(c, half, 1))
            c = c * cos + partner * sin
            swapped = []
            for r0 in range(0, HEAD_PAIR, 2 * half):
                swapped += [ct[r0 + half:r0 + 2 * half], ct[r0:r0 + half]]
            ct = ct * cost + jnp.concatenate(swapped, axis=0) * sint
        k_ref[0, :, h0:h0 + HEAD_PAIR] = c.astype(BF16)
        qt_ref[0, h0:h0 + HEAD_PAIR, :] = (ct * q_scale).astype(BF16)


def _inproj(x, shift, scale, g, w_uk, w_qvt, tables):
    b, l, d = x.shape
    d_pool = d // 2
    d_attn = w_qvt.shape[0] // 2
    t = min(512, l)
    rope = tables is not None
    row = lambda bi, i: (bi, i, 0)
    col = lambda bi, i: (bi, 0, i)
    per_batch = lambda bi, i: (bi, 0, 0)
    const = lambda bi, i: (0, 0)
    in_specs = [
        pl.BlockSpec((1, t, d), row),
        pl.BlockSpec((1, 1, d), per_batch),
        pl.BlockSpec((1, 1, d), per_batch),
        pl.BlockSpec((1, d), const),
        pl.BlockSpec(w_uk.shape, const),
        pl.BlockSpec(w_qvt.shape, const),
    ]
    args = [x, shift, scale, g.reshape(1, d), w_uk, w_qvt]
    if rope:
        in_specs += [pl.BlockSpec((t, HEAD_PAIR), lambda bi, i: (i, 0))] * 2
        in_specs += [pl.BlockSpec((HEAD_PAIR, t), lambda bi, i: (0, i))] * 2
        args += list(tables)
    return pl.pallas_call(
        functools.partial(_inproj_kernel, rope=rope, d_pool=d_pool, d_attn=d_attn,
                          q_scale=HEAD_DIM ** -0.5 * math.log2(math.e)),
        grid=(b, l // t),
        in_specs=in_specs,
        out_specs=[pl.BlockSpec((1, t, d_pool), row), pl.BlockSpec((1, d_attn, t), col),
                   pl.BlockSpec((1, t, d_attn), row), pl.BlockSpec((1, d_attn, t), col)],
        out_shape=[jax.ShapeDtypeStruct((b, l, d_pool), F32), jax.ShapeDtypeStruct((b, d_attn, l), BF16),
                   jax.ShapeDtypeStruct((b, l, d_attn), BF16), jax.ShapeDtypeStruct((b, d_attn, l), BF16)],
        compiler_params=_params(("parallel", "parallel")),
        name="inproj",
    )(*args)


def _attn_kernel(*refs, tq, tk, cb, n_chunks, has_ctx, lam_init):
    if has_ctx:
        lam_ref, qt_ref, k_ref, vt_ref, kc_ref, vct_ref, g_ref, o_ref, qs_scr, m_scr, l_scr, acc_scr = refs
    else:
        lam_ref, qt_ref, k_ref, vt_ref, g_ref, o_ref, qs_scr, m_scr, l_scr, acc_scr = refs
    qt = qt_ref[0]
    sub = lax.broadcasted_iota(jnp.int32, qt.shape, 0)
    zero = jnp.zeros_like(qt)
    qs_scr[:, :tq] = jnp.where(sub < HEAD_DIM, qt, zero)
    qs_scr[:, tq:] = jnp.where(sub >= HEAD_DIM, qt, zero)
    m_scr[...] = jnp.full(m_scr.shape, -jnp.inf, F32)
    l_scr[...] = jnp.zeros(l_scr.shape, F32)
    acc_scr[...] = jnp.zeros(acc_scr.shape, F32)

    def chunk(kj, vtj):
        vta = jnp.concatenate([vtj, jnp.ones((ONES_ROWS, vtj.shape[1]), BF16)], axis=0)

        def scores(c0):
            return jnp.dot(kj, qs_scr[:, c0:c0 + cb], preferred_element_type=F32)

        s_next = scores(0)
        for c0 in range(0, 2 * tq, cb):
            cols = slice(c0, c0 + cb)
            s = s_next
            if c0 + cb < 2 * tq:
                s_next = scores(c0 + cb)
            m_old = m_scr[:, cols]
            m_new = jnp.maximum(m_old, jnp.max(s, axis=0, keepdims=True))
            alpha = jnp.exp2(m_old - m_new)
            p = jnp.exp2(s - m_new).astype(BF16)
            r = jnp.dot(vta, p, preferred_element_type=F32)
            l_scr[:, cols] = alpha * l_scr[:, cols] + r[HEAD_PAIR:HEAD_PAIR + 1]
            acc_scr[:, cols] = alpha * acc_scr[:, cols] + r[:HEAD_PAIR]
            m_scr[:, cols] = m_new

    def body(j, carry):
        start = pl.multiple_of(j * tk, tk)
        chunk(k_ref[0, pl.ds(start, tk), :], vt_ref[0, :, pl.ds(start, tk)])
        return carry

    lax.fori_loop(0, n_chunks, body, 0)
    if has_ctx:
        chunk(kc_ref[0], vct_ref[0])

    o = acc_scr[...] / l_scr[...]
    lv = lam_ref[...]
    lam = (jnp.exp(jnp.sum(lv[0:1] * lv[1:2], axis=1, keepdims=True))
           - jnp.exp(jnp.sum(lv[2:3] * lv[3:4], axis=1, keepdims=True)) + lam_init)
    a = (o[:, :tq] - lam * o[:, tq:]).T
    o_ref[0] = (_rmsnorm(a, g_ref[...]) * (1.0 - lam_init)).astype(BF16)


def _attention(qt, k, vt, kc, vct, lam4, subln_g, lam_init):
    b, d_attn, lq = qt.shape
    lk = k.shape[1]
    n_heads = d_attn // HEAD_PAIR
    tq = min(1024, lq)
    tk = min(512, lk)
    cb = min(512, 2 * tq)
    has_ctx = kc is not None
    qmap = lambda bi, h, i: (bi, h, i)
    kmap = lambda bi, h, i: (bi, 0, h)
    vmap = lambda bi, h, i: (bi, h, 0)
    const = lambda bi, h, i: (0, 0)
    in_specs = [
        pl.BlockSpec(lam4.shape, const),
        pl.BlockSpec((1, HEAD_PAIR, tq), qmap),
        pl.BlockSpec((1, lk, HEAD_PAIR), kmap),
        pl.BlockSpec((1, HEAD_PAIR, lk), vmap),
    ]
    args = [lam4, qt, k, vt]
    if has_ctx:
        lc = kc.shape[1]
        in_specs += [pl.BlockSpec((1, lc, HEAD_PAIR), kmap), pl.BlockSpec((1, HEAD_PAIR, lc), vmap)]
        args += [kc, vct]
    in_specs.append(pl.BlockSpec((1, HEAD_PAIR), const))
    args.append(subln_g.reshape(1, HEAD_PAIR))
    return pl.pallas_call(
        functools.partial(_attn_kernel, tq=tq, tk=tk, cb=cb, n_chunks=lk // tk, has_ctx=has_ctx,
                          lam_init=lam_init),
        grid=(b, n_heads, lq // tq),
        in_specs=in_specs,
        out_specs=pl.BlockSpec((1, tq, HEAD_PAIR), lambda bi, h, i: (bi, i, h)),
        out_shape=jax.ShapeDtypeStruct((b, lq, d_attn), BF16),
        scratch_shapes=[
            pltpu.VMEM((HEAD_PAIR, 2 * tq), BF16),
            pltpu.VMEM((1, 2 * tq), F32),
            pltpu.VMEM((1, 2 * tq), F32),
            pltpu.VMEM((HEAD_PAIR, 2 * tq), F32),
        ],
        compiler_params=_params(("parallel", "parallel", "parallel")),
        name="diff_attn",
    )(*args)


def _route(logits):
    lane = lax.broadcasted_iota(jnp.int32, logits.shape, 1)
    big = jnp.int32(ROUTER_LANES)
    neg = jnp.float32(-jnp.inf)
    lc = jnp.where(lane < N_GROUPS, logits, neg)
    mc = jnp.max(lc, axis=1, keepdims=True)
    p_group = 1.0 / jnp.sum(jnp.exp(lc - mc), axis=1, keepdims=True)
    g_idx = jnp.min(jnp.where(lc == mc, lane, big), axis=1, keepdims=True)
    lo = N_GROUPS + EXPERTS_PER_GROUP * g_idx
    lf = jnp.where((lane >= lo) & (lane < lo + EXPERTS_PER_GROUP), logits, neg)
    v1 = jnp.max(lf, axis=1, keepdims=True)
    i1 = jnp.min(jnp.where(lf == v1, lane, big), axis=1, keepdims=True)
    lf2 = jnp.where(lane == i1, neg, lf)
    v2 = jnp.max(lf2, axis=1, keepdims=True)
    i2 = jnp.min(jnp.where(lf2 == v2, lane, big), axis=1, keepdims=True)
    e2 = jnp.exp(v2 - v1)
    w1 = p_group / (1.0 + e2)
    w2 = p_group * e2 / (1.0 + e2)
    return jnp.where(lane == i1, w1, 0.0) + jnp.where(lane == i2, w2, 0.0)


def _merge_kernel(u_ref, up_ref, un_ref, a_ref, x_ref, pw_ref, ps_ref, wo_ref, g1_ref, n2_ref, sh_ref,
                  sc_ref, wr_ref, rb_ref, xo_ref, h_ref, gate_ref, ext_scr, *, t, seq_len, d_pool):
    i = pl.program_id(1)
    n_tiles = pl.num_programs(1)
    u = u_ref[0]
    ext_scr[0:HALO] = jnp.where(i > 0, up_ref[0], 0.0)
    ext_scr[HALO:HALO + t] = u
    ext_scr[HALO + t:] = jnp.where(i < n_tiles - 1, un_ref[0], 0.0)

    pos = i * t + lax.broadcasted_iota(jnp.int32, (t, 1), 0)
    gc = d_pool // len(POOL_WINDOWS)
    mix = jnp.zeros((t, wo_ref.shape[1]), F32)
    for gi, w in enumerate(POOL_WINDOWS):
        left = w // 2
        right = w - 1 - left
        cols = slice(gi * gc, (gi + 1) * gc)
        s = ext_scr[pl.ds(HALO - left, t), cols]
        for o in range(-left + 1, right + 1):
            s = s + ext_scr[pl.ds(HALO + o, t), cols]
        cnt = (jnp.minimum(pos + right + 1, seq_len) - jnp.maximum(pos - left, 0)).astype(F32)
        pooled = s / cnt - u[:, cols]
        mixed = jnp.dot(pooled.astype(BF16), pw_ref[gi], preferred_element_type=F32) * ps_ref[:, cols]
        mix = mix + jnp.dot(mixed.astype(BF16), wo_ref[cols, :], preferred_element_type=F32)
    mix = mix + jnp.dot(a_ref[0], wo_ref[d_pool:, :], preferred_element_type=F32)

    x_new = x_ref[0] + g1_ref[0] * mix
    xo_ref[0] = x_new
    h2 = _rmsnorm(x_new, n2_ref[...]) * (1.0 + sc_ref[0]) + sh_ref[0]
    h_ref[0] = h2.astype(BF16)
    logits = jnp.dot(h2, wr_ref[...], precision=lax.Precision.HIGHEST,
                     preferred_element_type=F32) + rb_ref[...]
    gate_ref[0] = _route(logits)


def _merge(u, attn, x, pool_w_bf, pool_scale, w_out_bf, g1, norm2_g, sh2, sc2, wr, rb):
    b, l, d = x.shape
    d_pool = u.shape[2]
    t = min(512, l)
    hb = t // HALO
    n_halo = l // HALO
    row = lambda bi, i: (bi, i, 0)
    per_batch = lambda bi, i: (bi, 0, 0)
    const2 = lambda bi, i: (0, 0)
    const3 = lambda bi, i: (0, 0, 0)
    return pl.pallas_call(
        functools.partial(_merge_kernel, t=t, seq_len=l, d_pool=d_pool),
        grid=(b, l // t),
        in_specs=[
            pl.BlockSpec((1, t, d_pool), row),
            pl.BlockSpec((1, HALO, d_pool), lambda bi, i: (bi, jnp.maximum(i * hb - 1, 0), 0)),
            pl.BlockSpec((1, HALO, d_pool), lambda bi, i: (bi, jnp.minimum((i + 1) * hb, n_halo - 1), 0)),
            pl.BlockSpec((1, t, attn.shape[2]), row),
            pl.BlockSpec((1, t, d), row),
            pl.BlockSpec(pool_w_bf.shape, const3),
            pl.BlockSpec((1, d_pool), const2),
            pl.BlockSpec(w_out_bf.shape, const2),
            pl.BlockSpec((1, 1, d), per_batch),
            pl.BlockSpec((1, d), const2),
            pl.BlockSpec((1, 1, d), per_batch),
            pl.BlockSpec((1, 1, d), per_batch),
            pl.BlockSpec(wr.shape, const2),
            pl.BlockSpec((1, ROUTER_LANES), const2),
        ],
        out_specs=[
            pl.BlockSpec((1, t, d), row),
            pl.BlockSpec((1, t, d), row),
            pl.BlockSpec((1, t, ROUTER_LANES), row),
        ],
        out_shape=[
            jax.ShapeDtypeStruct((b, l, d), F32),
            jax.ShapeDtypeStruct((b, l, d), BF16),
            jax.ShapeDtypeStruct((b, l, ROUTER_LANES), F32),
        ],
        scratch_shapes=[pltpu.VMEM((t + 2 * HALO, d_pool), F32)],
        compiler_params=_params(("parallel", "parallel")),
        name="merge",
    )(u, u, u, attn, x, pool_w_bf, pool_scale.reshape(1, d_pool), w_out_bf, g1, norm2_g.reshape(1, d),
      sh2, sc2, wr, rb)


def _moe_kernel(*refs, final_norm):
    if final_norm:
        x_ref, h_ref, gate_ref, g2_ref, wg_ref, wu_ref, wd_ref, fg_ref, o_ref, acc_scr = refs
    else:
        x_ref, h_ref, gate_ref, g2_ref, wg_ref, wu_ref, wd_ref, o_ref, acc_scr = refs
    e = pl.program_id(2)

    @pl.when(e == 0)
    def _():
        acc_scr[...] = jnp.zeros(acc_scr.shape, F32)

    h = h_ref[0]
    a = jnp.dot(h, wg_ref[0, 0], preferred_element_type=F32)
    bu = jnp.dot(h, wu_ref[0, 0], preferred_element_type=F32)
    gates = gate_ref[0]
    lane = lax.broadcasted_iota(jnp.int32, gates.shape, 1)
    gcol = jnp.sum(jnp.where(lane == e + N_GROUPS, gates, 0.0), axis=1, keepdims=True)
    hid = (a * jax.nn.sigmoid(a)) * bu * gcol
    acc_scr[...] += jnp.dot(hid.astype(BF16), wd_ref[0, 0], preferred_element_type=F32)

    @pl.when(e == pl.num_programs(2) - 1)
    def _():
        y = x_ref[0] + g2_ref[0] * acc_scr[...]
        if final_norm:
            y = _rmsnorm(y, fg_ref[...])
        o_ref[0] = y


def _moe(x, h, gates, g2, wg_bf, wu_bf, wd_bf, layer, final_g):
    b, l, d = x.shape
    n_e, f = wg_bf.shape[1], wg_bf.shape[3]
    t = min(1024, l)
    row = lambda bi, i, e: (bi, i, 0)
    per_batch = lambda bi, i, e: (bi, 0, 0)
    final_norm = final_g is not None
    in_specs = [
        pl.BlockSpec((1, t, d), row),
        pl.BlockSpec((1, t, d), row),
        pl.BlockSpec((1, t, ROUTER_LANES), row),
        pl.BlockSpec((1, 1, d), per_batch),
        pl.BlockSpec((1, 1, d, f), lambda bi, i, e: (layer, e, 0, 0)),
        pl.BlockSpec((1, 1, d, f), lambda bi, i, e: (layer, e, 0, 0)),
        pl.BlockSpec((1, 1, f, d), lambda bi, i, e: (layer, e, 0, 0)),
    ]
    args = [x, h, gates, g2, wg_bf, wu_bf, wd_bf]
    if final_norm:
        in_specs.append(pl.BlockSpec((1, d), lambda bi, i, e: (0, 0)))
        args.append(final_g.reshape(1, d))
    return pl.pallas_call(
        functools.partial(_moe_kernel, final_norm=final_norm),
        grid=(b, l // t, n_e),
        in_specs=in_specs,
        out_specs=pl.BlockSpec((1, t, d), row),
        out_shape=jax.ShapeDtypeStruct((b, l, d), F32),
        scratch_shapes=[pltpu.VMEM((t, d), F32)],
        compiler_params=_params(("parallel", "parallel", "arbitrary")),
        name="moe",
    )(*args)


def kernel(x, c, ctx, c_ctx, ada_w, ada_b, norm1_g, norm2_g, w_in, pool_w, pool_scale, lambda_q1, lambda_k1,
           lambda_q2, lambda_k2, subln_g, w_out, router_coarse_w, router_coarse_b, router_fine_w,
           router_fine_b, w_gate, w_up, w_down, final_g):
    b, l, d = x.shape
    depth = ada_w.shape[0]
    assert b + 1 <= MOD_ROWS and l % GRID_W == 0

    tables = _rope_tables(l)
    cc = jnp.zeros((MOD_ROWS, d), F32).at[:b].set(c).at[b].set(c_ctx)
    d_pool = d // 2
    d_attn = (w_in.shape[-1] - d_pool) // 3
    w_uk = jnp.concatenate([w_in[..., :d_pool], w_in[..., d_pool + d_attn:d_pool + 2 * d_attn]],
                           axis=-1).astype(BF16)
    w_qvt = jnp.swapaxes(jnp.concatenate([w_in[..., d_pool:d_pool + d_attn], w_in[..., d_pool + 2 * d_attn:]],
                                         axis=-1), 1, 2).astype(BF16)
    pool_w_bf = pool_w.astype(BF16)
    w_out_bf = w_out.astype(BF16)
    n_e = w_gate.shape[1] * w_gate.shape[2]
    f = w_gate.shape[-1]
    wg_bf = w_gate.astype(BF16).reshape(depth, n_e, d, f)
    wu_bf = w_up.astype(BF16).reshape(depth, n_e, d, f)
    wd_bf = w_down.astype(BF16).reshape(depth, n_e, f, d)
    pad = ROUTER_LANES - N_GROUPS - N_EXPERTS
    wr = jnp.concatenate([router_coarse_w, router_fine_w, jnp.zeros((depth, d, pad), F32)], axis=-1)
    rb = jnp.concatenate([router_coarse_b, router_fine_b, jnp.zeros((depth, pad), F32)], axis=-1)

    xc = ctx
    for layer in range(depth):
        last = layer == depth - 1
        lam_init = 0.8 - 0.6 * math.exp(-0.3 * layer)
        mod = _ada(cc, ada_w, ada_b, layer)
        sh1, sc1, g1, sh2, sc2, g2 = jnp.split(mod[:b, None, :], 6, axis=-1)
        csh1, csc1, cg1, csh2, csc2, cg2 = jnp.split(
            jnp.broadcast_to(mod[b][None, None, :], (b, 1, mod.shape[1])), 6, axis=-1)
        lam4 = jnp.stack([lambda_q1[layer], lambda_k1[layer], lambda_q2[layer], lambda_k2[layer]])

        u, q, k, v = _inproj(x, sh1, sc1, norm1_g[layer], w_uk[layer], w_qvt[layer], tables)
        uc, qc, kc, vc = _inproj(xc, csh1, csc1, norm1_g[layer], w_uk[layer], w_qvt[layer], None)
        attn = _attention(q, k, v, kc, vc, lam4, subln_g[layer], lam_init)
        merge_w = (pool_w_bf[layer], pool_scale[layer], w_out_bf[layer])
        route_w = (wr[layer], rb[layer].reshape(1, ROUTER_LANES))
        x, hx2, gates = _merge(u, attn, x, *merge_w, g1, norm2_g[layer], sh2, sc2, *route_w)
        if not last:
            attn_c = _attention(qc, kc, vc, None, None, lam4, subln_g[layer], lam_init)
            xc, hc2, gates_c = _merge(uc, attn_c, xc, *merge_w, cg1, norm2_g[layer], csh2, csc2, *route_w)
            xc = _moe(xc, hc2, gates_c, cg2, wg_bf, wu_bf, wd_bf, layer, None)
        x = _moe(x, hx2, gates, g2, wg_bf, wu_bf, wd_bf, layer, final_g if last else None)
    return x
```

```python
import functools
import math

import jax
import jax.numpy as jnp
from jax import lax
from jax.experimental import pallas as pl
from jax.experimental.pallas import tpu as pltpu

F32 = jnp.float32
BF16 = jnp.bfloat16

GRID_W = 64
POOL_WINDOWS = (2, 4, 8, 16)
HEAD_DIM = 64
HEAD_PAIR = 2 * HEAD_DIM
ROPE_THETA = 10000.0
N_GROUPS = 4
EXPERTS_PER_GROUP = 8
N_EXPERTS = N_GROUPS * EXPERTS_PER_GROUP
EPS = 1e-6
HALO = 16
ROUTER_LANES = 128
MOD_ROWS = 8
ONES_ROWS = 16

VMEM_LIMIT = 48 * 1024 * 1024


def _params(sem):
    return pltpu.CompilerParams(dimension_semantics=sem, vmem_limit_bytes=VMEM_LIMIT)


def _ada_kernel(cc_ref, w_ref, b_ref, o_ref):
    s = cc_ref[...]
    s = s * jax.nn.sigmoid(s)
    o_ref[...] = jnp.dot(s, w_ref[0], precision=lax.Precision.HIGHEST,
                         preferred_element_type=F32) + b_ref[0]


def _ada(cc, ada_w, ada_b, layer):
    _, d, n = ada_w.shape
    bn = 1024
    return pl.pallas_call(
        _ada_kernel,
        grid=(n // bn,),
        in_specs=[
            pl.BlockSpec((MOD_ROWS, d), lambda j: (0, 0)),
            pl.BlockSpec((1, d, bn), lambda j: (layer, 0, j)),
            pl.BlockSpec((1, 1, bn), lambda j: (layer, 0, j)),
        ],
        out_specs=pl.BlockSpec((MOD_ROWS, bn), lambda j: (0, j)),
        out_shape=jax.ShapeDtypeStruct((MOD_ROWS, n), F32),
        compiler_params=_params(("arbitrary",)),
        name="ada",
    )(cc, ada_w, ada_b.reshape(ada_b.shape[0], 1, n))


def _rope_tables(n):
    pos = jnp.arange(n, dtype=jnp.int32)
    row = (pos // GRID_W).astype(F32)
    col = (pos % GRID_W).astype(F32)
    n_freq = HEAD_DIM // 4
    inv = ROPE_THETA ** (-jnp.arange(n_freq, dtype=F32) / n_freq)
    ang_r = row[:, None] * inv
    ang_c = col[:, None] * inv
    cos = jnp.concatenate([jnp.cos(ang_r), jnp.cos(ang_r), jnp.cos(ang_c), jnp.cos(ang_c)], axis=-1)
    sin = jnp.concatenate([-jnp.sin(ang_r), jnp.sin(ang_r), -jnp.sin(ang_c), jnp.sin(ang_c)], axis=-1)
    cos, sin = jnp.tile(cos, (1, 2)), jnp.tile(sin, (1, 2))
    return cos, sin, cos.T, sin.T


def _rmsnorm(x, g):
    ms = jnp.mean(x * x, axis=-1, keepdims=True)
    return x * lax.rsqrt(ms + EPS) * g


def _inproj_kernel(*refs, rope, d_pool, d_attn, q_scale):
    if rope:
        (x_ref, sh_ref, sc_ref, g_ref, wuk_ref, wqv_ref, cos_ref, sin_ref, cost_ref, sint_ref,
         u_ref, qt_ref, k_ref, vt_ref) = refs
    else:
        x_ref, sh_ref, sc_ref, g_ref, wuk_ref, wqv_ref, u_ref, qt_ref, k_ref, vt_ref = refs
    h = (_rmsnorm(x_ref[0], g_ref[...]) * (1.0 + sc_ref[0]) + sh_ref[0]).astype(BF16)
    p = jnp.dot(h, wuk_ref[...], preferred_element_type=F32)
    pt = lax.dot_general(wqv_ref[...], h, (((1,), (1,)), ((), ())), preferred_element_type=F32)
    u_ref[0] = p[:, :d_pool]
    vt_ref[0] = pt[d_attn:].astype(BF16)
    half = HEAD_DIM // 4
    if rope:
        cos, sin = cos_ref[...], sin_ref[...]
        cost, sint = cost_ref[...], sint_ref[...]
        lane = lax.broadcasted_iota(jnp.int32, cos.shape, 1)
        first_half = (lane % (2 * half)) < half
    for h0 in range(0, d_attn, HEAD_PAIR):
        c = p[:, d_pool + h0: d_pool + h0 + HEAD_PAIR]
        ct = pt[h0:h0 + HEAD_PAIR]
        if rope:
            partner = jnp.where(first_half, pltpu.roll(c, HEAD_PAIR - half, 1), pltpu.roll(c, half, 1))
            c = c * cos + partner * sin
            swapped = []
            for r0 in range(0, HEAD_PAIR, 2 * half):
                swapped += [ct[r0 + half:r0 + 2 * half], ct[r0:r0 + half]]
            ct = ct * cost + jnp.concatenate(swapped, axis=0) * sint
        k_ref[0, :, h0:h0 + HEAD_PAIR] = c.astype(BF16)
        qt_ref[0, h0:h0 + HEAD_PAIR, :] = (ct * q_scale).astype(BF16)


def _inproj(x, shift, scale, g, w_uk, w_qvt, tables):
    b, l, d = x.shape
    d_pool = d // 2
    d_attn = w_qvt.shape[0] // 2
    t = min(512, l)
    rope = tables is not None
    row = lambda bi, i: (bi, i, 0)
    col = lambda bi, i: (bi, 0, i)
    per_batch = lambda bi, i: (bi, 0, 0)
    const = lambda bi, i: (0, 0)
    in_specs = [
        pl.BlockSpec((1, t, d), row),
        pl.BlockSpec((1, 1, d), per_batch),
        pl.BlockSpec((1, 1, d), per_batch),
        pl.BlockSpec((1, d), const),
        pl.BlockSpec(w_uk.shape, const),
        pl.BlockSpec(w_qvt.shape, const),
    ]
    args = [x, shift, scale, g.reshape(1, d), w_uk, w_qvt]
    if rope:
        in_specs += [pl.BlockSpec((t, HEAD_PAIR), lambda bi, i: (i, 0))] * 2
        in_specs += [pl.BlockSpec((HEAD_PAIR, t), lambda bi, i: (0, i))] * 2
        args += list(tables)
    return pl.pallas_call(
        functools.partial(_inproj_kernel, rope=rope, d_pool=d_pool, d_attn=d_attn,
                          q_scale=HEAD_DIM ** -0.5 * math.log2(math.e)),
        grid=(b, l // t),
        in_specs=in_specs,
        out_specs=[pl.BlockSpec((1, t, d_pool), row), pl.BlockSpec((1, d_attn, t), col),
                   pl.BlockSpec((1, t, d_attn), row), pl.BlockSpec((1, d_attn, t), col)],
        out_shape=[jax.ShapeDtypeStruct((b, l, d_pool), F32), jax.ShapeDtypeStruct((b, d_attn, l), BF16),
                   jax.ShapeDtypeStruct((b, l, d_attn), BF16), jax.ShapeDtypeStruct((b, d_attn, l), BF16)],
        compiler_params=_params(("parallel", "parallel")),
        name="inproj",
    )(*args)


def _attn_kernel(*refs, tq, tk, cb, n_chunks, has_ctx, lam_init):
    if has_ctx:
        (lam_ref, qt_ref, k_ref, vt_ref, kc_ref, vct_ref, g_ref, o_ref,
         qs_scr, m_scr, l_scr, acc_scr, s_scr) = refs
    else:
        lam_ref, qt_ref, k_ref, vt_ref, g_ref, o_ref, qs_scr, m_scr, l_scr, acc_scr, s_scr = refs
    qt = qt_ref[0]
    sub = lax.broadcasted_iota(jnp.int32, qt.shape, 0)
    zero = jnp.zeros_like(qt)
    qs_scr[:, :tq] = jnp.where(sub < HEAD_DIM, qt, zero)
    qs_scr[:, tq:] = jnp.where(sub >= HEAD_DIM, qt, zero)
    m_scr[...] = jnp.full(m_scr.shape, -jnp.inf, F32)
    l_scr[...] = jnp.zeros(l_scr.shape, F32)
    acc_scr[...] = jnp.zeros(acc_scr.shape, F32)

    def scores(kj, c0):
        return jnp.dot(kj, qs_scr[:, c0:c0 + cb], preferred_element_type=F32)

    def chunk(kj, vtj, k_after):
        vta = jnp.concatenate([vtj, jnp.ones((ONES_ROWS, vtj.shape[1]), BF16)], axis=0)
        n_keys = kj.shape[0]
        s_next = s_scr[0:n_keys, :]
        for c0 in range(0, 2 * tq, cb):
            cols = slice(c0, c0 + cb)
            s = s_next
            if c0 + cb < 2 * tq:
                s_next = scores(kj, c0 + cb)
            elif k_after is not None:
                s_scr[0:k_after.shape[0], :] = scores(k_after, 0)
            m_old = m_scr[:, cols]
            m_new = jnp.maximum(m_old, jnp.max(s, axis=0, keepdims=True))
            alpha = jnp.exp2(m_old - m_new)
            p = jnp.exp2(s - m_new).astype(BF16)
            r = jnp.dot(vta, p, preferred_element_type=F32)
            l_scr[:, cols] = alpha * l_scr[:, cols] + r[HEAD_PAIR:HEAD_PAIR + 1]
            acc_scr[:, cols] = alpha * acc_scr[:, cols] + r[:HEAD_PAIR]
            m_scr[:, cols] = m_new

    def span(j):
        return pl.ds(j * tk if isinstance(j, int) else pl.multiple_of(j * tk, tk), tk)

    def keys(j):
        return k_ref[0, span(j), :]

    def values_t(j):
        return vt_ref[0, :, span(j)]

    s_scr[...] = scores(keys(0), 0)
    if n_chunks > 1:
        def body(j, carry):
            chunk(keys(j), values_t(j), keys(j + 1))
            return carry

        lax.fori_loop(0, n_chunks - 1, body, 0)
    last = n_chunks - 1
    chunk(keys(last), values_t(last), kc_ref[0] if has_ctx else None)
    if has_ctx:
        chunk(kc_ref[0], vct_ref[0], None)

    o = acc_scr[...] / l_scr[...]
    lv = lam_ref[...]
    lam = (jnp.exp(jnp.sum(lv[0:1] * lv[1:2], axis=1, keepdims=True))
           - jnp.exp(jnp.sum(lv[2:3] * lv[3:4], axis=1, keepdims=True)) + lam_init)
    a = (o[:, :tq] - lam * o[:, tq:]).T
    o_ref[0] = (_rmsnorm(a, g_ref[...]) * (1.0 - lam_init)).astype(BF16)


def _attention(qt, k, vt, kc, vct, lam4, subln_g, lam_init):
    b, d_attn, lq = qt.shape
    lk = k.shape[1]
    n_heads = d_attn // HEAD_PAIR
    tq = min(1024, lq)
    tk = min(512, lk)
    cb = min(512, 2 * tq)
    has_ctx = kc is not None
    qmap = lambda bi, h, i: (bi, h, i)
    kmap = lambda bi, h, i: (bi, 0, h)
    vmap = lambda bi, h, i: (bi, h, 0)
    const = lambda bi, h, i: (0, 0)
    in_specs = [
        pl.BlockSpec(lam4.shape, const),
        pl.BlockSpec((1, HEAD_PAIR, tq), qmap),
        pl.BlockSpec((1, lk, HEAD_PAIR), kmap),
        pl.BlockSpec((1, HEAD_PAIR, lk), vmap),
    ]
    args = [lam4, qt, k, vt]
    if has_ctx:
        lc = kc.shape[1]
        in_specs += [pl.BlockSpec((1, lc, HEAD_PAIR), kmap), pl.BlockSpec((1, HEAD_PAIR, lc), vmap)]
        args += [kc, vct]
    in_specs.append(pl.BlockSpec((1, HEAD_PAIR), const))
    args.append(subln_g.reshape(1, HEAD_PAIR))
    return pl.pallas_call(
        functools.partial(_attn_kernel, tq=tq, tk=tk, cb=cb, n_chunks=lk // tk, has_ctx=has_ctx,
                          lam_init=lam_init),
        grid=(b, n_heads, lq // tq),
        in_specs=in_specs,
        out_specs=pl.BlockSpec((1, tq, HEAD_PAIR), lambda bi, h, i: (bi, i, h)),
        out_shape=jax.ShapeDtypeStruct((b, lq, d_attn), BF16),
        scratch_shapes=[
            pltpu.VMEM((HEAD_PAIR, 2 * tq), BF16),
            pltpu.VMEM((1, 2 * tq), F32),
            pltpu.VMEM((1, 2 * tq), F32),
            pltpu.VMEM((HEAD_PAIR, 2 * tq), F32),
            pltpu.VMEM((tk, cb), F32),
        ],
        compiler_params=_params(("parallel", "parallel", "parallel")),
        name="diff_attn",
    )(*args)


def _route(logits):
    lane = lax.broadcasted_iota(jnp.int32, logits.shape, 1)
    big = jnp.int32(ROUTER_LANES)
    neg = jnp.float32(-jnp.inf)
    lc = jnp.where(lane < N_GROUPS, logits, neg)
    mc = jnp.max(lc, axis=1, keepdims=True)
    p_group = 1.0 / jnp.sum(jnp.exp(lc - mc), axis=1, keepdims=True)
    g_idx = jnp.min(jnp.where(lc == mc, lane, big), axis=1, keepdims=True)
    lo = N_GROUPS + EXPERTS_PER_GROUP * g_idx
    lf = jnp.where((lane >= lo) & (lane < lo + EXPERTS_PER_GROUP), logits, neg)
    v1 = jnp.max(lf, axis=1, keepdims=True)
    i1 = jnp.min(jnp.where(lf == v1, lane, big), axis=1, keepdims=True)
    lf2 = jnp.where(lane == i1, neg, lf)
    v2 = jnp.max(lf2, axis=1, keepdims=True)
    i2 = jnp.min(jnp.where(lf2 == v2, lane, big), axis=1, keepdims=True)
    e2 = jnp.exp(v2 - v1)
    w1 = p_group / (1.0 + e2)
    w2 = p_group * e2 / (1.0 + e2)
    return jnp.where(lane == i1, w1, 0.0) + jnp.where(lane == i2, w2, 0.0)


def _merge_kernel(u_ref, up_ref, un_ref, a_ref, x_ref, pw_ref, ps_ref, wo_ref, g1_ref, n2_ref, sh_ref,
                  sc_ref, wr_ref, rb_ref, xo_ref, h_ref, gate_ref, ext_scr, *, t, seq_len, d_pool):
    i = pl.program_id(1)
    n_tiles = pl.num_programs(1)
    u = u_ref[0]
    ext_scr[0:HALO] = jnp.where(i > 0, up_ref[0], 0.0)
    ext_scr[HALO:HALO + t] = u
    ext_scr[HALO + t:] = jnp.where(i < n_tiles - 1, un_ref[0], 0.0)

    pos = i * t + lax.broadcasted_iota(jnp.int32, (t, 1), 0)
    gc = d_pool // len(POOL_WINDOWS)
    mix = jnp.zeros((t, wo_ref.shape[1]), F32)
    for gi, w in enumerate(POOL_WINDOWS):
        left = w // 2
        right = w - 1 - left
        cols = slice(gi * gc, (gi + 1) * gc)
        s = ext_scr[pl.ds(HALO - left, t), cols]
        for o in range(-left + 1, right + 1):
            s = s + ext_scr[pl.ds(HALO + o, t), cols]
        cnt = (jnp.minimum(pos + right + 1, seq_len) - jnp.maximum(pos - left, 0)).astype(F32)
        pooled = s / cnt - u[:, cols]
        mixed = jnp.dot(pooled.astype(BF16), pw_ref[gi], preferred_element_type=F32) * ps_ref[:, cols]
        mix = mix + jnp.dot(mixed.astype(BF16), wo_ref[cols, :], preferred_element_type=F32)
    mix = mix + jnp.dot(a_ref[0], wo_ref[d_pool:, :], preferred_element_type=F32)

    x_new = x_ref[0] + g1_ref[0] * mix
    xo_ref[0] = x_new
    h2 = _rmsnorm(x_new, n2_ref[...]) * (1.0 + sc_ref[0]) + sh_ref[0]
    h_ref[0] = h2.astype(BF16)
    logits = jnp.dot(h2, wr_ref[...], precision=lax.Precision.HIGHEST,
                     preferred_element_type=F32) + rb_ref[...]
    gate_ref[0] = _route(logits)


def _merge(u, attn, x, pool_w_bf, pool_scale, w_out_bf, g1, norm2_g, sh2, sc2, wr, rb):
    b, l, d = x.shape
    d_pool = u.shape[2]
    t = min(512, l)
    hb = t // HALO
    n_halo = l // HALO
    row = lambda bi, i: (bi, i, 0)
    per_batch = lambda bi, i: (bi, 0, 0)
    const2 = lambda bi, i: (0, 0)
    const3 = lambda bi, i: (0, 0, 0)
    return pl.pallas_call(
        functools.partial(_merge_kernel, t=t, seq_len=l, d_pool=d_pool),
        grid=(b, l // t),
        in_specs=[
            pl.BlockSpec((1, t, d_pool), row),
            pl.BlockSpec((1, HALO, d_pool), lambda bi, i: (bi, jnp.maximum(i * hb - 1, 0), 0)),
            pl.BlockSpec((1, HALO, d_pool), lambda bi, i: (bi, jnp.minimum((i + 1) * hb, n_halo - 1), 0)),
            pl.BlockSpec((1, t, attn.shape[2]), row),
            pl.BlockSpec((1, t, d), row),
            pl.BlockSpec(pool_w_bf.shape, const3),
            pl.BlockSpec((1, d_pool), const2),
            pl.BlockSpec(w_out_bf.shape, const2),
            pl.BlockSpec((1, 1, d), per_batch),
            pl.BlockSpec((1, d), const2),
            pl.BlockSpec((1, 1, d), per_batch),
            pl.BlockSpec((1, 1, d), per_batch),
            pl.BlockSpec(wr.shape, const2),
            pl.BlockSpec((1, ROUTER_LANES), const2),
        ],
        out_specs=[
            pl.BlockSpec((1, t, d), row),
            pl.BlockSpec((1, t, d), row),
            pl.BlockSpec((1, t, ROUTER_LANES), row),
        ],
        out_shape=[
            jax.ShapeDtypeStruct((b, l, d), F32),
            jax.ShapeDtypeStruct((b, l, d), BF16),
            jax.ShapeDtypeStruct((b, l, ROUTER_LANES), F32),
        ],
        scratch_shapes=[pltpu.VMEM((t + 2 * HALO, d_pool), F32)],
        compiler_params=_params(("parallel", "parallel")),
        name="merge",
    )(u, u, u, attn, x, pool_w_bf, pool_scale.reshape(1, d_pool), w_out_bf, g1, norm2_g.reshape(1, d),
      sh2, sc2, wr, rb)


def _moe_kernel(*refs, final_norm):
    if final_norm:
        x_ref, h_ref, gate_ref, g2_ref, wg_ref, wu_ref, wd_ref, fg_ref, o_ref, acc_scr = refs
    else:
        x_ref, h_ref, gate_ref, g2_ref, wg_ref, wu_ref, wd_ref, o_ref, acc_scr = refs
    e = pl.program_id(2)

    @pl.when(e == 0)
    def _():
        acc_scr[...] = jnp.zeros(acc_scr.shape, F32)

    h = h_ref[0]
    a = jnp.dot(h, wg_ref[0, 0], preferred_element_type=F32)
    bu = jnp.dot(h, wu_ref[0, 0], preferred_element_type=F32)
    gates = gate_ref[0]
    lane = lax.broadcasted_iota(jnp.int32, gates.shape, 1)
    gcol = jnp.sum(jnp.where(lane == e + N_GROUPS, gates, 0.0), axis=1, keepdims=True)
    hid = (a * jax.nn.sigmoid(a)) * bu * gcol
    acc_scr[...] += jnp.dot(hid.astype(BF16), wd_ref[0, 0], preferred_element_type=F32)

    @pl.when(e == pl.num_programs(2) - 1)
    def _():
        y = x_ref[0] + g2_ref[0] * acc_scr[...]
        if final_norm:
            y = _rmsnorm(y, fg_ref[...])
        o_ref[0] = y


def _moe(x, h, gates, g2, wg_bf, wu_bf, wd_bf, layer, final_g):
    b, l, d = x.shape
    n_e, f = wg_bf.shape[1], wg_bf.shape[3]
    t = min(1024, l)
    row = lambda bi, i, e: (bi, i, 0)
    per_batch = lambda bi, i, e: (bi, 0, 0)
    final_norm = final_g is not None
    in_specs = [
        pl.BlockSpec((1, t, d), row),
        pl.BlockSpec((1, t, d), row),
        pl.BlockSpec((1, t, ROUTER_LANES), row),
        pl.BlockSpec((1, 1, d), per_batch),
        pl.BlockSpec((1, 1, d, f), lambda bi, i, e: (layer, e, 0, 0)),
        pl.BlockSpec((1, 1, d, f), lambda bi, i, e: (layer, e, 0, 0)),
        pl.BlockSpec((1, 1, f, d), lambda bi, i, e: (layer, e, 0, 0)),
    ]
    args = [x, h, gates, g2, wg_bf, wu_bf, wd_bf]
    if final_norm:
        in_specs.append(pl.BlockSpec((1, d), lambda bi, i, e: (0, 0)))
        args.append(final_g.reshape(1, d))
    return pl.pallas_call(
        functools.partial(_moe_kernel, final_norm=final_norm),
        grid=(b, l // t, n_e),
        in_specs=in_specs,
        out_specs=pl.BlockSpec((1, t, d), row),
        out_shape=jax.ShapeDtypeStruct((b, l, d), F32),
        scratch_shapes=[pltpu.VMEM((t, d), F32)],
        compiler_params=_params(("parallel", "parallel", "arbitrary")),
        name="moe",
    )(*args)


def kernel(x, c, ctx, c_ctx, ada_w, ada_b, norm1_g, norm2_g, w_in, pool_w, pool_scale, lambda_q1, lambda_k1,
           lambda_q2, lambda_k2, subln_g, w_out, router_coarse_w, router_coarse_b, router_fine_w,
           router_fine_b, w_gate, w_up, w_down, final_g):
    b, l, d = x.shape
    depth = ada_w.shape[0]
    assert b + 1 <= MOD_ROWS and l % GRID_W == 0

    tables = _rope_tables(l)
    cc = jnp.zeros((MOD_ROWS, d), F32).at[:b].set(c).at[b].set(c_ctx)
    d_pool = d // 2
    d_attn = (w_in.shape[-1] - d_pool) // 3
    w_uk = jnp.concatenate([w_in[..., :d_pool], w_in[..., d_pool + d_attn:d_pool + 2 * d_attn]],
                           axis=-1).astype(BF16)
    w_qvt = jnp.swapaxes(jnp.concatenate([w_in[..., d_pool:d_pool + d_attn], w_in[..., d_pool + 2 * d_attn:]],
                                         axis=-1), 1, 2).astype(BF16)
    pool_w_bf = pool_w.astype(BF16)
    w_out_bf = w_out.astype(BF16)
    n_e = w_gate.shape[1] * w_gate.shape[2]
    f = w_gate.shape[-1]
    wg_bf = w_gate.astype(BF16).reshape(depth, n_e, d, f)
    wu_bf = w_up.astype(BF16).reshape(depth, n_e, d, f)
    wd_bf = w_down.astype(BF16).reshape(depth, n_e, f, d)
    pad = ROUTER_LANES - N_GROUPS - N_EXPERTS
    wr = jnp.concatenate([router_coarse_w, router_fine_w, jnp.zeros((depth, d, pad), F32)], axis=-1)
    rb = jnp.concatenate([router_coarse_b, router_fine_b, jnp.zeros((depth, pad), F32)], axis=-1)

    xc = ctx
    for layer in range(depth):
        last = layer == depth - 1
        lam_init = 0.8 - 0.6 * math.exp(-0.3 * layer)
        mod = _ada(cc, ada_w, ada_b, layer)
        sh1, sc1, g1, sh2, sc2, g2 = jnp.split(mod[:b, None, :], 6, axis=-1)
        csh1, csc1, cg1, csh2, csc2, cg2 = jnp.split(
            jnp.broadcast_to(mod[b][None, None, :], (b, 1, mod.shape[1])), 6, axis=-1)
        lam4 = jnp.stack([lambda_q1[layer], lambda_k1[layer], lambda_q2[layer], lambda_k2[layer]])

        u, q, k, v = _inproj(x, sh1, sc1, norm1_g[layer], w_uk[layer], w_qvt[layer], tables)
        uc, qc, kc, vc = _inproj(xc, csh1, csc1, norm1_g[layer], w_uk[layer], w_qvt[layer], None)
        attn = _attention(q, k, v, kc, vc, lam4, subln_g[layer], lam_init)
        merge_w = (pool_w_bf[layer], pool_scale[layer], w_out_bf[layer])
        route_w = (wr[layer], rb[layer].reshape(1, ROUTER_LANES))
        x, hx2, gates = _merge(u, attn, x, *merge_w, g1, norm2_g[layer], sh2, sc2, *route_w)
        if not last:
            attn_c = _attention(qc, kc, vc, None, None, lam4, subln_g[layer], lam_init)
            xc, hc2, gates_c = _merge(uc, attn_c, xc, *merge_w, cg1, norm2_g[layer], csh2, csc2, *route_w)
            xc = _moe(xc, hc2, gates_c, cg2, wg_bf, wu_bf, wd_bf, layer, None)
        x = _moe(x, hx2, gates, g2, wg_bf, wu_bf, wd_bf, layer, final_g if last else None)
    return x
```

```python
import functools
import math

import jax
import jax.numpy as jnp
from jax import lax
from jax.experimental import pallas as pl
from jax.experimental.pallas import tpu as pltpu

F32 = jnp.float32
BF16 = jnp.bfloat16

GRID_W = 64
POOL_WINDOWS = (2, 4, 8, 16)
HEAD_DIM = 64
HEAD_PAIR = 2 * HEAD_DIM
ROPE_THETA = 10000.0
N_GROUPS = 4
EXPERTS_PER_GROUP = 8
N_EXPERTS = N_GROUPS * EXPERTS_PER_GROUP
EPS = 1e-6
HALO = 16
ROUTER_LANES = 128
MOD_ROWS = 8
ONES_ROWS = 16
SLOT_ALIGN = 16
MOE_TOKEN_TILE = 512
MOE_ROW_TILE = 256

VMEM_LIMIT = 48 * 1024 * 1024


def _params(sem):
    return pltpu.CompilerParams(dimension_semantics=sem, vmem_limit_bytes=VMEM_LIMIT)


def _ada_kernel(cc_ref, w_ref, b_ref, o_ref):
    s = cc_ref[...]
    s = s * jax.nn.sigmoid(s)
    o_ref[...] = jnp.dot(s, w_ref[0], precision=lax.Precision.HIGHEST,
                         preferred_element_type=F32) + b_ref[0]


def _ada(cc, ada_w, ada_b, layer):
    _, d, n = ada_w.shape
    bn = 1024
    return pl.pallas_call(
        _ada_kernel,
        grid=(n // bn,),
        in_specs=[
            pl.BlockSpec((MOD_ROWS, d), lambda j: (0, 0)),
            pl.BlockSpec((1, d, bn), lambda j: (layer, 0, j)),
            pl.BlockSpec((1, 1, bn), lambda j: (layer, 0, j)),
        ],
        out_specs=pl.BlockSpec((MOD_ROWS, bn), lambda j: (0, j)),
        out_shape=jax.ShapeDtypeStruct((MOD_ROWS, n), F32),
        compiler_params=_params(("arbitrary",)),
        name="ada",
    )(cc, ada_w, ada_b.reshape(ada_b.shape[0], 1, n))


def _rope_tables(n):
    pos = jnp.arange(n, dtype=jnp.int32)
    row = (pos // GRID_W).astype(F32)
    col = (pos % GRID_W).astype(F32)
    n_freq = HEAD_DIM // 4
    inv = ROPE_THETA ** (-jnp.arange(n_freq, dtype=F32) / n_freq)
    ang_r = row[:, None] * inv
    ang_c = col[:, None] * inv
    cos = jnp.concatenate([jnp.cos(ang_r), jnp.cos(ang_r), jnp.cos(ang_c), jnp.cos(ang_c)], axis=-1)
    sin = jnp.concatenate([-jnp.sin(ang_r), jnp.sin(ang_r), -jnp.sin(ang_c), jnp.sin(ang_c)], axis=-1)
    cos, sin = jnp.tile(cos, (1, 2)), jnp.tile(sin, (1, 2))
    return cos, sin, cos.T, sin.T


def _rmsnorm(x, g):
    ms = jnp.mean(x * x, axis=-1, keepdims=True)
    return x * lax.rsqrt(ms + EPS) * g


def _inproj_kernel(*refs, rope, d_pool, d_attn, q_scale):
    if rope:
        (x_ref, sh_ref, sc_ref, g_ref, wuk_ref, wqv_ref, cos_ref, sin_ref, cost_ref, sint_ref,
         u_ref, qt_ref, k_ref, vt_ref) = refs
    else:
        x_ref, sh_ref, sc_ref, g_ref, wuk_ref, wqv_ref, u_ref, qt_ref, k_ref, vt_ref = refs
    h = (_rmsnorm(x_ref[0], g_ref[...]) * (1.0 + sc_ref[0]) + sh_ref[0]).astype(BF16)
    p = jnp.dot(h, wuk_ref[...], preferred_element_type=F32)
    pt = lax.dot_general(wqv_ref[...], h, (((1,), (1,)), ((), ())), preferred_element_type=F32)
    u_ref[0] = p[:, :d_pool]
    vt_ref[0] = pt[d_attn:].astype(BF16)
    half = HEAD_DIM // 4
    if rope:
        cos, sin = cos_ref[...], sin_ref[...]
        cost, sint = cost_ref[...], sint_ref[...]
        lane = lax.broadcasted_iota(jnp.int32, cos.shape, 1)
        first_half = (lane % (2 * half)) < half
    for h0 in range(0, d_attn, HEAD_PAIR):
        c = p[:, d_pool + h0: d_pool + h0 + HEAD_PAIR]
        ct = pt[h0:h0 + HEAD_PAIR]
        if rope:
            partner = jnp.where(first_half, pltpu.roll(c, HEAD_PAIR - half, 1), pltpu.roll(c, half, 1))
            c = c * cos + partner * sin
            swapped = []
            for r0 in range(0, HEAD_PAIR, 2 * half):
                swapped += [ct[r0 + half:r0 + 2 * half], ct[r0:r0 + half]]
            ct = ct * cost + jnp.concatenate(swapped, axis=0) * sint
        k_ref[0, :, h0:h0 + HEAD_PAIR] = c.astype(BF16)
        qt_ref[0, h0:h0 + HEAD_PAIR, :] = (ct * q_scale).astype(BF16)


def _inproj(x, shift, scale, g, w_uk, w_qvt, tables):
    b, l, d = x.shape
    d_pool = d // 2
    d_attn = w_qvt.shape[0] // 2
    t = min(512, l)
    rope = tables is not None
    row = lambda bi, i: (bi, i, 0)
    col = lambda bi, i: (bi, 0, i)
    per_batch = lambda bi, i: (bi, 0, 0)
    const = lambda bi, i: (0, 0)
    in_specs = [
        pl.BlockSpec((1, t, d), row),
        pl.BlockSpec((1, 1, d), per_batch),
        pl.BlockSpec((1, 1, d), per_batch),
        pl.BlockSpec((1, d), const),
        pl.BlockSpec(w_uk.shape, const),
        pl.BlockSpec(w_qvt.shape, const),
    ]
    args = [x, shift, scale, g.reshape(1, d), w_uk, w_qvt]
    if rope:
        in_specs += [pl.BlockSpec((t, HEAD_PAIR), lambda bi, i: (i, 0))] * 2
        in_specs += [pl.BlockSpec((HEAD_PAIR, t), lambda bi, i: (0, i))] * 2
        args += list(tables)
    return pl.pallas_call(
        functools.partial(_inproj_kernel, rope=rope, d_pool=d_pool, d_attn=d_attn,
                          q_scale=HEAD_DIM ** -0.5 * math.log2(math.e)),
        grid=(b, l // t),
        in_specs=in_specs,
        out_specs=[pl.BlockSpec((1, t, d_pool), row), pl.BlockSpec((1, d_attn, t), col),
                   pl.BlockSpec((1, t, d_attn), row), pl.BlockSpec((1, d_attn, t), col)],
        out_shape=[jax.ShapeDtypeStruct((b, l, d_pool), F32), jax.ShapeDtypeStruct((b, d_attn, l), BF16),
                   jax.ShapeDtypeStruct((b, l, d_attn), BF16), jax.ShapeDtypeStruct((b, d_attn, l), BF16)],
        compiler_params=_params(("arbitrary", "arbitrary")),
        name="inproj",
    )(*args)


def _attn_kernel(*refs, tq, tk, cb, n_chunks, has_ctx, lam_init):
    if has_ctx:
        (lam_ref, qt_ref, k_ref, vt_ref, kc_ref, vct_ref, g_ref, o_ref,
         qs_scr, m_scr, l_scr, acc_scr, s_scr) = refs
    else:
        lam_ref, qt_ref, k_ref, vt_ref, g_ref, o_ref, qs_scr, m_scr, l_scr, acc_scr, s_scr = refs
    qt = qt_ref[0]
    sub = lax.broadcasted_iota(jnp.int32, qt.shape, 0)
    zero = jnp.zeros_like(qt)
    qs_scr[:, :tq] = jnp.where(sub < HEAD_DIM, qt, zero)
    qs_scr[:, tq:] = jnp.where(sub >= HEAD_DIM, qt, zero)
    m_scr[...] = jnp.full(m_scr.shape, -jnp.inf, F32)
    l_scr[...] = jnp.zeros(l_scr.shape, F32)
    acc_scr[...] = jnp.zeros(acc_scr.shape, F32)

    def scores(kj, c0):
        return jnp.dot(kj, qs_scr[:, c0:c0 + cb], preferred_element_type=F32)

    def chunk(kj, vtj, k_after):
        vta = jnp.concatenate([vtj, jnp.ones((ONES_ROWS, vtj.shape[1]), BF16)], axis=0)
        n_keys = kj.shape[0]
        s_next = s_scr[0:n_keys, :]
        for c0 in range(0, 2 * tq, cb):
            cols = slice(c0, c0 + cb)
            s = s_next
            if c0 + cb < 2 * tq:
                s_next = scores(kj, c0 + cb)
            elif k_after is not None:
                s_scr[0:k_after.shape[0], :] = scores(k_after, 0)
            m_old = m_scr[:, cols]
            m_new = jnp.maximum(m_old, jnp.max(s, axis=0, keepdims=True))
            alpha = jnp.exp2(m_old - m_new)
            p = jnp.exp2(s - m_new).astype(BF16)
            r = jnp.dot(vta, p, preferred_element_type=F32)
            l_scr[:, cols] = alpha * l_scr[:, cols] + r[HEAD_PAIR:HEAD_PAIR + 1]
            acc_scr[:, cols] = alpha * acc_scr[:, cols] + r[:HEAD_PAIR]
            m_scr[:, cols] = m_new

    def span(j):
        return pl.ds(j * tk if isinstance(j, int) else pl.multiple_of(j * tk, tk), tk)

    def keys(j):
        return k_ref[0, span(j), :]

    def values_t(j):
        return vt_ref[0, :, span(j)]

    s_scr[...] = scores(keys(0), 0)
    if n_chunks > 1:
        def body(j, carry):
            chunk(keys(j), values_t(j), keys(j + 1))
            return carry

        lax.fori_loop(0, n_chunks - 1, body, 0, unroll=4)
    last = n_chunks - 1
    chunk(keys(last), values_t(last), kc_ref[0] if has_ctx else None)
    if has_ctx:
        chunk(kc_ref[0], vct_ref[0], None)

    o = acc_scr[...] / l_scr[...]
    lv = lam_ref[...]
    lam = (jnp.exp(jnp.sum(lv[0:1] * lv[1:2], axis=1, keepdims=True))
           - jnp.exp(jnp.sum(lv[2:3] * lv[3:4], axis=1, keepdims=True)) + lam_init)
    a = (o[:, :tq] - lam * o[:, tq:]).T
    o_ref[0] = (_rmsnorm(a, g_ref[...]) * (1.0 - lam_init)).astype(BF16)


def _attention(qt, k, vt, kc, vct, lam4, subln_g, lam_init):
    b, d_attn, lq = qt.shape
    lk = k.shape[1]
    n_heads = d_attn // HEAD_PAIR
    tq = min(1024, lq)
    tk = min(512, lk)
    cb = min(512, 2 * tq)
    has_ctx = kc is not None
    qmap = lambda bi, h, i: (bi, h, i)
    kmap = lambda bi, h, i: (bi, 0, h)
    vmap = lambda bi, h, i: (bi, h, 0)
    const = lambda bi, h, i: (0, 0)
    in_specs = [
        pl.BlockSpec(lam4.shape, const),
        pl.BlockSpec((1, HEAD_PAIR, tq), qmap),
        pl.BlockSpec((1, lk, HEAD_PAIR), kmap),
        pl.BlockSpec((1, HEAD_PAIR, lk), vmap),
    ]
    args = [lam4, qt, k, vt]
    if has_ctx:
        lc = kc.shape[1]
        in_specs += [pl.BlockSpec((1, lc, HEAD_PAIR), kmap), pl.BlockSpec((1, HEAD_PAIR, lc), vmap)]
        args += [kc, vct]
    in_specs.append(pl.BlockSpec((1, HEAD_PAIR), const))
    args.append(subln_g.reshape(1, HEAD_PAIR))
    return pl.pallas_call(
        functools.partial(_attn_kernel, tq=tq, tk=tk, cb=cb, n_chunks=lk // tk, has_ctx=has_ctx,
                          lam_init=lam_init),
        grid=(b, n_heads, lq // tq),
        in_specs=in_specs,
        out_specs=pl.BlockSpec((1, tq, HEAD_PAIR), lambda bi, h, i: (bi, i, h)),
        out_shape=jax.ShapeDtypeStruct((b, lq, d_attn), BF16),
        scratch_shapes=[
            pltpu.VMEM((HEAD_PAIR, 2 * tq), BF16),
            pltpu.VMEM((1, 2 * tq), F32),
            pltpu.VMEM((1, 2 * tq), F32),
            pltpu.VMEM((HEAD_PAIR, 2 * tq), F32),
            pltpu.VMEM((tk, cb), F32),
        ],
        compiler_params=_params(("arbitrary", "arbitrary", "arbitrary")),
        name="diff_attn",
    )(*args)


def _route(logits):
    lane = lax.broadcasted_iota(jnp.int32, logits.shape, 1)
    big = jnp.int32(ROUTER_LANES)
    neg = jnp.float32(-jnp.inf)
    lc = jnp.where(lane < N_GROUPS, logits, neg)
    mc = jnp.max(lc, axis=1, keepdims=True)
    p_group = 1.0 / jnp.sum(jnp.exp(lc - mc), axis=1, keepdims=True)
    g_idx = jnp.min(jnp.where(lc == mc, lane, big), axis=1, keepdims=True)
    lo = N_GROUPS + EXPERTS_PER_GROUP * g_idx
    lf = jnp.where((lane >= lo) & (lane < lo + EXPERTS_PER_GROUP), logits, neg)
    v1 = jnp.max(lf, axis=1, keepdims=True)
    i1 = jnp.min(jnp.where(lf == v1, lane, big), axis=1, keepdims=True)
    lf2 = jnp.where(lane == i1, neg, lf)
    v2 = jnp.max(lf2, axis=1, keepdims=True)
    i2 = jnp.min(jnp.where(lf2 == v2, lane, big), axis=1, keepdims=True)
    e2 = jnp.exp(v2 - v1)
    w1 = p_group / (1.0 + e2)
    w2 = p_group * e2 / (1.0 + e2)
    return lane, i1, i2, w1, w2


def _split_bf16(x):
    hi = x.astype(BF16)
    return hi, (x - hi.astype(F32)).astype(BF16)


def _sorted_positions(lane, i1, i2):
    t = lane.shape[0]
    oh1 = lane == i1
    oh2 = lane == i2
    oh = jnp.concatenate([oh1, oh2], axis=1).astype(BF16)
    r = lax.broadcasted_iota(jnp.int32, (t, t), 0)
    c = lax.broadcasted_iota(jnp.int32, (t, t), 1)
    before = (c < r).astype(BF16)
    rank = jnp.dot(before, oh, preferred_element_type=F32)
    cnt1 = jnp.sum(oh1.astype(F32), axis=0, keepdims=True)
    cnt2 = jnp.sum(oh2.astype(F32), axis=0, keepdims=True)
    run = jnp.ceil((cnt1 + cnt2) * (1.0 / SLOT_ALIGN)) * SLOT_ALIGN
    lr = lax.broadcasted_iota(jnp.int32, (ROUTER_LANES, ROUTER_LANES), 0)
    lc = lax.broadcasted_iota(jnp.int32, (ROUTER_LANES, ROUTER_LANES), 1)
    start = jnp.dot(jnp.broadcast_to(run, (8, ROUTER_LANES)).astype(BF16), (lr < lc).astype(BF16),
                    preferred_element_type=F32)[0:1]
    pos1 = jnp.sum(jnp.where(oh1, start + rank[:, :ROUTER_LANES], 0.0), axis=1, keepdims=True)
    pos2 = jnp.sum(jnp.where(oh2, start + cnt1 + rank[:, ROUTER_LANES:], 0.0), axis=1, keepdims=True)
    return pos1, pos2, run


def _merge_kernel(u_ref, up_ref, un_ref, a_ref, x_ref, pw_ref, ps_ref, wo_ref, g1_ref, n2_ref, sh_ref,
                  sc_ref, wr_ref, rb_ref, xo_ref, xa_ref, tm_ref, lm_ref, run_ref, ext_scr, *, t, seq_len,
                  d_pool):
    i = pl.program_id(1)
    n_tiles = pl.num_programs(1)
    u = u_ref[0]
    ext_scr[0:HALO] = jnp.where(i > 0, up_ref[0], 0.0)
    ext_scr[HALO:HALO + t] = u
    ext_scr[HALO + t:] = jnp.where(i < n_tiles - 1, un_ref[0], 0.0)

    pos = i * t + lax.broadcasted_iota(jnp.int32, (t, 1), 0)
    gc = d_pool // len(POOL_WINDOWS)
    mix = jnp.zeros((t, wo_ref.shape[1]), F32)
    for gi, w in enumerate(POOL_WINDOWS):
        left = w // 2
        right = w - 1 - left
        cols = slice(gi * gc, (gi + 1) * gc)
        s = ext_scr[pl.ds(HALO - left, t), cols]
        for o in range(-left + 1, right + 1):
            s = s + ext_scr[pl.ds(HALO + o, t), cols]
        cnt = (jnp.minimum(pos + right + 1, seq_len) - jnp.maximum(pos - left, 0)).astype(F32)
        pooled = s / cnt - u[:, cols]
        mixed = jnp.dot(pooled.astype(BF16), pw_ref[gi], preferred_element_type=F32) * ps_ref[:, cols]
        mix = mix + jnp.dot(mixed.astype(BF16), wo_ref[cols, :], preferred_element_type=F32)
    mix = mix + jnp.dot(a_ref[0], wo_ref[d_pool:, :], preferred_element_type=F32)

    x_new = x_ref[0] + g1_ref[0] * mix
    xo_ref[0] = x_new
    h2 = _rmsnorm(x_new, n2_ref[...]) * (1.0 + sc_ref[0]) + sh_ref[0]
    h_hi, h_lo = _split_bf16(h2)
    d = h2.shape[1]
    logits = (jnp.dot(h_hi, wr_ref[0], preferred_element_type=F32)
              + jnp.dot(h_lo, wr_ref[0], preferred_element_type=F32)
              + jnp.dot(h_hi, wr_ref[1], preferred_element_type=F32)) + rb_ref[...]
    lane, i1, i2, w1, w2 = _route(logits)
    pos1, pos2, run = _sorted_positions(lane, i1, i2)
    w1_hi = w1.astype(BF16).astype(F32)
    w2_hi = w2.astype(BF16).astype(F32)
    aux = jnp.zeros(lane.shape, F32)
    for k, val in enumerate((w1_hi, w1 - w1_hi, w2_hi, w2 - w2_hi, (i1 - N_GROUPS).astype(F32))):
        aux = jnp.where(lane == k, val, aux)
    xa_ref[0, :, :d] = h_hi
    xa_ref[0, :, d:] = aux.astype(BF16)
    info = jnp.where(lane == 0, pos1, jnp.where(lane == 1, pos2, 0.0))
    tm_ref[0] = info
    lm_ref[0] = info.T[:8]
    run_ref[0, 0] = run


def _merge(u, attn, x, pool_w_bf, pool_scale, w_out_bf, g1, norm2_g, sh2, sc2, wr, rb):
    b, l, d = x.shape
    d_pool = u.shape[2]
    t = min(MOE_TOKEN_TILE, l)
    hb = t // HALO
    n_halo = l // HALO
    row = lambda bi, i: (bi, i, 0)
    per_batch = lambda bi, i: (bi, 0, 0)
    const2 = lambda bi, i: (0, 0)
    const3 = lambda bi, i: (0, 0, 0)
    return pl.pallas_call(
        functools.partial(_merge_kernel, t=t, seq_len=l, d_pool=d_pool),
        grid=(b, l // t),
        in_specs=[
            pl.BlockSpec((1, t, d_pool), row),
            pl.BlockSpec((1, HALO, d_pool), lambda bi, i: (bi, jnp.maximum(i * hb - 1, 0), 0)),
            pl.BlockSpec((1, HALO, d_pool), lambda bi, i: (bi, jnp.minimum((i + 1) * hb, n_halo - 1), 0)),
            pl.BlockSpec((1, t, attn.shape[2]), row),
            pl.BlockSpec((1, t, d), row),
            pl.BlockSpec(pool_w_bf.shape, const3),
            pl.BlockSpec((1, d_pool), const2),
            pl.BlockSpec(w_out_bf.shape, const2),
            pl.BlockSpec((1, 1, d), per_batch),
            pl.BlockSpec((1, d), const2),
            pl.BlockSpec((1, 1, d), per_batch),
            pl.BlockSpec((1, 1, d), per_batch),
            pl.BlockSpec(wr.shape, const3),
            pl.BlockSpec((1, ROUTER_LANES), const2),
        ],
        out_specs=[
            pl.BlockSpec((1, t, d), row),
            pl.BlockSpec((1, t, d + ROUTER_LANES), row),
            pl.BlockSpec((1, t, ROUTER_LANES), row),
            pl.BlockSpec((1, 8, t), lambda bi, i: (bi, 0, i)),
            pl.BlockSpec((1, 1, 1, ROUTER_LANES), lambda bi, i: (bi, i, 0, 0)),
        ],
        out_shape=[
            jax.ShapeDtypeStruct((b, l, d), F32),
            jax.ShapeDtypeStruct((b, l, d + ROUTER_LANES), BF16),
            jax.ShapeDtypeStruct((b, l, ROUTER_LANES), F32),
            jax.ShapeDtypeStruct((b, 8, l), F32),
            jax.ShapeDtypeStruct((b, l // t, 1, ROUTER_LANES), F32),
        ],
        scratch_shapes=[pltpu.VMEM((t + 2 * HALO, d_pool), F32)],
        compiler_params=_params(("arbitrary", "arbitrary")),
        name="merge",
    )(u, u, u, attn, x, pool_w_bf, pool_scale.reshape(1, d_pool), w_out_bf, g1, norm2_g.reshape(1, d),
      sh2, sc2, wr, rb)


def _moe_plan(runs, t, tm):
    counts = runs[:, N_GROUPS:N_GROUPS + N_EXPERTS].astype(jnp.int32)
    n_tiles = counts.shape[0]
    total = jnp.sum(counts, axis=0)
    region = (total + tm - 1) // tm * tm
    region_end = jnp.cumsum(region)
    start = region_end - region
    dst = start[None, :] + jnp.cumsum(counts, axis=0) - counts
    off = jnp.cumsum(counts, axis=1) - counts
    s_max = 2 * n_tiles * t + n_tiles * N_EXPERTS * SLOT_ALIGN + N_EXPERTS * tm
    n_row_tiles = -(-s_max // tm)
    first_row = jnp.arange(n_row_tiles, dtype=jnp.int32) * tm
    valid = first_row < region_end[-1]
    last_valid = jnp.maximum(region_end[-1] // tm - 1, 0)
    block = jnp.where(valid, jnp.arange(n_row_tiles, dtype=jnp.int32), last_valid)
    expert = jnp.minimum(jnp.searchsorted(region_end, block * tm, side="right"), N_EXPERTS - 1).astype(jnp.int32)
    plan = dict(dst=dst.reshape(-1), off=off.reshape(-1), n=(counts // SLOT_ALIGN).reshape(-1),
                slack_dst=start + total, slack_n=(region - total) // SLOT_ALIGN,
                expert=expert, block=block, valid=valid.astype(jnp.int32))
    return plan, n_row_tiles


def _copy_runs(n_ref, tile, src_of, dst_of, sem):
    total = 0
    for e in range(N_EXPERTS):
        n = n_ref[tile * N_EXPERTS + e]

        def issue(j, carry, e=e):
            pltpu.make_async_copy(src_of(e, j), dst_of(e, j), sem).start()
            return carry

        lax.fori_loop(0, n, issue, 0)
        total = total + n

    def drain(j, carry):
        pltpu.make_async_copy(src_of(0, 0), dst_of(0, 0), sem).wait()
        return carry

    lax.fori_loop(0, total, drain, 0)


def _rows(ref, first):
    return ref.at[pl.ds(pl.multiple_of(first, SLOT_ALIGN), SLOT_ALIGN)]


def _dispatch_kernel(dst_ref, off_ref, n_ref, sdst_ref, sn_ref, xa_ref, lm_ref, xs_hbm, buf, sem, *, p):
    i = pl.program_id(0)
    t = xa_ref.shape[0]
    pos1 = lm_ref[0, 0:1, :].astype(jnp.int32)
    pos2 = lm_ref[0, 1:2, :].astype(jnp.int32)
    row = lax.broadcasted_iota(jnp.int32, (p, t), 0)
    perm = ((row == pos1) | (row == pos2)).astype(BF16)
    buf[...] = jnp.dot(perm, xa_ref[...], preferred_element_type=F32).astype(BF16)
    _copy_runs(n_ref, i,
               lambda e, j: _rows(buf, off_ref[i * N_EXPERTS + e] + j * SLOT_ALIGN),
               lambda e, j: _rows(xs_hbm, dst_ref[i * N_EXPERTS + e] + j * SLOT_ALIGN), sem)

    @pl.when(i == pl.num_programs(0) - 1)
    def _():
        buf[0:SLOT_ALIGN] = jnp.zeros((SLOT_ALIGN, buf.shape[1]), BF16)
        _copy_runs(sn_ref, 0,
                   lambda e, j: buf.at[pl.ds(0, SLOT_ALIGN)],
                   lambda e, j: _rows(xs_hbm, sdst_ref[e] + j * SLOT_ALIGN), sem)


def _expert_kernel(e_ref, blk_ref, valid_ref, xs_ref, wg_ref, wu_ref, wd_ref, ys_ref, *, d):
    r = pl.program_id(0)

    @pl.when(valid_ref[r] == 1)
    def _():
        x = xs_ref[:, :d]
        aux = xs_ref[:, d:].astype(F32)
        is_first = aux[:, 4:5] == e_ref[r].astype(F32)
        gate = jnp.where(is_first, aux[:, 0:1] + aux[:, 1:2], aux[:, 2:3] + aux[:, 3:4])
        a = jnp.dot(x, wg_ref[0, 0], preferred_element_type=F32)
        bu = jnp.dot(x, wu_ref[0, 0], preferred_element_type=F32)
        hid = (a * jax.nn.sigmoid(a)) * bu * gate
        ys_ref[...] = jnp.dot(hid.astype(BF16), wd_ref[0, 0], preferred_element_type=F32).astype(BF16)


def _combine_kernel(*refs, final_norm):
    if final_norm:
        dst_ref, off_ref, n_ref, ys_hbm, tm_ref, x_ref, g2_ref, fg_ref, o_ref, buf, sem = refs
    else:
        dst_ref, off_ref, n_ref, ys_hbm, tm_ref, x_ref, g2_ref, o_ref, buf, sem = refs
    i = pl.program_id(0)
    t = x_ref.shape[0]
    p = buf.shape[0]
    buf[...] = jnp.zeros(buf.shape, BF16)
    _copy_runs(n_ref, i,
               lambda e, j: _rows(ys_hbm, dst_ref[i * N_EXPERTS + e] + j * SLOT_ALIGN),
               lambda e, j: _rows(buf, off_ref[i * N_EXPERTS + e] + j * SLOT_ALIGN), sem)
    pos1 = tm_ref[:, 0:1].astype(jnp.int32)
    pos2 = tm_ref[:, 1:2].astype(jnp.int32)
    col = lax.broadcasted_iota(jnp.int32, (t, p), 1)
    perm_t = ((col == pos1) | (col == pos2)).astype(BF16)
    y = x_ref[...] + g2_ref[0] * jnp.dot(perm_t, buf[...], preferred_element_type=F32)
    if final_norm:
        y = _rmsnorm(y, fg_ref[...])
    o_ref[...] = y


def _moe(x, xa, info_tm, info_lm, runs, g2, wg_bf, wu_bf, wd_bf, layer, final_g):
    b, l, d = x.shape
    da = xa.shape[2]
    f = wg_bf.shape[3]
    t = min(MOE_TOKEN_TILE, l)
    tiles_per_batch = l // t
    n_tiles = b * tiles_per_batch
    p = 2 * t + N_EXPERTS * SLOT_ALIGN
    tm = MOE_ROW_TILE
    plan, n_row_tiles = _moe_plan(runs.reshape(n_tiles, ROUTER_LANES), t, tm)
    s_rows = n_row_tiles * tm
    n = b * l

    xs = pl.pallas_call(
        functools.partial(_dispatch_kernel, p=p),
        grid_spec=pltpu.PrefetchScalarGridSpec(
            num_scalar_prefetch=5,
            grid=(n_tiles,),
            in_specs=[
                pl.BlockSpec((t, da), lambda i, *_: (i, 0)),
                pl.BlockSpec((1, 8, t), lambda i, *_: (i // tiles_per_batch, 0, i % tiles_per_batch)),
            ],
            out_specs=pl.BlockSpec(memory_space=pl.ANY),
            scratch_shapes=[pltpu.VMEM((p, da), BF16), pltpu.SemaphoreType.DMA],
        ),
        out_shape=jax.ShapeDtypeStruct((s_rows, da), BF16),
        compiler_params=_params(("arbitrary",)),
        name="moe_dispatch",
    )(plan["dst"], plan["off"], plan["n"], plan["slack_dst"], plan["slack_n"], xa.reshape(n, da), info_lm)

    w_map = lambda r, e_ref, blk_ref, valid_ref: (layer, e_ref[r], 0, 0)
    row_map = lambda r, e_ref, blk_ref, valid_ref: (blk_ref[r], 0)
    ys = pl.pallas_call(
        functools.partial(_expert_kernel, d=d),
        grid_spec=pltpu.PrefetchScalarGridSpec(
            num_scalar_prefetch=3,
            grid=(n_row_tiles,),
            in_specs=[
                pl.BlockSpec((tm, da), row_map),
                pl.BlockSpec((1, 1, d, f), w_map),
                pl.BlockSpec((1, 1, d, f), w_map),
                pl.BlockSpec((1, 1, f, d), w_map),
            ],
            out_specs=pl.BlockSpec((tm, d), row_map),
        ),
        out_shape=jax.ShapeDtypeStruct((s_rows, d), BF16),
        compiler_params=_params(("arbitrary",)),
        name="moe_experts",
    )(plan["expert"], plan["block"], plan["valid"], xs, wg_bf, wu_bf, wd_bf)

    final_norm = final_g is not None
    in_specs = [
        pl.BlockSpec(memory_space=pl.ANY),
        pl.BlockSpec((t, ROUTER_LANES), lambda i, *_: (i, 0)),
        pl.BlockSpec((t, d), lambda i, *_: (i, 0)),
        pl.BlockSpec((1, 1, d), lambda i, *_: (i // tiles_per_batch, 0, 0)),
    ]
    args = [ys, info_tm.reshape(n, ROUTER_LANES), x.reshape(n, d), g2]
    if final_norm:
        in_specs.append(pl.BlockSpec((1, d), lambda i, *_: (0, 0)))
        args.append(final_g.reshape(1, d))
    out = pl.pallas_call(
        functools.partial(_combine_kernel, final_norm=final_norm),
        grid_spec=pltpu.PrefetchScalarGridSpec(
            num_scalar_prefetch=3,
            grid=(n_tiles,),
            in_specs=in_specs,
            out_specs=pl.BlockSpec((t, d), lambda i, *_: (i, 0)),
            scratch_shapes=[pltpu.VMEM((p, d), BF16), pltpu.SemaphoreType.DMA],
        ),
        out_shape=jax.ShapeDtypeStruct((n, d), F32),
        compiler_params=_params(("arbitrary",)),
        name="moe_combine",
    )(plan["dst"], plan["off"], plan["n"], *args)
    return out.reshape(b, l, d)


def kernel(x, c, ctx, c_ctx, ada_w, ada_b, norm1_g, norm2_g, w_in, pool_w, pool_scale, lambda_q1, lambda_k1,
           lambda_q2, lambda_k2, subln_g, w_out, router_coarse_w, router_coarse_b, router_fine_w,
           router_fine_b, w_gate, w_up, w_down, final_g):
    b, l, d = x.shape
    depth = ada_w.shape[0]
    assert b + 1 <= MOD_ROWS and l % GRID_W == 0

    tables = _rope_tables(l)
    cc = jnp.zeros((MOD_ROWS, d), F32).at[:b].set(c).at[b].set(c_ctx)
    d_pool = d // 2
    d_attn = (w_in.shape[-1] - d_pool) // 3
    w_uk = jnp.concatenate([w_in[..., :d_pool], w_in[..., d_pool + d_attn:d_pool + 2 * d_attn]],
                           axis=-1).astype(BF16)
    w_qvt = jnp.swapaxes(jnp.concatenate([w_in[..., d_pool:d_pool + d_attn], w_in[..., d_pool + 2 * d_attn:]],
                                         axis=-1), 1, 2).astype(BF16)
    pool_w_bf = pool_w.astype(BF16)
    w_out_bf = w_out.astype(BF16)
    n_e = w_gate.shape[1] * w_gate.shape[2]
    f = w_gate.shape[-1]
    wg_bf = w_gate.astype(BF16).reshape(depth, n_e, d, f)
    wu_bf = w_up.astype(BF16).reshape(depth, n_e, d, f)
    wd_bf = w_down.astype(BF16).reshape(depth, n_e, f, d)
    pad = ROUTER_LANES - N_GROUPS - N_EXPERTS
    wr = jnp.concatenate([router_coarse_w, router_fine_w, jnp.zeros((depth, d, pad), F32)], axis=-1)
    wr_hi = wr.astype(BF16)
    wr = jnp.stack([wr_hi, (wr - wr_hi.astype(F32)).astype(BF16)], axis=1)
    rb = jnp.concatenate([router_coarse_b, router_fine_b, jnp.zeros((depth, pad), F32)], axis=-1)

    xc = ctx
    for layer in range(depth):
        last = layer == depth - 1
        lam_init = 0.8 - 0.6 * math.exp(-0.3 * layer)
        mod = _ada(cc, ada_w, ada_b, layer)
        sh1, sc1, g1, sh2, sc2, g2 = jnp.split(mod[:b, None, :], 6, axis=-1)
        csh1, csc1, cg1, csh2, csc2, cg2 = jnp.split(
            jnp.broadcast_to(mod[b][None, None, :], (b, 1, mod.shape[1])), 6, axis=-1)
        lam4 = jnp.stack([lambda_q1[layer], lambda_k1[layer], lambda_q2[layer], lambda_k2[layer]])

        u, q, k, v = _inproj(x, sh1, sc1, norm1_g[layer], w_uk[layer], w_qvt[layer], tables)
        uc, qc, kc, vc = _inproj(xc, csh1, csc1, norm1_g[layer], w_uk[layer], w_qvt[layer], None)
        attn = _attention(q, k, v, kc, vc, lam4, subln_g[layer], lam_init)
        merge_w = (pool_w_bf[layer], pool_scale[layer], w_out_bf[layer])
        route_w = (wr[layer], rb[layer].reshape(1, ROUTER_LANES))
        x, *routed = _merge(u, attn, x, *merge_w, g1, norm2_g[layer], sh2, sc2, *route_w)
        if not last:
            attn_c = _attention(qc, kc, vc, None, None, lam4, subln_g[layer], lam_init)
            xc, *routed_c = _merge(uc, attn_c, xc, *merge_w, cg1, norm2_g[layer], csh2, csc2, *route_w)
            xc = _moe(xc, *routed_c, cg2, wg_bf, wu_bf, wd_bf, layer, None)
        x = _moe(x, *routed, g2, wg_bf, wu_bf, wd_bf, layer, final_g if last else None)
    return x
```

```python
import functools
import math

import jax
import jax.numpy as jnp
from jax import lax
from jax.experimental import pallas as pl
from jax.experimental.pallas import tpu as pltpu

F32 = jnp.float32
BF16 = jnp.bfloat16

GRID_W = 64
POOL_WINDOWS = (2, 4, 8, 16)
HEAD_DIM = 64
HEAD_PAIR = 2 * HEAD_DIM
ROPE_THETA = 10000.0
N_GROUPS = 4
EXPERTS_PER_GROUP = 8
N_EXPERTS = N_GROUPS * EXPERTS_PER_GROUP
EPS = 1e-6
HALO = 16
ROUTER_LANES = 128
MOD_ROWS = 8
ONES_ROWS = 16
SLOT_ALIGN = 16
MOE_TOKEN_TILE = 512
MOE_ROW_TILE = 512

VMEM_LIMIT = 48 * 1024 * 1024


def _params(sem):
    return pltpu.CompilerParams(dimension_semantics=sem, vmem_limit_bytes=VMEM_LIMIT)


def _ada_kernel(cc_ref, w_ref, b_ref, o_ref):
    s = cc_ref[...]
    s = s * jax.nn.sigmoid(s)
    o_ref[...] = jnp.dot(s, w_ref[0], precision=lax.Precision.HIGHEST,
                         preferred_element_type=F32) + b_ref[0]


def _ada(cc, ada_w, ada_b, layer):
    _, d, n = ada_w.shape
    bn = 1024
    return pl.pallas_call(
        _ada_kernel,
        grid=(n // bn,),
        in_specs=[
            pl.BlockSpec((MOD_ROWS, d), lambda j: (0, 0)),
            pl.BlockSpec((1, d, bn), lambda j: (layer, 0, j)),
            pl.BlockSpec((1, 1, bn), lambda j: (layer, 0, j)),
        ],
        out_specs=pl.BlockSpec((MOD_ROWS, bn), lambda j: (0, j)),
        out_shape=jax.ShapeDtypeStruct((MOD_ROWS, n), F32),
        compiler_params=_params(("arbitrary",)),
        name="ada",
    )(cc, ada_w, ada_b.reshape(ada_b.shape[0], 1, n))


def _rope_tables(n):
    pos = jnp.arange(n, dtype=jnp.int32)
    row = (pos // GRID_W).astype(F32)
    col = (pos % GRID_W).astype(F32)
    n_freq = HEAD_DIM // 4
    inv = ROPE_THETA ** (-jnp.arange(n_freq, dtype=F32) / n_freq)
    ang_r = row[:, None] * inv
    ang_c = col[:, None] * inv
    cos = jnp.concatenate([jnp.cos(ang_r), jnp.cos(ang_r), jnp.cos(ang_c), jnp.cos(ang_c)], axis=-1)
    sin = jnp.concatenate([-jnp.sin(ang_r), jnp.sin(ang_r), -jnp.sin(ang_c), jnp.sin(ang_c)], axis=-1)
    cos, sin = jnp.tile(cos, (1, 2)), jnp.tile(sin, (1, 2))
    return cos, sin, cos.T, sin.T


def _rmsnorm(x, g):
    ms = jnp.mean(x * x, axis=-1, keepdims=True)
    return x * lax.rsqrt(ms + EPS) * g


def _inproj_kernel(*refs, rope, d_pool, d_attn, q_scale):
    if rope:
        (x_ref, sh_ref, sc_ref, g_ref, wuk_ref, wqv_ref, cos_ref, sin_ref, cost_ref, sint_ref,
         u_ref, qt_ref, k_ref, vt_ref) = refs
    else:
        x_ref, sh_ref, sc_ref, g_ref, wuk_ref, wqv_ref, u_ref, qt_ref, k_ref, vt_ref = refs
    h = (_rmsnorm(x_ref[0], g_ref[...]) * (1.0 + sc_ref[0]) + sh_ref[0]).astype(BF16)
    p = jnp.dot(h, wuk_ref[...], preferred_element_type=F32)
    pt = lax.dot_general(wqv_ref[...], h, (((1,), (1,)), ((), ())), preferred_element_type=F32)
    u_ref[0] = p[:, :d_pool]
    vt_ref[0] = pt[d_attn:].astype(BF16)
    half = HEAD_DIM // 4
    if rope:
        cos, sin = cos_ref[...], sin_ref[...]
        cost, sint = cost_ref[...], sint_ref[...]
        lane = lax.broadcasted_iota(jnp.int32, cos.shape, 1)
        first_half = (lane % (2 * half)) < half
    for h0 in range(0, d_attn, HEAD_PAIR):
        c = p[:, d_pool + h0: d_pool + h0 + HEAD_PAIR]
        ct = pt[h0:h0 + HEAD_PAIR]
        if rope:
            partner = jnp.where(first_half, pltpu.roll(c, HEAD_PAIR - half, 1), pltpu.roll(c, half, 1))
            c = c * cos + partner * sin
            swapped = []
            for r0 in range(0, HEAD_PAIR, 2 * half):
                swapped += [ct[r0 + half:r0 + 2 * half], ct[r0:r0 + half]]
            ct = ct * cost + jnp.concatenate(swapped, axis=0) * sint
        k_ref[0, :, h0:h0 + HEAD_PAIR] = c.astype(BF16)
        qt_ref[0, h0:h0 + HEAD_PAIR, :] = (ct * q_scale).astype(BF16)


def _inproj(x, shift, scale, g, w_uk, w_qvt, tables):
    b, l, d = x.shape
    d_pool = d // 2
    d_attn = w_qvt.shape[0] // 2
    t = min(512, l)
    rope = tables is not None
    row = lambda bi, i: (bi, i, 0)
    col = lambda bi, i: (bi, 0, i)
    per_batch = lambda bi, i: (bi, 0, 0)
    const = lambda bi, i: (0, 0)
    in_specs = [
        pl.BlockSpec((1, t, d), row),
        pl.BlockSpec((1, 1, d), per_batch),
        pl.BlockSpec((1, 1, d), per_batch),
        pl.BlockSpec((1, d), const),
        pl.BlockSpec(w_uk.shape, const),
        pl.BlockSpec(w_qvt.shape, const),
    ]
    args = [x, shift, scale, g.reshape(1, d), w_uk, w_qvt]
    if rope:
        in_specs += [pl.BlockSpec((t, HEAD_PAIR), lambda bi, i: (i, 0))] * 2
        in_specs += [pl.BlockSpec((HEAD_PAIR, t), lambda bi, i: (0, i))] * 2
        args += list(tables)
    return pl.pallas_call(
        functools.partial(_inproj_kernel, rope=rope, d_pool=d_pool, d_attn=d_attn,
                          q_scale=HEAD_DIM ** -0.5 * math.log2(math.e)),
        grid=(b, l // t),
        in_specs=in_specs,
        out_specs=[pl.BlockSpec((1, t, d_pool), row), pl.BlockSpec((1, d_attn, t), col),
                   pl.BlockSpec((1, t, d_attn), row), pl.BlockSpec((1, d_attn, t), col)],
        out_shape=[jax.ShapeDtypeStruct((b, l, d_pool), F32), jax.ShapeDtypeStruct((b, d_attn, l), BF16),
                   jax.ShapeDtypeStruct((b, l, d_attn), BF16), jax.ShapeDtypeStruct((b, d_attn, l), BF16)],
        compiler_params=_params(("arbitrary", "arbitrary")),
        name="inproj",
    )(*args)


def _attn_kernel(*refs, tq, tk, cb, n_chunks, has_ctx, lam_init):
    if has_ctx:
        (lam_ref, qt_ref, k_ref, vt_ref, kc_ref, vct_ref, g_ref, o_ref,
         qs_scr, m_scr, l_scr, acc_scr, s_scr) = refs
    else:
        lam_ref, qt_ref, k_ref, vt_ref, g_ref, o_ref, qs_scr, m_scr, l_scr, acc_scr, s_scr = refs
    qt = qt_ref[0]
    sub = lax.broadcasted_iota(jnp.int32, qt.shape, 0)
    zero = jnp.zeros_like(qt)
    qs_scr[:, :tq] = jnp.where(sub < HEAD_DIM, qt, zero)
    qs_scr[:, tq:] = jnp.where(sub >= HEAD_DIM, qt, zero)
    m_scr[...] = jnp.full(m_scr.shape, -jnp.inf, F32)
    l_scr[...] = jnp.zeros(l_scr.shape, F32)
    acc_scr[...] = jnp.zeros(acc_scr.shape, F32)

    def scores(kj, c0):
        return jnp.dot(kj, qs_scr[:, c0:c0 + cb], preferred_element_type=F32)

    def chunk(kj, vtj, k_after):
        vta = jnp.concatenate([vtj, jnp.ones((ONES_ROWS, vtj.shape[1]), BF16)], axis=0)
        n_keys = kj.shape[0]
        s_next = s_scr[0:n_keys, :]
        for c0 in range(0, 2 * tq, cb):
            cols = slice(c0, c0 + cb)
            s = s_next
            if c0 + cb < 2 * tq:
                s_next = scores(kj, c0 + cb)
            elif k_after is not None:
                s_scr[0:k_after.shape[0], :] = scores(k_after, 0)
            m_old = m_scr[:, cols]
            m_new = jnp.maximum(m_old, jnp.max(s, axis=0, keepdims=True))
            alpha = jnp.exp2(m_old - m_new)
            p = jnp.exp2(s - m_new).astype(BF16)
            r = jnp.dot(vta, p, preferred_element_type=F32)
            l_scr[:, cols] = alpha * l_scr[:, cols] + r[HEAD_PAIR:HEAD_PAIR + 1]
            acc_scr[:, cols] = alpha * acc_scr[:, cols] + r[:HEAD_PAIR]
            m_scr[:, cols] = m_new

    def span(j):
        return pl.ds(j * tk if isinstance(j, int) else pl.multiple_of(j * tk, tk), tk)

    def keys(j):
        return k_ref[0, span(j), :]

    def values_t(j):
        return vt_ref[0, :, span(j)]

    s_scr[...] = scores(keys(0), 0)
    if n_chunks > 1:
        def body(j, carry):
            chunk(keys(j), values_t(j), keys(j + 1))
            return carry

        lax.fori_loop(0, n_chunks - 1, body, 0, unroll=4)
    last = n_chunks - 1
    chunk(keys(last), values_t(last), kc_ref[0] if has_ctx else None)
    if has_ctx:
        chunk(kc_ref[0], vct_ref[0], None)

    o = acc_scr[...] / l_scr[...]
    lv = lam_ref[...]
    lam = (jnp.exp(jnp.sum(lv[0:1] * lv[1:2], axis=1, keepdims=True))
           - jnp.exp(jnp.sum(lv[2:3] * lv[3:4], axis=1, keepdims=True)) + lam_init)
    a = (o[:, :tq] - lam * o[:, tq:]).T
    o_ref[0] = (_rmsnorm(a, g_ref[...]) * (1.0 - lam_init)).astype(BF16)


def _attention(qt, k, vt, kc, vct, lam4, subln_g, lam_init):
    b, d_attn, lq = qt.shape
    lk = k.shape[1]
    n_heads = d_attn // HEAD_PAIR
    tq = min(1024, lq)
    tk = min(512, lk)
    cb = min(512, 2 * tq)
    has_ctx = kc is not None
    qmap = lambda bi, h, i: (bi, h, i)
    kmap = lambda bi, h, i: (bi, 0, h)
    vmap = lambda bi, h, i: (bi, h, 0)
    const = lambda bi, h, i: (0, 0)
    in_specs = [
        pl.BlockSpec(lam4.shape, const),
        pl.BlockSpec((1, HEAD_PAIR, tq), qmap),
        pl.BlockSpec((1, lk, HEAD_PAIR), kmap),
        pl.BlockSpec((1, HEAD_PAIR, lk), vmap),
    ]
    args = [lam4, qt, k, vt]
    if has_ctx:
        lc = kc.shape[1]
        in_specs += [pl.BlockSpec((1, lc, HEAD_PAIR), kmap), pl.BlockSpec((1, HEAD_PAIR, lc), vmap)]
        args += [kc, vct]
    in_specs.append(pl.BlockSpec((1, HEAD_PAIR), const))
    args.append(subln_g.reshape(1, HEAD_PAIR))
    return pl.pallas_call(
        functools.partial(_attn_kernel, tq=tq, tk=tk, cb=cb, n_chunks=lk // tk, has_ctx=has_ctx,
                          lam_init=lam_init),
        grid=(b, n_heads, lq // tq),
        in_specs=in_specs,
        out_specs=pl.BlockSpec((1, tq, HEAD_PAIR), lambda bi, h, i: (bi, i, h)),
        out_shape=jax.ShapeDtypeStruct((b, lq, d_attn), BF16),
        scratch_shapes=[
            pltpu.VMEM((HEAD_PAIR, 2 * tq), BF16),
            pltpu.VMEM((1, 2 * tq), F32),
            pltpu.VMEM((1, 2 * tq), F32),
            pltpu.VMEM((HEAD_PAIR, 2 * tq), F32),
            pltpu.VMEM((tk, cb), F32),
        ],
        compiler_params=_params(("arbitrary", "arbitrary", "arbitrary")),
        name="diff_attn",
    )(*args)


def _route(logits):
    lane = lax.broadcasted_iota(jnp.int32, logits.shape, 1)
    big = jnp.int32(ROUTER_LANES)
    neg = jnp.float32(-jnp.inf)
    lc = jnp.where(lane < N_GROUPS, logits, neg)
    mc = jnp.max(lc, axis=1, keepdims=True)
    p_group = 1.0 / jnp.sum(jnp.exp(lc - mc), axis=1, keepdims=True)
    g_idx = jnp.min(jnp.where(lc == mc, lane, big), axis=1, keepdims=True)
    lo = N_GROUPS + EXPERTS_PER_GROUP * g_idx
    lf = jnp.where((lane >= lo) & (lane < lo + EXPERTS_PER_GROUP), logits, neg)
    v1 = jnp.max(lf, axis=1, keepdims=True)
    i1 = jnp.min(jnp.where(lf == v1, lane, big), axis=1, keepdims=True)
    lf2 = jnp.where(lane == i1, neg, lf)
    v2 = jnp.max(lf2, axis=1, keepdims=True)
    i2 = jnp.min(jnp.where(lf2 == v2, lane, big), axis=1, keepdims=True)
    e2 = jnp.exp(v2 - v1)
    w1 = p_group / (1.0 + e2)
    w2 = p_group * e2 / (1.0 + e2)
    return lane, i1, i2, w1, w2


def _split_bf16(x):
    hi = x.astype(BF16)
    return hi, (x - hi.astype(F32)).astype(BF16)


def _sorted_positions(lane, i1, i2):
    t = lane.shape[0]
    oh1 = lane == i1
    oh2 = lane == i2
    oh = jnp.concatenate([oh1, oh2], axis=1).astype(BF16)
    r = lax.broadcasted_iota(jnp.int32, (t, t), 0)
    c = lax.broadcasted_iota(jnp.int32, (t, t), 1)
    before = (c < r).astype(BF16)
    rank = jnp.dot(before, oh, preferred_element_type=F32)
    cnt1 = jnp.sum(oh1.astype(F32), axis=0, keepdims=True)
    cnt2 = jnp.sum(oh2.astype(F32), axis=0, keepdims=True)
    run = jnp.ceil((cnt1 + cnt2) * (1.0 / SLOT_ALIGN)) * SLOT_ALIGN
    lr = lax.broadcasted_iota(jnp.int32, (ROUTER_LANES, ROUTER_LANES), 0)
    lc = lax.broadcasted_iota(jnp.int32, (ROUTER_LANES, ROUTER_LANES), 1)
    start = jnp.dot(jnp.broadcast_to(run, (8, ROUTER_LANES)).astype(BF16), (lr < lc).astype(BF16),
                    preferred_element_type=F32)[0:1]
    pos1 = jnp.sum(jnp.where(oh1, start + rank[:, :ROUTER_LANES], 0.0), axis=1, keepdims=True)
    pos2 = jnp.sum(jnp.where(oh2, start + cnt1 + rank[:, ROUTER_LANES:], 0.0), axis=1, keepdims=True)
    return pos1, pos2, run


def _merge_kernel(u_ref, up_ref, un_ref, a_ref, x_ref, pw_ref, ps_ref, wo_ref, g1_ref, n2_ref, sh_ref,
                  sc_ref, wr_ref, rb_ref, xo_ref, xa_ref, tm_ref, lm_ref, run_ref, ext_scr, *, t, seq_len,
                  d_pool):
    i = pl.program_id(1)
    n_tiles = pl.num_programs(1)
    u = u_ref[0]
    ext_scr[0:HALO] = jnp.where(i > 0, up_ref[0], 0.0)
    ext_scr[HALO:HALO + t] = u
    ext_scr[HALO + t:] = jnp.where(i < n_tiles - 1, un_ref[0], 0.0)

    pos = i * t + lax.broadcasted_iota(jnp.int32, (t, 1), 0)
    gc = d_pool // len(POOL_WINDOWS)
    mix = jnp.zeros((t, wo_ref.shape[1]), F32)
    for gi, w in enumerate(POOL_WINDOWS):
        left = w // 2
        right = w - 1 - left
        cols = slice(gi * gc, (gi + 1) * gc)
        s = ext_scr[pl.ds(HALO - left, t), cols]
        for o in range(-left + 1, right + 1):
            s = s + ext_scr[pl.ds(HALO + o, t), cols]
        cnt = (jnp.minimum(pos + right + 1, seq_len) - jnp.maximum(pos - left, 0)).astype(F32)
        pooled = s / cnt - u[:, cols]
        mixed = jnp.dot(pooled.astype(BF16), pw_ref[gi], preferred_element_type=F32) * ps_ref[:, cols]
        mix = mix + jnp.dot(mixed.astype(BF16), wo_ref[cols, :], preferred_element_type=F32)
    mix = mix + jnp.dot(a_ref[0], wo_ref[d_pool:, :], preferred_element_type=F32)

    x_new = x_ref[0] + g1_ref[0] * mix
    xo_ref[0] = x_new
    h2 = _rmsnorm(x_new, n2_ref[...]) * (1.0 + sc_ref[0]) + sh_ref[0]
    h_hi, h_lo = _split_bf16(h2)
    d = h2.shape[1]
    logits = (jnp.dot(h_hi, wr_ref[0], preferred_element_type=F32)
              + jnp.dot(h_lo, wr_ref[0], preferred_element_type=F32)
              + jnp.dot(h_hi, wr_ref[1], preferred_element_type=F32)) + rb_ref[...]
    lane, i1, i2, w1, w2 = _route(logits)
    pos1, pos2, run = _sorted_positions(lane, i1, i2)
    w1_hi = w1.astype(BF16).astype(F32)
    w2_hi = w2.astype(BF16).astype(F32)
    aux = jnp.zeros(lane.shape, F32)
    for k, val in enumerate((w1_hi, w1 - w1_hi, w2_hi, w2 - w2_hi, (i1 - N_GROUPS).astype(F32))):
        aux = jnp.where(lane == k, val, aux)
    xa_ref[0, :, :d] = h_hi
    xa_ref[0, :, d:] = aux.astype(BF16)
    info = jnp.where(lane == 0, pos1, jnp.where(lane == 1, pos2, 0.0))
    tm_ref[0] = info
    lm_ref[0] = info.T[:8]
    run_ref[0, 0] = run


def _merge(u, attn, x, pool_w_bf, pool_scale, w_out_bf, g1, norm2_g, sh2, sc2, wr, rb):
    b, l, d = x.shape
    d_pool = u.shape[2]
    t = min(MOE_TOKEN_TILE, l)
    hb = t // HALO
    n_halo = l // HALO
    row = lambda bi, i: (bi, i, 0)
    per_batch = lambda bi, i: (bi, 0, 0)
    const2 = lambda bi, i: (0, 0)
    const3 = lambda bi, i: (0, 0, 0)
    return pl.pallas_call(
        functools.partial(_merge_kernel, t=t, seq_len=l, d_pool=d_pool),
        grid=(b, l // t),
        in_specs=[
            pl.BlockSpec((1, t, d_pool), row),
            pl.BlockSpec((1, HALO, d_pool), lambda bi, i: (bi, jnp.maximum(i * hb - 1, 0), 0)),
            pl.BlockSpec((1, HALO, d_pool), lambda bi, i: (bi, jnp.minimum((i + 1) * hb, n_halo - 1), 0)),
            pl.BlockSpec((1, t, attn.shape[2]), row),
            pl.BlockSpec((1, t, d), row),
            pl.BlockSpec(pool_w_bf.shape, const3),
            pl.BlockSpec((1, d_pool), const2),
            pl.BlockSpec(w_out_bf.shape, const2),
            pl.BlockSpec((1, 1, d), per_batch),
            pl.BlockSpec((1, d), const2),
            pl.BlockSpec((1, 1, d), per_batch),
            pl.BlockSpec((1, 1, d), per_batch),
            pl.BlockSpec(wr.shape, const3),
            pl.BlockSpec((1, ROUTER_LANES), const2),
        ],
        out_specs=[
            pl.BlockSpec((1, t, d), row),
            pl.BlockSpec((1, t, d + ROUTER_LANES), row),
            pl.BlockSpec((1, t, ROUTER_LANES), row),
            pl.BlockSpec((1, 8, t), lambda bi, i: (bi, 0, i)),
            pl.BlockSpec((1, 1, 1, ROUTER_LANES), lambda bi, i: (bi, i, 0, 0)),
        ],
        out_shape=[
            jax.ShapeDtypeStruct((b, l, d), F32),
            jax.ShapeDtypeStruct((b, l, d + ROUTER_LANES), BF16),
            jax.ShapeDtypeStruct((b, l, ROUTER_LANES), F32),
            jax.ShapeDtypeStruct((b, 8, l), F32),
            jax.ShapeDtypeStruct((b, l // t, 1, ROUTER_LANES), F32),
        ],
        scratch_shapes=[pltpu.VMEM((t + 2 * HALO, d_pool), F32)],
        compiler_params=_params(("arbitrary", "arbitrary")),
        name="merge",
    )(u, u, u, attn, x, pool_w_bf, pool_scale.reshape(1, d_pool), w_out_bf, g1, norm2_g.reshape(1, d),
      sh2, sc2, wr, rb)


def _moe_plan(runs, t, tm):
    counts = runs[:, N_GROUPS:N_GROUPS + N_EXPERTS].astype(jnp.int32)
    n_tiles = counts.shape[0]
    total = jnp.sum(counts, axis=0)
    region = (total + tm - 1) // tm * tm
    region_end = jnp.cumsum(region)
    start = region_end - region
    dst = start[None, :] + jnp.cumsum(counts, axis=0) - counts
    off = jnp.cumsum(counts, axis=1) - counts
    s_max = 2 * n_tiles * t + n_tiles * N_EXPERTS * SLOT_ALIGN + N_EXPERTS * tm
    n_row_tiles = -(-s_max // tm)
    first_row = jnp.arange(n_row_tiles, dtype=jnp.int32) * tm
    valid = first_row < region_end[-1]
    last_valid = jnp.maximum(region_end[-1] // tm - 1, 0)
    block = jnp.where(valid, jnp.arange(n_row_tiles, dtype=jnp.int32), last_valid)
    expert = jnp.sum((region_end[None, :] <= (block * tm)[:, None]).astype(jnp.int32), axis=1)
    expert = jnp.minimum(expert, N_EXPERTS - 1)
    plan = dict(dst=dst.reshape(-1), off=off.reshape(-1), n=(counts // SLOT_ALIGN).reshape(-1),
                slack_dst=start + total, slack_n=(region - total) // SLOT_ALIGN,
                expert=expert, block=block, valid=valid.astype(jnp.int32))
    return plan, n_row_tiles


def _start_runs(n_ref, tile, src_of, dst_of, sem):
    for e in range(N_EXPERTS):
        n = n_ref[tile * N_EXPERTS + e]

        def issue(j, carry, e=e):
            pltpu.make_async_copy(src_of(e, j), dst_of(e, j), sem).start()
            return carry

        lax.fori_loop(0, n, issue, 0)


def _wait_runs(n_ref, tile, src, dst, sem):
    total = 0
    for e in range(N_EXPERTS):
        total = total + n_ref[tile * N_EXPERTS + e]

    def drain(j, carry):
        pltpu.make_async_copy(src, dst, sem).wait()
        return carry

    lax.fori_loop(0, total, drain, 0)


def _span(first):
    return pl.ds(pl.multiple_of(first, SLOT_ALIGN), SLOT_ALIGN)


def _dispatch_kernel(dst_ref, off_ref, n_ref, sdst_ref, sn_ref, xa_ref, lm_ref, xs_hbm, buf, zbuf, sems, *, p):
    i = pl.program_id(0)
    last = pl.num_programs(0) - 1
    slot = i % 2
    t = xa_ref.shape[0]
    one_src, one_dst = buf.at[0, pl.ds(0, SLOT_ALIGN)], xs_hbm.at[pl.ds(0, SLOT_ALIGN)]

    @pl.when(i >= 2)
    def _():
        _wait_runs(n_ref, i - 2, one_src, one_dst, sems.at[slot])

    pos1 = lm_ref[0, 0:1, :].astype(jnp.int32)
    pos2 = lm_ref[0, 1:2, :].astype(jnp.int32)
    row = lax.broadcasted_iota(jnp.int32, (p, t), 0)
    perm = ((row == pos1) | (row == pos2)).astype(BF16)
    buf[slot] = jnp.dot(perm, xa_ref[...], preferred_element_type=F32).astype(BF16)
    _start_runs(n_ref, i,
                lambda e, j: buf.at[slot, _span(off_ref[i * N_EXPERTS + e] + j * SLOT_ALIGN)],
                lambda e, j: xs_hbm.at[_span(dst_ref[i * N_EXPERTS + e] + j * SLOT_ALIGN)], sems.at[slot])

    @pl.when(i == last)
    def _():
        @pl.when(i >= 1)
        def _():
            _wait_runs(n_ref, i - 1, one_src, one_dst, sems.at[1 - slot])

        _wait_runs(n_ref, i, one_src, one_dst, sems.at[slot])
        zbuf[...] = jnp.zeros(zbuf.shape, BF16)
        _start_runs(sn_ref, 0, lambda e, j: zbuf, lambda e, j: xs_hbm.at[_span(sdst_ref[e] + j * SLOT_ALIGN)],
                    sems.at[slot])
        _wait_runs(sn_ref, 0, one_src, one_dst, sems.at[slot])


def _expert_kernel(e_ref, blk_ref, valid_ref, xs_ref, wg_ref, wu_ref, wd_ref, ys_ref, *, d):
    r = pl.program_id(0)

    @pl.when(valid_ref[r] == 1)
    def _():
        x = xs_ref[:, :d]
        aux = xs_ref[:, d:].astype(F32)
        is_first = aux[:, 4:5] == e_ref[r].astype(F32)
        gate = jnp.where(is_first, aux[:, 0:1] + aux[:, 1:2], aux[:, 2:3] + aux[:, 3:4])
        a = jnp.dot(x, wg_ref[0, 0], preferred_element_type=F32)
        bu = jnp.dot(x, wu_ref[0, 0], preferred_element_type=F32)
        hid = (a * jax.nn.sigmoid(a)) * bu * gate
        ys_ref[...] = jnp.dot(hid.astype(BF16), wd_ref[0, 0], preferred_element_type=F32).astype(BF16)


def _combine_kernel(*refs, final_norm):
    if final_norm:
        dst_ref, off_ref, n_ref, ys_hbm, tm_ref, x_ref, g2_ref, fg_ref, o_ref, buf, sems = refs
    else:
        dst_ref, off_ref, n_ref, ys_hbm, tm_ref, x_ref, g2_ref, o_ref, buf, sems = refs
    i = pl.program_id(0)
    slot = i % 2
    t = x_ref.shape[0]
    p = buf.shape[1]

    def fetch(tile, s):
        buf[s] = jnp.zeros(buf.shape[1:], BF16)
        _start_runs(n_ref, tile,
                    lambda e, j: ys_hbm.at[_span(dst_ref[tile * N_EXPERTS + e] + j * SLOT_ALIGN)],
                    lambda e, j: buf.at[s, _span(off_ref[tile * N_EXPERTS + e] + j * SLOT_ALIGN)], sems.at[s])

    @pl.when(i == 0)
    def _():
        fetch(i, slot)

    @pl.when(i + 1 < pl.num_programs(0))
    def _():
        fetch(i + 1, 1 - slot)

    _wait_runs(n_ref, i, ys_hbm.at[pl.ds(0, SLOT_ALIGN)], buf.at[0, pl.ds(0, SLOT_ALIGN)], sems.at[slot])
    pos1 = tm_ref[:, 0:1].astype(jnp.int32)
    pos2 = tm_ref[:, 1:2].astype(jnp.int32)
    col = lax.broadcasted_iota(jnp.int32, (t, p), 1)
    perm_t = ((col == pos1) | (col == pos2)).astype(BF16)
    y = x_ref[...] + g2_ref[0] * jnp.dot(perm_t, buf[slot], preferred_element_type=F32)
    if final_norm:
        y = _rmsnorm(y, fg_ref[...])
    o_ref[...] = y


def _moe(x, xa, info_tm, info_lm, runs, g2, wg_bf, wu_bf, wd_bf, layer, final_g):
    b, l, d = x.shape
    da = xa.shape[2]
    f = wg_bf.shape[3]
    t = min(MOE_TOKEN_TILE, l)
    tiles_per_batch = l // t
    n_tiles = b * tiles_per_batch
    p = 2 * t + N_EXPERTS * SLOT_ALIGN
    tm = MOE_ROW_TILE
    plan, n_row_tiles = _moe_plan(runs.reshape(n_tiles, ROUTER_LANES), t, tm)
    s_rows = n_row_tiles * tm
    n = b * l

    xs = pl.pallas_call(
        functools.partial(_dispatch_kernel, p=p),
        grid_spec=pltpu.PrefetchScalarGridSpec(
            num_scalar_prefetch=5,
            grid=(n_tiles,),
            in_specs=[
                pl.BlockSpec((t, da), lambda i, *_: (i, 0)),
                pl.BlockSpec((1, 8, t), lambda i, *_: (i // tiles_per_batch, 0, i % tiles_per_batch)),
            ],
            out_specs=pl.BlockSpec(memory_space=pl.ANY),
            scratch_shapes=[pltpu.VMEM((2, p, da), BF16), pltpu.VMEM((SLOT_ALIGN, da), BF16),
                            pltpu.SemaphoreType.DMA((2,))],
        ),
        out_shape=jax.ShapeDtypeStruct((s_rows, da), BF16),
        compiler_params=_params(("arbitrary",)),
        name="moe_dispatch",
    )(plan["dst"], plan["off"], plan["n"], plan["slack_dst"], plan["slack_n"], xa.reshape(n, da), info_lm)

    w_map = lambda r, e_ref, blk_ref, valid_ref: (layer, e_ref[r], 0, 0)
    row_map = lambda r, e_ref, blk_ref, valid_ref: (blk_ref[r], 0)
    ys = pl.pallas_call(
        functools.partial(_expert_kernel, d=d),
        grid_spec=pltpu.PrefetchScalarGridSpec(
            num_scalar_prefetch=3,
            grid=(n_row_tiles,),
            in_specs=[
                pl.BlockSpec((tm, da), row_map),
                pl.BlockSpec((1, 1, d, f), w_map),
                pl.BlockSpec((1, 1, d, f), w_map),
                pl.BlockSpec((1, 1, f, d), w_map),
            ],
            out_specs=pl.BlockSpec((tm, d), row_map),
        ),
        out_shape=jax.ShapeDtypeStruct((s_rows, d), BF16),
        compiler_params=_params(("arbitrary",)),
        name="moe_experts",
    )(plan["expert"], plan["block"], plan["valid"], xs, wg_bf, wu_bf, wd_bf)

    final_norm = final_g is not None
    in_specs = [
        pl.BlockSpec(memory_space=pl.ANY),
        pl.BlockSpec((t, ROUTER_LANES), lambda i, *_: (i, 0)),
        pl.BlockSpec((t, d), lambda i, *_: (i, 0)),
        pl.BlockSpec((1, 1, d), lambda i, *_: (i // tiles_per_batch, 0, 0)),
    ]
    args = [ys, info_tm.reshape(n, ROUTER_LANES), x.reshape(n, d), g2]
    if final_norm:
        in_specs.append(pl.BlockSpec((1, d), lambda i, *_: (0, 0)))
        args.append(final_g.reshape(1, d))
    out = pl.pallas_call(
        functools.partial(_combine_kernel, final_norm=final_norm),
        grid_spec=pltpu.PrefetchScalarGridSpec(
            num_scalar_prefetch=3,
            grid=(n_tiles,),
            in_specs=in_specs,
            out_specs=pl.BlockSpec((t, d), lambda i, *_: (i, 0)),
            scratch_shapes=[pltpu.VMEM((2, p, d), BF16), pltpu.SemaphoreType.DMA((2,))],
        ),
        out_shape=jax.ShapeDtypeStruct((n, d), F32),
        compiler_params=_params(("arbitrary",)),
        name="moe_combine",
    )(plan["dst"], plan["off"], plan["n"], *args)
    return out.reshape(b, l, d)


def kernel(x, c, ctx, c_ctx, ada_w, ada_b, norm1_g, norm2_g, w_in, pool_w, pool_scale, lambda_q1, lambda_k1,
           lambda_q2, lambda_k2, subln_g, w_out, router_coarse_w, router_coarse_b, router_fine_w,
           router_fine_b, w_gate, w_up, w_down, final_g):
    b, l, d = x.shape
    depth = ada_w.shape[0]
    assert b + 1 <= MOD_ROWS and l % GRID_W == 0

    tables = _rope_tables(l)
    cc = jnp.zeros((MOD_ROWS, d), F32).at[:b].set(c).at[b].set(c_ctx)
    d_pool = d // 2
    d_attn = (w_in.shape[-1] - d_pool) // 3
    w_uk = jnp.concatenate([w_in[..., :d_pool], w_in[..., d_pool + d_attn:d_pool + 2 * d_attn]],
                           axis=-1).astype(BF16)
    w_qvt = jnp.swapaxes(jnp.concatenate([w_in[..., d_pool:d_pool + d_attn], w_in[..., d_pool + 2 * d_attn:]],
                                         axis=-1), 1, 2).astype(BF16)
    pool_w_bf = pool_w.astype(BF16)
    w_out_bf = w_out.astype(BF16)
    n_e = w_gate.shape[1] * w_gate.shape[2]
    f = w_gate.shape[-1]
    wg_bf = w_gate.astype(BF16).reshape(depth, n_e, d, f)
    wu_bf = w_up.astype(BF16).reshape(depth, n_e, d, f)
    wd_bf = w_down.astype(BF16).reshape(depth, n_e, f, d)
    pad = ROUTER_LANES - N_GROUPS - N_EXPERTS
    wr = jnp.concatenate([router_coarse_w, router_fine_w, jnp.zeros((depth, d, pad), F32)], axis=-1)
    wr_hi = wr.astype(BF16)
    wr = jnp.stack([wr_hi, (wr - wr_hi.astype(F32)).astype(BF16)], axis=1)
    rb = jnp.concatenate([router_coarse_b, router_fine_b, jnp.zeros((depth, pad), F32)], axis=-1)

    xc = ctx
    for layer in range(depth):
        last = layer == depth - 1
        lam_init = 0.8 - 0.6 * math.exp(-0.3 * layer)
        mod = _ada(cc, ada_w, ada_b, layer)
        sh1, sc1, g1, sh2, sc2, g2 = jnp.split(mod[:b, None, :], 6, axis=-1)
        csh1, csc1, cg1, csh2, csc2, cg2 = jnp.split(
            jnp.broadcast_to(mod[b][None, None, :], (b, 1, mod.shape[1])), 6, axis=-1)
        lam4 = jnp.stack([lambda_q1[layer], lambda_k1[layer], lambda_q2[layer], lambda_k2[layer]])

        u, q, k, v = _inproj(x, sh1, sc1, norm1_g[layer], w_uk[layer], w_qvt[layer], tables)
        uc, qc, kc, vc = _inproj(xc, csh1, csc1, norm1_g[layer], w_uk[layer], w_qvt[layer], None)
        attn = _attention(q, k, v, kc, vc, lam4, subln_g[layer], lam_init)
        merge_w = (pool_w_bf[layer], pool_scale[layer], w_out_bf[layer])
        route_w = (wr[layer], rb[layer].reshape(1, ROUTER_LANES))
        x, *routed = _merge(u, attn, x, *merge_w, g1, norm2_g[layer], sh2, sc2, *route_w)
        if not last:
            attn_c = _attention(qc, kc, vc, None, None, lam4, subln_g[layer], lam_init)
            xc, *routed_c = _merge(uc, attn_c, xc, *merge_w, cg1, norm2_g[layer], csh2, csc2, *route_w)
            xc = _moe(xc, *routed_c, cg2, wg_bf, wu_bf, wd_bf, layer, None)
        x = _moe(x, *routed, g2, wg_bf, wu_bf, wd_bf, layer, final_g if last else None)
    return x
```

```python
import functools
import math

import jax
import jax.numpy as jnp
from jax import lax
from jax.experimental import pallas as pl
from jax.experimental.pallas import tpu as pltpu

F32 = jnp.float32
BF16 = jnp.bfloat16

GRID_W = 64
POOL_WINDOWS = (2, 4, 8, 16)
HEAD_DIM = 64
HEAD_PAIR = 2 * HEAD_DIM
ROPE_THETA = 10000.0
N_GROUPS = 4
EXPERTS_PER_GROUP = 8
N_EXPERTS = N_GROUPS * EXPERTS_PER_GROUP
EPS = 1e-6
HALO = 16
ROUTER_LANES = 128
MOD_ROWS = 8
ONES_ROWS = 16
MAX_EXCESS = 24.0
KEY_GROUP = 4
SLOT_ALIGN = 16
MOE_TOKEN_TILE = 512
MOE_ROW_TILE = 512

VMEM_LIMIT = 48 * 1024 * 1024


def _params(sem):
    return pltpu.CompilerParams(dimension_semantics=sem, vmem_limit_bytes=VMEM_LIMIT)


def _ada_kernel(cc_ref, w_ref, b_ref, o_ref):
    s = cc_ref[...]
    s = s * jax.nn.sigmoid(s)
    o_ref[...] = jnp.dot(s, w_ref[0], precision=lax.Precision.HIGHEST,
                         preferred_element_type=F32) + b_ref[0]


def _ada(cc, ada_w, ada_b, layer):
    _, d, n = ada_w.shape
    bn = 1024
    return pl.pallas_call(
        _ada_kernel,
        grid=(n // bn,),
        in_specs=[
            pl.BlockSpec((MOD_ROWS, d), lambda j: (0, 0)),
            pl.BlockSpec((1, d, bn), lambda j: (layer, 0, j)),
            pl.BlockSpec((1, 1, bn), lambda j: (layer, 0, j)),
        ],
        out_specs=pl.BlockSpec((MOD_ROWS, bn), lambda j: (0, j)),
        out_shape=jax.ShapeDtypeStruct((MOD_ROWS, n), F32),
        compiler_params=_params(("arbitrary",)),
        name="ada",
    )(cc, ada_w, ada_b.reshape(ada_b.shape[0], 1, n))


def _rope_tables(n):
    pos = jnp.arange(n, dtype=jnp.int32)
    row = (pos // GRID_W).astype(F32)
    col = (pos % GRID_W).astype(F32)
    n_freq = HEAD_DIM // 4
    inv = ROPE_THETA ** (-jnp.arange(n_freq, dtype=F32) / n_freq)
    ang_r = row[:, None] * inv
    ang_c = col[:, None] * inv
    cos = jnp.concatenate([jnp.cos(ang_r), jnp.cos(ang_r), jnp.cos(ang_c), jnp.cos(ang_c)], axis=-1)
    sin = jnp.concatenate([-jnp.sin(ang_r), jnp.sin(ang_r), -jnp.sin(ang_c), jnp.sin(ang_c)], axis=-1)
    cos, sin = jnp.tile(cos, (1, 2)), jnp.tile(sin, (1, 2))
    return cos, sin, cos.T, sin.T


def _rmsnorm(x, g):
    ms = jnp.mean(x * x, axis=-1, keepdims=True)
    return x * lax.rsqrt(ms + EPS) * g


def _inproj_kernel(*refs, rope, d_pool, d_attn, q_scale):
    if rope:
        (x_ref, sh_ref, sc_ref, g_ref, wuk_ref, wqv_ref, cos_ref, sin_ref, cost_ref, sint_ref,
         u_ref, qt_ref, k_ref, vt_ref) = refs
    else:
        x_ref, sh_ref, sc_ref, g_ref, wuk_ref, wqv_ref, u_ref, qt_ref, k_ref, vt_ref = refs
    h = (_rmsnorm(x_ref[0], g_ref[...]) * (1.0 + sc_ref[0]) + sh_ref[0]).astype(BF16)
    p = jnp.dot(h, wuk_ref[...], preferred_element_type=F32)
    pt = lax.dot_general(wqv_ref[...], h, (((1,), (1,)), ((), ())), preferred_element_type=F32)
    u_ref[0] = p[:, :d_pool]
    vt_ref[0] = pt[d_attn:].astype(BF16)
    half = HEAD_DIM // 4
    if rope:
        cos, sin = cos_ref[...], sin_ref[...]
        cost, sint = cost_ref[...], sint_ref[...]
        lane = lax.broadcasted_iota(jnp.int32, cos.shape, 1)
        first_half = (lane % (2 * half)) < half
    for h0 in range(0, d_attn, HEAD_PAIR):
        c = p[:, d_pool + h0: d_pool + h0 + HEAD_PAIR]
        ct = pt[h0:h0 + HEAD_PAIR]
        if rope:
            partner = jnp.where(first_half, pltpu.roll(c, HEAD_PAIR - half, 1), pltpu.roll(c, half, 1))
            c = c * cos + partner * sin
            swapped = []
            for r0 in range(0, HEAD_PAIR, 2 * half):
                swapped += [ct[r0 + half:r0 + 2 * half], ct[r0:r0 + half]]
            ct = ct * cost + jnp.concatenate(swapped, axis=0) * sint
        k_ref[0, :, h0:h0 + HEAD_PAIR] = c.astype(BF16)
        qt_ref[0, h0:h0 + HEAD_PAIR, :] = (ct * q_scale).astype(BF16)


def _inproj(x, shift, scale, g, w_uk, w_qvt, tables):
    b, l, d = x.shape
    d_pool = d // 2
    d_attn = w_qvt.shape[0] // 2
    t = min(512, l)
    rope = tables is not None
    row = lambda bi, i: (bi, i, 0)
    col = lambda bi, i: (bi, 0, i)
    per_batch = lambda bi, i: (bi, 0, 0)
    const = lambda bi, i: (0, 0)
    in_specs = [
        pl.BlockSpec((1, t, d), row),
        pl.BlockSpec((1, 1, d), per_batch),
        pl.BlockSpec((1, 1, d), per_batch),
        pl.BlockSpec((1, d), const),
        pl.BlockSpec(w_uk.shape, const),
        pl.BlockSpec(w_qvt.shape, const),
    ]
    args = [x, shift, scale, g.reshape(1, d), w_uk, w_qvt]
    if rope:
        in_specs += [pl.BlockSpec((t, HEAD_PAIR), lambda bi, i: (i, 0))] * 2
        in_specs += [pl.BlockSpec((HEAD_PAIR, t), lambda bi, i: (0, i))] * 2
        args += list(tables)
    return pl.pallas_call(
        functools.partial(_inproj_kernel, rope=rope, d_pool=d_pool, d_attn=d_attn,
                          q_scale=HEAD_DIM ** -0.5 * math.log2(math.e)),
        grid=(b, l // t),
        in_specs=in_specs,
        out_specs=[pl.BlockSpec((1, t, d_pool), row), pl.BlockSpec((1, d_attn, t), col),
                   pl.BlockSpec((1, t, d_attn), row), pl.BlockSpec((1, d_attn, t), col)],
        out_shape=[jax.ShapeDtypeStruct((b, l, d_pool), F32), jax.ShapeDtypeStruct((b, d_attn, l), BF16),
                   jax.ShapeDtypeStruct((b, l, d_attn), BF16), jax.ShapeDtypeStruct((b, d_attn, l), BF16)],
        compiler_params=_params(("arbitrary", "arbitrary")),
        name="inproj",
    )(*args)


def _attn_kernel(*refs, tq, tk, cb, n_chunks, group, has_ctx, lam_init):
    if has_ctx:
        (lam_ref, qt_ref, k_ref, vt_ref, kc_ref, vct_ref, g_ref, o_ref,
         qs_scr, m_scr, l_scr, acc_scr, s_scr) = refs
    else:
        lam_ref, qt_ref, k_ref, vt_ref, g_ref, o_ref, qs_scr, m_scr, l_scr, acc_scr, s_scr = refs
    qt = qt_ref[0]
    sub = lax.broadcasted_iota(jnp.int32, qt.shape, 0)
    zero = jnp.zeros_like(qt)
    qs_scr[:, :tq] = jnp.where(sub < HEAD_DIM, qt, zero)
    qs_scr[:, tq:] = jnp.where(sub >= HEAD_DIM, qt, zero)
    m_scr[...] = jnp.full(m_scr.shape, -jnp.inf, F32)
    l_scr[...] = jnp.zeros(l_scr.shape, F32)
    acc_scr[...] = jnp.zeros(acc_scr.shape, F32)

    n_blocks = 2 * tq // cb
    blocks = [slice(c * cb, (c + 1) * cb) for c in range(n_blocks)]

    def scores(kj, c):
        return jnp.dot(kj, qs_scr[:, blocks[c]], preferred_element_type=F32)

    def with_ones(vtj):
        return jnp.concatenate([vtj, jnp.ones((ONES_ROWS, vtj.shape[1]), BF16)], axis=0)

    def span(j):
        return pl.ds(j * tk if isinstance(j, int) else pl.multiple_of(j * tk, tk), tk)

    def keys(j):
        return k_ref[0, span(j), :]

    def values_t(j):
        return vt_ref[0, :, span(j)]

    def online_chunk(kj, vtj):
        vta = with_ones(vtj)
        for c, cols in enumerate(blocks):
            s = scores(kj, c)
            m_old = m_scr[:, cols]
            m_new = jnp.maximum(m_old, jnp.max(s, axis=0, keepdims=True))
            alpha = jnp.exp2(m_old - m_new)
            p = jnp.exp2(s - m_new).astype(BF16)
            r = jnp.dot(vta, p, preferred_element_type=F32)
            l_scr[:, cols] = alpha * l_scr[:, cols] + r[HEAD_PAIR:HEAD_PAIR + 1]
            acc_scr[:, cols] = alpha * acc_scr[:, cols] + r[:HEAD_PAIR]
            m_scr[:, cols] = m_new

    def group_fast(j0, j_next):
        items = [(c, jj) for c in range(n_blocks) for jj in range(group)]
        s_next = s_scr[...]
        sums, excess = [], None
        for idx, (c, jj) in enumerate(items):
            s = s_next
            if idx + 1 < len(items):
                c2, jj2 = items[idx + 1]
                s_next = scores(keys(j0 + jj2), c2)
            elif j_next is not None:
                s_scr[...] = scores(keys(j_next), 0)
            m_ref = m_scr[:, blocks[c]]
            top = jnp.max(s, axis=0, keepdims=True)
            p = jnp.exp2(s - m_ref).astype(BF16)
            r = jnp.dot(with_ones(values_t(j0 + jj)), p, preferred_element_type=F32)
            r_c, top_c = (r, top) if jj == 0 else (r_c + r, jnp.maximum(top_c, top))
            if jj == group - 1:
                sums.append(r_c)
                over = jnp.max(top_c - m_ref)
                excess = over if excess is None else jnp.maximum(excess, over)
        ok = excess <= MAX_EXCESS

        @pl.when(ok)
        def _():
            for cols, r_c in zip(blocks, sums):
                l_scr[:, cols] += r_c[HEAD_PAIR:HEAD_PAIR + 1]
                acc_scr[:, cols] += r_c[:HEAD_PAIR]

        return ok

    def group_step(j0, j_next):
        ok = group_fast(j0, j_next)

        @pl.when(jnp.logical_not(ok))
        def _():
            def redo(jj, carry):
                online_chunk(keys(j0 + jj), values_t(j0 + jj))
                return carry

            lax.fori_loop(0, group, redo, 0)

    n_groups = n_chunks // group
    if has_ctx:
        online_chunk(kc_ref[0], vct_ref[0])
    s_scr[...] = scores(keys(0), 0)
    if n_groups > 1:
        def body(g, carry):
            group_step(g * group, (g + 1) * group)
            return carry

        lax.fori_loop(0, n_groups - 1, body, 0)
    group_step((n_groups - 1) * group, None)

    o = acc_scr[...] / l_scr[...]
    lv = lam_ref[...]
    lam = (jnp.exp(jnp.sum(lv[0:1] * lv[1:2], axis=1, keepdims=True))
           - jnp.exp(jnp.sum(lv[2:3] * lv[3:4], axis=1, keepdims=True)) + lam_init)
    a = (o[:, :tq] - lam * o[:, tq:]).T
    o_ref[0] = (_rmsnorm(a, g_ref[...]) * (1.0 - lam_init)).astype(BF16)


def _attention(qt, k, vt, kc, vct, lam4, subln_g, lam_init):
    b, d_attn, lq = qt.shape
    lk = k.shape[1]
    n_heads = d_attn // HEAD_PAIR
    tq = min(1024, lq)
    tk = min(512, lk)
    cb = min(512, 2 * tq)
    has_ctx = kc is not None
    qmap = lambda bi, h, i: (bi, h, i)
    kmap = lambda bi, h, i: (bi, 0, h)
    vmap = lambda bi, h, i: (bi, h, 0)
    const = lambda bi, h, i: (0, 0)
    in_specs = [
        pl.BlockSpec(lam4.shape, const),
        pl.BlockSpec((1, HEAD_PAIR, tq), qmap),
        pl.BlockSpec((1, lk, HEAD_PAIR), kmap),
        pl.BlockSpec((1, HEAD_PAIR, lk), vmap),
    ]
    args = [lam4, qt, k, vt]
    if has_ctx:
        lc = kc.shape[1]
        in_specs += [pl.BlockSpec((1, lc, HEAD_PAIR), kmap), pl.BlockSpec((1, HEAD_PAIR, lc), vmap)]
        args += [kc, vct]
    in_specs.append(pl.BlockSpec((1, HEAD_PAIR), const))
    args.append(subln_g.reshape(1, HEAD_PAIR))
    return pl.pallas_call(
        functools.partial(_attn_kernel, tq=tq, tk=tk, cb=cb, n_chunks=lk // tk,
                          group=math.gcd(KEY_GROUP, lk // tk), has_ctx=has_ctx,
                          lam_init=lam_init),
        grid=(b, n_heads, lq // tq),
        in_specs=in_specs,
        out_specs=pl.BlockSpec((1, tq, HEAD_PAIR), lambda bi, h, i: (bi, i, h)),
        out_shape=jax.ShapeDtypeStruct((b, lq, d_attn), BF16),
        scratch_shapes=[
            pltpu.VMEM((HEAD_PAIR, 2 * tq), BF16),
            pltpu.VMEM((1, 2 * tq), F32),
            pltpu.VMEM((1, 2 * tq), F32),
            pltpu.VMEM((HEAD_PAIR, 2 * tq), F32),
            pltpu.VMEM((tk, cb), F32),
        ],
        compiler_params=_params(("arbitrary", "arbitrary", "arbitrary")),
        name="diff_attn",
    )(*args)


def _route(logits):
    lane = lax.broadcasted_iota(jnp.int32, logits.shape, 1)
    big = jnp.int32(ROUTER_LANES)
    neg = jnp.float32(-jnp.inf)
    lc = jnp.where(lane < N_GROUPS, logits, neg)
    mc = jnp.max(lc, axis=1, keepdims=True)
    p_group = 1.0 / jnp.sum(jnp.exp(lc - mc), axis=1, keepdims=True)
    g_idx = jnp.min(jnp.where(lc == mc, lane, big), axis=1, keepdims=True)
    lo = N_GROUPS + EXPERTS_PER_GROUP * g_idx
    lf = jnp.where((lane >= lo) & (lane < lo + EXPERTS_PER_GROUP), logits, neg)
    v1 = jnp.max(lf, axis=1, keepdims=True)
    i1 = jnp.min(jnp.where(lf == v1, lane, big), axis=1, keepdims=True)
    lf2 = jnp.where(lane == i1, neg, lf)
    v2 = jnp.max(lf2, axis=1, keepdims=True)
    i2 = jnp.min(jnp.where(lf2 == v2, lane, big), axis=1, keepdims=True)
    e2 = jnp.exp(v2 - v1)
    w1 = p_group / (1.0 + e2)
    w2 = p_group * e2 / (1.0 + e2)
    return lane, i1, i2, w1, w2


def _split_bf16(x):
    hi = x.astype(BF16)
    return hi, (x - hi.astype(F32)).astype(BF16)


def _sorted_positions(lane, i1, i2):
    t = lane.shape[0]
    oh1 = lane == i1
    oh2 = lane == i2
    oh = jnp.concatenate([oh1, oh2], axis=1).astype(BF16)
    r = lax.broadcasted_iota(jnp.int32, (t, t), 0)
    c = lax.broadcasted_iota(jnp.int32, (t, t), 1)
    before = (c < r).astype(BF16)
    rank = jnp.dot(before, oh, preferred_element_type=F32)
    cnt1 = jnp.sum(oh1.astype(F32), axis=0, keepdims=True)
    cnt2 = jnp.sum(oh2.astype(F32), axis=0, keepdims=True)
    run = jnp.ceil((cnt1 + cnt2) * (1.0 / SLOT_ALIGN)) * SLOT_ALIGN
    lr = lax.broadcasted_iota(jnp.int32, (ROUTER_LANES, ROUTER_LANES), 0)
    lc = lax.broadcasted_iota(jnp.int32, (ROUTER_LANES, ROUTER_LANES), 1)
    start = jnp.dot(jnp.broadcast_to(run, (8, ROUTER_LANES)).astype(BF16), (lr < lc).astype(BF16),
                    preferred_element_type=F32)[0:1]
    pos1 = jnp.sum(jnp.where(oh1, start + rank[:, :ROUTER_LANES], 0.0), axis=1, keepdims=True)
    pos2 = jnp.sum(jnp.where(oh2, start + cnt1 + rank[:, ROUTER_LANES:], 0.0), axis=1, keepdims=True)
    return pos1, pos2, run


def _merge_kernel(u_ref, up_ref, un_ref, a_ref, x_ref, pw_ref, ps_ref, wo_ref, g1_ref, n2_ref, sh_ref,
                  sc_ref, wr_ref, rb_ref, xo_ref, xa_ref, tm_ref, lm_ref, run_ref, ext_scr, *, t, seq_len,
                  d_pool):
    i = pl.program_id(1)
    n_tiles = pl.num_programs(1)
    u = u_ref[0]
    ext_scr[0:HALO] = jnp.where(i > 0, up_ref[0], 0.0)
    ext_scr[HALO:HALO + t] = u
    ext_scr[HALO + t:] = jnp.where(i < n_tiles - 1, un_ref[0], 0.0)

    pos = i * t + lax.broadcasted_iota(jnp.int32, (t, 1), 0)
    gc = d_pool // len(POOL_WINDOWS)
    mix = jnp.zeros((t, wo_ref.shape[1]), F32)
    for gi, w in enumerate(POOL_WINDOWS):
        left = w // 2
        right = w - 1 - left
        cols = slice(gi * gc, (gi + 1) * gc)
        s = ext_scr[pl.ds(HALO - left, t), cols]
        for o in range(-left + 1, right + 1):
            s = s + ext_scr[pl.ds(HALO + o, t), cols]
        cnt = (jnp.minimum(pos + right + 1, seq_len) - jnp.maximum(pos - left, 0)).astype(F32)
        pooled = s / cnt - u[:, cols]
        mixed = jnp.dot(pooled.astype(BF16), pw_ref[gi], preferred_element_type=F32) * ps_ref[:, cols]
        mix = mix + jnp.dot(mixed.astype(BF16), wo_ref[cols, :], preferred_element_type=F32)
    mix = mix + jnp.dot(a_ref[0], wo_ref[d_pool:, :], preferred_element_type=F32)

    x_new = x_ref[0] + g1_ref[0] * mix
    xo_ref[0] = x_new
    h2 = _rmsnorm(x_new, n2_ref[...]) * (1.0 + sc_ref[0]) + sh_ref[0]
    h_hi, h_lo = _split_bf16(h2)
    d = h2.shape[1]
    logits = (jnp.dot(h_hi, wr_ref[0], preferred_element_type=F32)
              + jnp.dot(h_lo, wr_ref[0], preferred_element_type=F32)
              + jnp.dot(h_hi, wr_ref[1], preferred_element_type=F32)) + rb_ref[...]
    lane, i1, i2, w1, w2 = _route(logits)
    pos1, pos2, run = _sorted_positions(lane, i1, i2)
    w1_hi = w1.astype(BF16).astype(F32)
    w2_hi = w2.astype(BF16).astype(F32)
    aux = jnp.zeros(lane.shape, F32)
    for k, val in enumerate((w1_hi, w1 - w1_hi, w2_hi, w2 - w2_hi, (i1 - N_GROUPS).astype(F32))):
        aux = jnp.where(lane == k, val, aux)
    xa_ref[0, :, :d] = h_hi
    xa_ref[0, :, d:] = aux.astype(BF16)
    info = jnp.where(lane == 0, pos1, jnp.where(lane == 1, pos2, 0.0))
    tm_ref[0] = info
    lm_ref[0] = info.T[:8]
    run_ref[0, 0] = run


def _merge(u, attn, x, pool_w_bf, pool_scale, w_out_bf, g1, norm2_g, sh2, sc2, wr, rb):
    b, l, d = x.shape
    d_pool = u.shape[2]
    t = min(MOE_TOKEN_TILE, l)
    hb = t // HALO
    n_halo = l // HALO
    row = lambda bi, i: (bi, i, 0)
    per_batch = lambda bi, i: (bi, 0, 0)
    const2 = lambda bi, i: (0, 0)
    const3 = lambda bi, i: (0, 0, 0)
    return pl.pallas_call(
        functools.partial(_merge_kernel, t=t, seq_len=l, d_pool=d_pool),
        grid=(b, l // t),
        in_specs=[
            pl.BlockSpec((1, t, d_pool), row),
            pl.BlockSpec((1, HALO, d_pool), lambda bi, i: (bi, jnp.maximum(i * hb - 1, 0), 0)),
            pl.BlockSpec((1, HALO, d_pool), lambda bi, i: (bi, jnp.minimum((i + 1) * hb, n_halo - 1), 0)),
            pl.BlockSpec((1, t, attn.shape[2]), row),
            pl.BlockSpec((1, t, d), row),
            pl.BlockSpec(pool_w_bf.shape, const3),
            pl.BlockSpec((1, d_pool), const2),
            pl.BlockSpec(w_out_bf.shape, const2),
            pl.BlockSpec((1, 1, d), per_batch),
            pl.BlockSpec((1, d), const2),
            pl.BlockSpec((1, 1, d), per_batch),
            pl.BlockSpec((1, 1, d), per_batch),
            pl.BlockSpec(wr.shape, const3),
            pl.BlockSpec((1, ROUTER_LANES), const2),
        ],
        out_specs=[
            pl.BlockSpec((1, t, d), row),
            pl.BlockSpec((1, t, d + ROUTER_LANES), row),
            pl.BlockSpec((1, t, ROUTER_LANES), row),
            pl.BlockSpec((1, 8, t), lambda bi, i: (bi, 0, i)),
            pl.BlockSpec((1, 1, 1, ROUTER_LANES), lambda bi, i: (bi, i, 0, 0)),
        ],
        out_shape=[
            jax.ShapeDtypeStruct((b, l, d), F32),
            jax.ShapeDtypeStruct((b, l, d + ROUTER_LANES), BF16),
            jax.ShapeDtypeStruct((b, l, ROUTER_LANES), F32),
            jax.ShapeDtypeStruct((b, 8, l), F32),
            jax.ShapeDtypeStruct((b, l // t, 1, ROUTER_LANES), F32),
        ],
        scratch_shapes=[pltpu.VMEM((t + 2 * HALO, d_pool), F32)],
        compiler_params=_params(("arbitrary", "arbitrary")),
        name="merge",
    )(u, u, u, attn, x, pool_w_bf, pool_scale.reshape(1, d_pool), w_out_bf, g1, norm2_g.reshape(1, d),
      sh2, sc2, wr, rb)


def _moe_plan(runs, t, tm):
    counts = runs[:, N_GROUPS:N_GROUPS + N_EXPERTS].astype(jnp.int32)
    n_tiles = counts.shape[0]
    total = jnp.sum(counts, axis=0)
    region = (total + tm - 1) // tm * tm
    region_end = jnp.cumsum(region)
    start = region_end - region
    dst = start[None, :] + jnp.cumsum(counts, axis=0) - counts
    off = jnp.cumsum(counts, axis=1) - counts
    s_max = 2 * n_tiles * t + n_tiles * N_EXPERTS * SLOT_ALIGN + N_EXPERTS * tm
    n_row_tiles = -(-s_max // tm)
    first_row = jnp.arange(n_row_tiles, dtype=jnp.int32) * tm
    valid = first_row < region_end[-1]
    last_valid = jnp.maximum(region_end[-1] // tm - 1, 0)
    block = jnp.where(valid, jnp.arange(n_row_tiles, dtype=jnp.int32), last_valid)
    expert = jnp.sum((region_end[None, :] <= (block * tm)[:, None]).astype(jnp.int32), axis=1)
    expert = jnp.minimum(expert, N_EXPERTS - 1)
    plan = dict(dst=dst.reshape(-1), off=off.reshape(-1), n=(counts // SLOT_ALIGN).reshape(-1),
                slack_dst=start + total, slack_n=(region - total) // SLOT_ALIGN,
                expert=expert, block=block, valid=valid.astype(jnp.int32))
    return plan, n_row_tiles


def _start_runs(n_ref, tile, src_of, dst_of, sem):
    for e in range(N_EXPERTS):
        n = n_ref[tile * N_EXPERTS + e]

        def issue(j, carry, e=e):
            pltpu.make_async_copy(src_of(e, j), dst_of(e, j), sem).start()
            return carry

        lax.fori_loop(0, n, issue, 0)


def _wait_runs(n_ref, tile, src, dst, sem):
    total = 0
    for e in range(N_EXPERTS):
        total = total + n_ref[tile * N_EXPERTS + e]

    def drain(j, carry):
        pltpu.make_async_copy(src, dst, sem).wait()
        return carry

    lax.fori_loop(0, total, drain, 0)


def _span(first):
    return pl.ds(pl.multiple_of(first, SLOT_ALIGN), SLOT_ALIGN)


def _dispatch_kernel(dst_ref, off_ref, n_ref, sdst_ref, sn_ref, xa_ref, lm_ref, xs_hbm, buf, zbuf, sems, *, p):
    i = pl.program_id(0)
    last = pl.num_programs(0) - 1
    slot = i % 2
    t = xa_ref.shape[0]
    one_src, one_dst = buf.at[0, pl.ds(0, SLOT_ALIGN)], xs_hbm.at[pl.ds(0, SLOT_ALIGN)]

    @pl.when(i >= 2)
    def _():
        _wait_runs(n_ref, i - 2, one_src, one_dst, sems.at[slot])

    pos1 = lm_ref[0, 0:1, :].astype(jnp.int32)
    pos2 = lm_ref[0, 1:2, :].astype(jnp.int32)
    row = lax.broadcasted_iota(jnp.int32, (p, t), 0)
    perm = ((row == pos1) | (row == pos2)).astype(BF16)
    buf[slot] = jnp.dot(perm, xa_ref[...], preferred_element_type=F32).astype(BF16)
    _start_runs(n_ref, i,
                lambda e, j: buf.at[slot, _span(off_ref[i * N_EXPERTS + e] + j * SLOT_ALIGN)],
                lambda e, j: xs_hbm.at[_span(dst_ref[i * N_EXPERTS + e] + j * SLOT_ALIGN)], sems.at[slot])

    @pl.when(i == last)
    def _():
        @pl.when(i >= 1)
        def _():
            _wait_runs(n_ref, i - 1, one_src, one_dst, sems.at[1 - slot])

        _wait_runs(n_ref, i, one_src, one_dst, sems.at[slot])
        zbuf[...] = jnp.zeros(zbuf.shape, BF16)
        _start_runs(sn_ref, 0, lambda e, j: zbuf, lambda e, j: xs_hbm.at[_span(sdst_ref[e] + j * SLOT_ALIGN)],
                    sems.at[slot])
        _wait_runs(sn_ref, 0, one_src, one_dst, sems.at[slot])


def _expert_kernel(e_ref, blk_ref, valid_ref, xs_ref, wg_ref, wu_ref, wd_ref, ys_ref, *, d):
    r = pl.program_id(0)

    @pl.when(valid_ref[r] == 1)
    def _():
        x = xs_ref[:, :d]
        aux = xs_ref[:, d:].astype(F32)
        is_first = aux[:, 4:5] == e_ref[r].astype(F32)
        gate = jnp.where(is_first, aux[:, 0:1] + aux[:, 1:2], aux[:, 2:3] + aux[:, 3:4])
        a = jnp.dot(x, wg_ref[0, 0], preferred_element_type=F32)
        bu = jnp.dot(x, wu_ref[0, 0], preferred_element_type=F32)
        hid = (a * jax.nn.sigmoid(a)) * bu * gate
        ys_ref[...] = jnp.dot(hid.astype(BF16), wd_ref[0, 0], preferred_element_type=F32).astype(BF16)


def _combine_kernel(*refs, final_norm):
    if final_norm:
        dst_ref, off_ref, n_ref, ys_hbm, tm_ref, x_ref, g2_ref, fg_ref, o_ref, buf, sems = refs
    else:
        dst_ref, off_ref, n_ref, ys_hbm, tm_ref, x_ref, g2_ref, o_ref, buf, sems = refs
    i = pl.program_id(0)
    slot = i % 2
    t = x_ref.shape[0]
    p = buf.shape[1]

    def fetch(tile, s):
        buf[s] = jnp.zeros(buf.shape[1:], BF16)
        _start_runs(n_ref, tile,
                    lambda e, j: ys_hbm.at[_span(dst_ref[tile * N_EXPERTS + e] + j * SLOT_ALIGN)],
                    lambda e, j: buf.at[s, _span(off_ref[tile * N_EXPERTS + e] + j * SLOT_ALIGN)], sems.at[s])

    @pl.when(i == 0)
    def _():
        fetch(i, slot)

    @pl.when(i + 1 < pl.num_programs(0))
    def _():
        fetch(i + 1, 1 - slot)

    _wait_runs(n_ref, i, ys_hbm.at[pl.ds(0, SLOT_ALIGN)], buf.at[0, pl.ds(0, SLOT_ALIGN)], sems.at[slot])
    pos1 = tm_ref[:, 0:1].astype(jnp.int32)
    pos2 = tm_ref[:, 1:2].astype(jnp.int32)
    col = lax.broadcasted_iota(jnp.int32, (t, p), 1)
    perm_t = ((col == pos1) | (col == pos2)).astype(BF16)
    y = x_ref[...] + g2_ref[0] * jnp.dot(perm_t, buf[slot], preferred_element_type=F32)
    if final_norm:
        y = _rmsnorm(y, fg_ref[...])
    o_ref[...] = y


def _moe(x, xa, info_tm, info_lm, runs, g2, wg_bf, wu_bf, wd_bf, layer, final_g):
    b, l, d = x.shape
    da = xa.shape[2]
    f = wg_bf.shape[3]
    t = min(MOE_TOKEN_TILE, l)
    tiles_per_batch = l // t
    n_tiles = b * tiles_per_batch
    p = 2 * t + N_EXPERTS * SLOT_ALIGN
    tm = MOE_ROW_TILE
    plan, n_row_tiles = _moe_plan(runs.reshape(n_tiles, ROUTER_LANES), t, tm)
    s_rows = n_row_tiles * tm
    n = b * l

    xs = pl.pallas_call(
        functools.partial(_dispatch_kernel, p=p),
        grid_spec=pltpu.PrefetchScalarGridSpec(
            num_scalar_prefetch=5,
            grid=(n_tiles,),
            in_specs=[
                pl.BlockSpec((t, da), lambda i, *_: (i, 0)),
                pl.BlockSpec((1, 8, t), lambda i, *_: (i // tiles_per_batch, 0, i % tiles_per_batch)),
            ],
            out_specs=pl.BlockSpec(memory_space=pl.ANY),
            scratch_shapes=[pltpu.VMEM((2, p, da), BF16), pltpu.VMEM((SLOT_ALIGN, da), BF16),
                            pltpu.SemaphoreType.DMA((2,))],
        ),
        out_shape=jax.ShapeDtypeStruct((s_rows, da), BF16),
        compiler_params=_params(("arbitrary",)),
        name="moe_dispatch",
    )(plan["dst"], plan["off"], plan["n"], plan["slack_dst"], plan["slack_n"], xa.reshape(n, da), info_lm)

    w_map = lambda r, e_ref, blk_ref, valid_ref: (layer, e_ref[r], 0, 0)
    row_map = lambda r, e_ref, blk_ref, valid_ref: (blk_ref[r], 0)
    ys = pl.pallas_call(
        functools.partial(_expert_kernel, d=d),
        grid_spec=pltpu.PrefetchScalarGridSpec(
            num_scalar_prefetch=3,
            grid=(n_row_tiles,),
            in_specs=[
                pl.BlockSpec((tm, da), row_map),
                pl.BlockSpec((1, 1, d, f), w_map),
                pl.BlockSpec((1, 1, d, f), w_map),
                pl.BlockSpec((1, 1, f, d), w_map),
            ],
            out_specs=pl.BlockSpec((tm, d), row_map),
        ),
        out_shape=jax.ShapeDtypeStruct((s_rows, d), BF16),
        compiler_params=_params(("arbitrary",)),
        name="moe_experts",
    )(plan["expert"], plan["block"], plan["valid"], xs, wg_bf, wu_bf, wd_bf)

    final_norm = final_g is not None
    in_specs = [
        pl.BlockSpec(memory_space=pl.ANY),
        pl.BlockSpec((t, ROUTER_LANES), lambda i, *_: (i, 0)),
        pl.BlockSpec((t, d), lambda i, *_: (i, 0)),
        pl.BlockSpec((1, 1, d), lambda i, *_: (i // tiles_per_batch, 0, 0)),
    ]
    args = [ys, info_tm.reshape(n, ROUTER_LANES), x.reshape(n, d), g2]
    if final_norm:
        in_specs.append(pl.BlockSpec((1, d), lambda i, *_: (0, 0)))
        args.append(final_g.reshape(1, d))
    out = pl.pallas_call(
        functools.partial(_combine_kernel, final_norm=final_norm),
        grid_spec=pltpu.PrefetchScalarGridSpec(
            num_scalar_prefetch=3,
            grid=(n_tiles,),
            in_specs=in_specs,
            out_specs=pl.BlockSpec((t, d), lambda i, *_: (i, 0)),
            scratch_shapes=[pltpu.VMEM((2, p, d), BF16), pltpu.SemaphoreType.DMA((2,))],
        ),
        out_shape=jax.ShapeDtypeStruct((n, d), F32),
        compiler_params=_params(("arbitrary",)),
        name="moe_combine",
    )(plan["dst"], plan["off"], plan["n"], *args)
    return out.reshape(b, l, d)


def kernel(x, c, ctx, c_ctx, ada_w, ada_b, norm1_g, norm2_g, w_in, pool_w, pool_scale, lambda_q1, lambda_k1,
           lambda_q2, lambda_k2, subln_g, w_out, router_coarse_w, router_coarse_b, router_fine_w,
           router_fine_b, w_gate, w_up, w_down, final_g):
    b, l, d = x.shape
    depth = ada_w.shape[0]
    assert b + 1 <= MOD_ROWS and l % GRID_W == 0

    tables = _rope_tables(l)
    cc = jnp.zeros((MOD_ROWS, d), F32).at[:b].set(c).at[b].set(c_ctx)
    d_pool = d // 2
    d_attn = (w_in.shape[-1] - d_pool) // 3
    w_uk = jnp.concatenate([w_in[..., :d_pool], w_in[..., d_pool + d_attn:d_pool + 2 * d_attn]],
                           axis=-1).astype(BF16)
    w_qvt = jnp.swapaxes(jnp.concatenate([w_in[..., d_pool:d_pool + d_attn], w_in[..., d_pool + 2 * d_attn:]],
                                         axis=-1), 1, 2).astype(BF16)
    pool_w_bf = pool_w.astype(BF16)
    w_out_bf = w_out.astype(BF16)
    n_e = w_gate.shape[1] * w_gate.shape[2]
    f = w_gate.shape[-1]
    wg_bf = w_gate.astype(BF16).reshape(depth, n_e, d, f)
    wu_bf = w_up.astype(BF16).reshape(depth, n_e, d, f)
    wd_bf = w_down.astype(BF16).reshape(depth, n_e, f, d)
    pad = ROUTER_LANES - N_GROUPS - N_EXPERTS
    wr = jnp.concatenate([router_coarse_w, router_fine_w, jnp.zeros((depth, d, pad), F32)], axis=-1)
    wr_hi = wr.astype(BF16)
    wr = jnp.stack([wr_hi, (wr - wr_hi.astype(F32)).astype(BF16)], axis=1)
    rb = jnp.concatenate([router_coarse_b, router_fine_b, jnp.zeros((depth, pad), F32)], axis=-1)

    xc = ctx
    for layer in range(depth):
        last = layer == depth - 1
        lam_init = 0.8 - 0.6 * math.exp(-0.3 * layer)
        mod = _ada(cc, ada_w, ada_b, layer)
        sh1, sc1, g1, sh2, sc2, g2 = jnp.split(mod[:b, None, :], 6, axis=-1)
        csh1, csc1, cg1, csh2, csc2, cg2 = jnp.split(
            jnp.broadcast_to(mod[b][None, None, :], (b, 1, mod.shape[1])), 6, axis=-1)
        lam4 = jnp.stack([lambda_q1[layer], lambda_k1[layer], lambda_q2[layer], lambda_k2[layer]])

        u, q, k, v = _inproj(x, sh1, sc1, norm1_g[layer], w_uk[layer], w_qvt[layer], tables)
        uc, qc, kc, vc = _inproj(xc, csh1, csc1, norm1_g[layer], w_uk[layer], w_qvt[layer], None)
        attn = _attention(q, k, v, kc, vc, lam4, subln_g[layer], lam_init)
        merge_w = (pool_w_bf[layer], pool_scale[layer], w_out_bf[layer])
        route_w = (wr[layer], rb[layer].reshape(1, ROUTER_LANES))
        x, *routed = _merge(u, attn, x, *merge_w, g1, norm2_g[layer], sh2, sc2, *route_w)
        if not last:
            attn_c = _attention(qc, kc, vc, None, None, lam4, subln_g[layer], lam_init)
            xc, *routed_c = _merge(uc, attn_c, xc, *merge_w, cg1, norm2_g[layer], csh2, csc2, *route_w)
            xc = _moe(xc, *routed_c, cg2, wg_bf, wu_bf, wd_bf, layer, None)
        x = _moe(x, *routed, g2, wg_bf, wu_bf, wd_bf, layer, final_g if last else None)
    return x
```

```python
import functools
import math

import jax
import jax.numpy as jnp
from jax import lax
from jax.experimental import pallas as pl
from jax.experimental.pallas import tpu as pltpu

F32 = jnp.float32
BF16 = jnp.bfloat16

GRID_W = 64
POOL_WINDOWS = (2, 4, 8, 16)
HEAD_DIM = 64
HEAD_PAIR = 2 * HEAD_DIM
ROPE_THETA = 10000.0
N_GROUPS = 4
EXPERTS_PER_GROUP = 8
N_EXPERTS = N_GROUPS * EXPERTS_PER_GROUP
EPS = 1e-6
HALO = 16
ROUTER_LANES = 128
MOD_ROWS = 8
ONES_ROWS = 16
MAX_EXCESS = 24.0
KEY_GROUP = 8
SLOT_ALIGN = 16
MOE_TOKEN_TILE = 512
MOE_ROW_TILE = 512
PERM_CHUNK = 256

VMEM_LIMIT = 48 * 1024 * 1024


def _params(sem):
    return pltpu.CompilerParams(dimension_semantics=sem, vmem_limit_bytes=VMEM_LIMIT)


def _ada_kernel(cc_ref, w_ref, b_ref, o_ref):
    s = cc_ref[...]
    s = s * jax.nn.sigmoid(s)
    o_ref[...] = jnp.dot(s, w_ref[0], precision=lax.Precision.HIGHEST,
                         preferred_element_type=F32) + b_ref[0]


def _ada(cc, ada_w, ada_b, layer):
    _, d, n = ada_w.shape
    bn = 1024
    return pl.pallas_call(
        _ada_kernel,
        grid=(n // bn,),
        in_specs=[
            pl.BlockSpec((MOD_ROWS, d), lambda j: (0, 0)),
            pl.BlockSpec((1, d, bn), lambda j: (layer, 0, j)),
            pl.BlockSpec((1, 1, bn), lambda j: (layer, 0, j)),
        ],
        out_specs=pl.BlockSpec((MOD_ROWS, bn), lambda j: (0, j)),
        out_shape=jax.ShapeDtypeStruct((MOD_ROWS, n), F32),
        compiler_params=_params(("arbitrary",)),
        name="ada",
    )(cc, ada_w, ada_b.reshape(ada_b.shape[0], 1, n))


def _rope_tables(n):
    pos = jnp.arange(n, dtype=jnp.int32)
    row = (pos // GRID_W).astype(F32)
    col = (pos % GRID_W).astype(F32)
    n_freq = HEAD_DIM // 4
    inv = ROPE_THETA ** (-jnp.arange(n_freq, dtype=F32) / n_freq)
    ang_r = row[:, None] * inv
    ang_c = col[:, None] * inv
    cos = jnp.concatenate([jnp.cos(ang_r), jnp.cos(ang_r), jnp.cos(ang_c), jnp.cos(ang_c)], axis=-1)
    sin = jnp.concatenate([-jnp.sin(ang_r), jnp.sin(ang_r), -jnp.sin(ang_c), jnp.sin(ang_c)], axis=-1)
    cos, sin = jnp.tile(cos, (1, 2)), jnp.tile(sin, (1, 2))
    return cos, sin, cos.T, sin.T


def _rmsnorm(x, g):
    ms = jnp.mean(x * x, axis=-1, keepdims=True)
    return x * lax.rsqrt(ms + EPS) * g


def _inproj_kernel(*refs, rope, d_pool, d_attn, q_scale):
    if rope:
        (x_ref, sh_ref, sc_ref, g_ref, wuk_ref, wqv_ref, cos_ref, sin_ref, cost_ref, sint_ref,
         u_ref, qt_ref, k_ref, vt_ref) = refs
    else:
        x_ref, sh_ref, sc_ref, g_ref, wuk_ref, wqv_ref, u_ref, qt_ref, k_ref, vt_ref = refs
    h = (_rmsnorm(x_ref[0], g_ref[...]) * (1.0 + sc_ref[0]) + sh_ref[0]).astype(BF16)
    p = jnp.dot(h, wuk_ref[...], preferred_element_type=F32)
    pt = lax.dot_general(wqv_ref[...], h, (((1,), (1,)), ((), ())), preferred_element_type=F32)
    u_ref[0] = p[:, :d_pool]
    vt_ref[0] = pt[d_attn:].astype(BF16)
    half = HEAD_DIM // 4
    if rope:
        cos, sin = cos_ref[...], sin_ref[...]
        cost, sint = cost_ref[...], sint_ref[...]
        lane = lax.broadcasted_iota(jnp.int32, cos.shape, 1)
        first_half = (lane % (2 * half)) < half
    for h0 in range(0, d_attn, HEAD_PAIR):
        c = p[:, d_pool + h0: d_pool + h0 + HEAD_PAIR]
        ct = pt[h0:h0 + HEAD_PAIR]
        if rope:
            partner = jnp.where(first_half, pltpu.roll(c, HEAD_PAIR - half, 1), pltpu.roll(c, half, 1))
            c = c * cos + partner * sin
            swapped = []
            for r0 in range(0, HEAD_PAIR, 2 * half):
                swapped += [ct[r0 + half:r0 + 2 * half], ct[r0:r0 + half]]
            ct = ct * cost + jnp.concatenate(swapped, axis=0) * sint
        k_ref[0, :, h0:h0 + HEAD_PAIR] = c.astype(BF16)
        qt_ref[0, h0:h0 + HEAD_PAIR, :] = (ct * q_scale).astype(BF16)


def _inproj(x, shift, scale, g, w_uk, w_qvt, tables):
    b, l, d = x.shape
    d_pool = d // 2
    d_attn = w_qvt.shape[0] // 2
    t = min(512, l)
    rope = tables is not None
    row = lambda bi, i: (bi, i, 0)
    col = lambda bi, i: (bi, 0, i)
    per_batch = lambda bi, i: (bi, 0, 0)
    const = lambda bi, i: (0, 0)
    in_specs = [
        pl.BlockSpec((1, t, d), row),
        pl.BlockSpec((1, 1, d), per_batch),
        pl.BlockSpec((1, 1, d), per_batch),
        pl.BlockSpec((1, d), const),
        pl.BlockSpec(w_uk.shape, const),
        pl.BlockSpec(w_qvt.shape, const),
    ]
    args = [x, shift, scale, g.reshape(1, d), w_uk, w_qvt]
    if rope:
        in_specs += [pl.BlockSpec((t, HEAD_PAIR), lambda bi, i: (i, 0))] * 2
        in_specs += [pl.BlockSpec((HEAD_PAIR, t), lambda bi, i: (0, i))] * 2
        args += list(tables)
    return pl.pallas_call(
        functools.partial(_inproj_kernel, rope=rope, d_pool=d_pool, d_attn=d_attn,
                          q_scale=HEAD_DIM ** -0.5 * math.log2(math.e)),
        grid=(b, l // t),
        in_specs=in_specs,
        out_specs=[pl.BlockSpec((1, t, d_pool), row), pl.BlockSpec((1, d_attn, t), col),
                   pl.BlockSpec((1, t, d_attn), row), pl.BlockSpec((1, d_attn, t), col)],
        out_shape=[jax.ShapeDtypeStruct((b, l, d_pool), F32), jax.ShapeDtypeStruct((b, d_attn, l), BF16),
                   jax.ShapeDtypeStruct((b, l, d_attn), BF16), jax.ShapeDtypeStruct((b, d_attn, l), BF16)],
        compiler_params=_params(("arbitrary", "arbitrary")),
        name="inproj",
    )(*args)


def _attn_kernel(*refs, tq, tk, cb, n_chunks, group, has_ctx, lam_init):
    if has_ctx:
        (lam_ref, qt_ref, k_ref, vt_ref, kc_ref, vct_ref, g_ref, o_ref,
         qs_scr, m_scr, l_scr, acc_scr, s_scr) = refs
    else:
        lam_ref, qt_ref, k_ref, vt_ref, g_ref, o_ref, qs_scr, m_scr, l_scr, acc_scr, s_scr = refs
    qt = qt_ref[0]
    sub = lax.broadcasted_iota(jnp.int32, qt.shape, 0)
    zero = jnp.zeros_like(qt)
    qs_scr[:, :tq] = jnp.where(sub < HEAD_DIM, qt, zero)
    qs_scr[:, tq:] = jnp.where(sub >= HEAD_DIM, qt, zero)
    m_scr[...] = jnp.full(m_scr.shape, -jnp.inf, F32)
    l_scr[...] = jnp.zeros(l_scr.shape, F32)
    acc_scr[...] = jnp.zeros(acc_scr.shape, F32)

    n_blocks = 2 * tq // cb
    blocks = [slice(c * cb, (c + 1) * cb) for c in range(n_blocks)]

    def scores(kj, c):
        return jnp.dot(kj, qs_scr[:, blocks[c]], preferred_element_type=F32)

    def with_ones(vtj):
        return jnp.concatenate([vtj, jnp.ones((ONES_ROWS, vtj.shape[1]), BF16)], axis=0)

    def span(j):
        return pl.ds(j * tk if isinstance(j, int) else pl.multiple_of(j * tk, tk), tk)

    def keys(j):
        return k_ref[0, span(j), :]

    def values_t(j):
        return vt_ref[0, :, span(j)]

    def online_chunk(kj, vtj):
        vta = with_ones(vtj)
        for c, cols in enumerate(blocks):
            s = scores(kj, c)
            m_old = m_scr[:, cols]
            m_new = jnp.maximum(m_old, jnp.max(s, axis=0, keepdims=True))
            alpha = jnp.exp2(m_old - m_new)
            p = jnp.exp2(s - m_new).astype(BF16)
            r = jnp.dot(vta, p, preferred_element_type=F32)
            l_scr[:, cols] = alpha * l_scr[:, cols] + r[HEAD_PAIR:HEAD_PAIR + 1]
            acc_scr[:, cols] = alpha * acc_scr[:, cols] + r[:HEAD_PAIR]
            m_scr[:, cols] = m_new

    def group_fast(j0, j_next):
        items = [(c, jj) for c in range(n_blocks) for jj in range(group)]
        s_next = s_scr[...]
        sums, excess = [], None
        for idx, (c, jj) in enumerate(items):
            s = s_next
            if idx + 1 < len(items):
                c2, jj2 = items[idx + 1]
                s_next = scores(keys(j0 + jj2), c2)
            elif j_next is not None:
                s_scr[...] = scores(keys(j_next), 0)
            m_ref = m_scr[:, blocks[c]]
            top = jnp.max(s, axis=0, keepdims=True)
            p = jnp.exp2(s - m_ref).astype(BF16)
            r = jnp.dot(with_ones(values_t(j0 + jj)), p, preferred_element_type=F32)
            r_c, top_c = (r, top) if jj == 0 else (r_c + r, jnp.maximum(top_c, top))
            if jj == group - 1:
                sums.append(r_c)
                over = jnp.max(top_c - m_ref)
                excess = over if excess is None else jnp.maximum(excess, over)
        ok = excess <= MAX_EXCESS

        @pl.when(ok)
        def _():
            for cols, r_c in zip(blocks, sums):
                l_scr[:, cols] += r_c[HEAD_PAIR:HEAD_PAIR + 1]
                acc_scr[:, cols] += r_c[:HEAD_PAIR]

        return ok

    def group_step(j0, j_next):
        ok = group_fast(j0, j_next)

        @pl.when(jnp.logical_not(ok))
        def _():
            def redo(jj, carry):
                online_chunk(keys(j0 + jj), values_t(j0 + jj))
                return carry

            lax.fori_loop(0, group, redo, 0)

    n_groups = n_chunks // group
    if has_ctx:
        online_chunk(kc_ref[0], vct_ref[0])
    s_scr[...] = scores(keys(0), 0)
    if n_groups > 1:
        def body(g, carry):
            group_step(g * group, (g + 1) * group)
            return carry

        lax.fori_loop(0, n_groups - 1, body, 0)
    group_step((n_groups - 1) * group, None)

    o = acc_scr[...] / l_scr[...]
    lv = lam_ref[...]
    lam = (jnp.exp(jnp.sum(lv[0:1] * lv[1:2], axis=1, keepdims=True))
           - jnp.exp(jnp.sum(lv[2:3] * lv[3:4], axis=1, keepdims=True)) + lam_init)
    a = (o[:, :tq] - lam * o[:, tq:]).T
    o_ref[0] = (_rmsnorm(a, g_ref[...]) * (1.0 - lam_init)).astype(BF16)


def _attention(qt, k, vt, kc, vct, lam4, subln_g, lam_init):
    b, d_attn, lq = qt.shape
    lk = k.shape[1]
    n_heads = d_attn // HEAD_PAIR
    tq = min(1024, lq)
    tk = min(512, lk)
    cb = min(512, 2 * tq)
    has_ctx = kc is not None
    qmap = lambda bi, h, i: (bi, h, i)
    kmap = lambda bi, h, i: (bi, 0, h)
    vmap = lambda bi, h, i: (bi, h, 0)
    const = lambda bi, h, i: (0, 0)
    in_specs = [
        pl.BlockSpec(lam4.shape, const),
        pl.BlockSpec((1, HEAD_PAIR, tq), qmap),
        pl.BlockSpec((1, lk, HEAD_PAIR), kmap),
        pl.BlockSpec((1, HEAD_PAIR, lk), vmap),
    ]
    args = [lam4, qt, k, vt]
    if has_ctx:
        lc = kc.shape[1]
        in_specs += [pl.BlockSpec((1, lc, HEAD_PAIR), kmap), pl.BlockSpec((1, HEAD_PAIR, lc), vmap)]
        args += [kc, vct]
    in_specs.append(pl.BlockSpec((1, HEAD_PAIR), const))
    args.append(subln_g.reshape(1, HEAD_PAIR))
    return pl.pallas_call(
        functools.partial(_attn_kernel, tq=tq, tk=tk, cb=cb, n_chunks=lk // tk,
                          group=math.gcd(KEY_GROUP, lk // tk), has_ctx=has_ctx,
                          lam_init=lam_init),
        grid=(b, n_heads, lq // tq),
        in_specs=in_specs,
        out_specs=pl.BlockSpec((1, tq, HEAD_PAIR), lambda bi, h, i: (bi, i, h)),
        out_shape=jax.ShapeDtypeStruct((b, lq, d_attn), BF16),
        scratch_shapes=[
            pltpu.VMEM((HEAD_PAIR, 2 * tq), BF16),
            pltpu.VMEM((1, 2 * tq), F32),
            pltpu.VMEM((1, 2 * tq), F32),
            pltpu.VMEM((HEAD_PAIR, 2 * tq), F32),
            pltpu.VMEM((tk, cb), F32),
        ],
        compiler_params=_params(("arbitrary", "arbitrary", "arbitrary")),
        name="diff_attn",
    )(*args)


def _route(logits):
    lane = lax.broadcasted_iota(jnp.int32, logits.shape, 1)
    big = jnp.int32(ROUTER_LANES)
    neg = jnp.float32(-jnp.inf)
    lc = jnp.where(lane < N_GROUPS, logits, neg)
    mc = jnp.max(lc, axis=1, keepdims=True)
    p_group = 1.0 / jnp.sum(jnp.exp(lc - mc), axis=1, keepdims=True)
    g_idx = jnp.min(jnp.where(lc == mc, lane, big), axis=1, keepdims=True)
    lo = N_GROUPS + EXPERTS_PER_GROUP * g_idx
    lf = jnp.where((lane >= lo) & (lane < lo + EXPERTS_PER_GROUP), logits, neg)
    v1 = jnp.max(lf, axis=1, keepdims=True)
    i1 = jnp.min(jnp.where(lf == v1, lane, big), axis=1, keepdims=True)
    lf2 = jnp.where(lane == i1, neg, lf)
    v2 = jnp.max(lf2, axis=1, keepdims=True)
    i2 = jnp.min(jnp.where(lf2 == v2, lane, big), axis=1, keepdims=True)
    e2 = jnp.exp(v2 - v1)
    w1 = p_group / (1.0 + e2)
    w2 = p_group * e2 / (1.0 + e2)
    return lane, i1, i2, w1, w2


def _split_bf16(x):
    hi = x.astype(BF16)
    return hi, (x - hi.astype(F32)).astype(BF16)


def _sorted_positions(lane, i1, i2):
    t = lane.shape[0]
    oh1 = lane == i1
    oh2 = lane == i2
    oh = jnp.concatenate([oh1, oh2], axis=1).astype(BF16)
    r = lax.broadcasted_iota(jnp.int32, (t, t), 0)
    c = lax.broadcasted_iota(jnp.int32, (t, t), 1)
    before = (c < r).astype(BF16)
    rank = jnp.dot(before, oh, preferred_element_type=F32)
    cnt1 = jnp.sum(oh1.astype(F32), axis=0, keepdims=True)
    cnt2 = jnp.sum(oh2.astype(F32), axis=0, keepdims=True)
    run = jnp.ceil((cnt1 + cnt2) * (1.0 / SLOT_ALIGN)) * SLOT_ALIGN
    lr = lax.broadcasted_iota(jnp.int32, (ROUTER_LANES, ROUTER_LANES), 0)
    lc = lax.broadcasted_iota(jnp.int32, (ROUTER_LANES, ROUTER_LANES), 1)
    start = jnp.dot(jnp.broadcast_to(run, (8, ROUTER_LANES)).astype(BF16), (lr < lc).astype(BF16),
                    preferred_element_type=F32)[0:1]
    pos1 = jnp.sum(jnp.where(oh1, start + rank[:, :ROUTER_LANES], 0.0), axis=1, keepdims=True)
    pos2 = jnp.sum(jnp.where(oh2, start + cnt1 + rank[:, ROUTER_LANES:], 0.0), axis=1, keepdims=True)
    return pos1, pos2, run


def _merge_kernel(u_ref, up_ref, un_ref, a_ref, x_ref, pw_ref, ps_ref, wo_ref, g1_ref, n2_ref, sh_ref,
                  sc_ref, wr_ref, rb_ref, xo_ref, xa_ref, tm_ref, lm_ref, run_ref, ext_scr, *, t, seq_len,
                  d_pool):
    i = pl.program_id(1)
    n_tiles = pl.num_programs(1)
    u = u_ref[0]
    ext_scr[0:HALO] = jnp.where(i > 0, up_ref[0], 0.0)
    ext_scr[HALO:HALO + t] = u
    ext_scr[HALO + t:] = jnp.where(i < n_tiles - 1, un_ref[0], 0.0)

    pos = i * t + lax.broadcasted_iota(jnp.int32, (t, 1), 0)
    gc = d_pool // len(POOL_WINDOWS)
    mix = jnp.zeros((t, wo_ref.shape[1]), F32)
    for gi, w in enumerate(POOL_WINDOWS):
        left = w // 2
        right = w - 1 - left
        cols = slice(gi * gc, (gi + 1) * gc)
        s = ext_scr[pl.ds(HALO - left, t), cols]
        for o in range(-left + 1, right + 1):
            s = s + ext_scr[pl.ds(HALO + o, t), cols]
        cnt = (jnp.minimum(pos + right + 1, seq_len) - jnp.maximum(pos - left, 0)).astype(F32)
        pooled = s / cnt - u[:, cols]
        mixed = jnp.dot(pooled.astype(BF16), pw_ref[gi], preferred_element_type=F32) * ps_ref[:, cols]
        mix = mix + jnp.dot(mixed.astype(BF16), wo_ref[cols, :], preferred_element_type=F32)
    mix = mix + jnp.dot(a_ref[0], wo_ref[d_pool:, :], preferred_element_type=F32)

    x_new = x_ref[0] + g1_ref[0] * mix
    xo_ref[0] = x_new
    h2 = _rmsnorm(x_new, n2_ref[...]) * (1.0 + sc_ref[0]) + sh_ref[0]
    h_hi, h_lo = _split_bf16(h2)
    d = h2.shape[1]
    logits = (jnp.dot(h_hi, wr_ref[0], preferred_element_type=F32)
              + jnp.dot(h_lo, wr_ref[0], preferred_element_type=F32)
              + jnp.dot(h_hi, wr_ref[1], preferred_element_type=F32)) + rb_ref[...]
    lane, i1, i2, w1, w2 = _route(logits)
    pos1, pos2, run = _sorted_positions(lane, i1, i2)
    w1_hi = w1.astype(BF16).astype(F32)
    w2_hi = w2.astype(BF16).astype(F32)
    aux = jnp.zeros(lane.shape, F32)
    for k, val in enumerate((w1_hi, w1 - w1_hi, w2_hi, w2 - w2_hi, (i1 - N_GROUPS).astype(F32))):
        aux = jnp.where(lane == k, val, aux)
    xa_ref[0, :, :d] = h_hi
    xa_ref[0, :, d:] = aux.astype(BF16)
    info = jnp.where(lane == 0, pos1, jnp.where(lane == 1, pos2, 0.0))
    tm_ref[0] = info
    lm_ref[0] = info.T[:8]
    run_ref[0, 0] = run


def _merge(u, attn, x, pool_w_bf, pool_scale, w_out_bf, g1, norm2_g, sh2, sc2, wr, rb):
    b, l, d = x.shape
    d_pool = u.shape[2]
    t = min(MOE_TOKEN_TILE, l)
    hb = t // HALO
    n_halo = l // HALO
    row = lambda bi, i: (bi, i, 0)
    per_batch = lambda bi, i: (bi, 0, 0)
    const2 = lambda bi, i: (0, 0)
    const3 = lambda bi, i: (0, 0, 0)
    return pl.pallas_call(
        functools.partial(_merge_kernel, t=t, seq_len=l, d_pool=d_pool),
        grid=(b, l // t),
        in_specs=[
            pl.BlockSpec((1, t, d_pool), row),
            pl.BlockSpec((1, HALO, d_pool), lambda bi, i: (bi, jnp.maximum(i * hb - 1, 0), 0)),
            pl.BlockSpec((1, HALO, d_pool), lambda bi, i: (bi, jnp.minimum((i + 1) * hb, n_halo - 1), 0)),
            pl.BlockSpec((1, t, attn.shape[2]), row),
            pl.BlockSpec((1, t, d), row),
            pl.BlockSpec(pool_w_bf.shape, const3),
            pl.BlockSpec((1, d_pool), const2),
            pl.BlockSpec(w_out_bf.shape, const2),
            pl.BlockSpec((1, 1, d), per_batch),
            pl.BlockSpec((1, d), const2),
            pl.BlockSpec((1, 1, d), per_batch),
            pl.BlockSpec((1, 1, d), per_batch),
            pl.BlockSpec(wr.shape, const3),
            pl.BlockSpec((1, ROUTER_LANES), const2),
        ],
        out_specs=[
            pl.BlockSpec((1, t, d), row),
            pl.BlockSpec((1, t, d + ROUTER_LANES), row),
            pl.BlockSpec((1, t, ROUTER_LANES), row),
            pl.BlockSpec((1, 8, t), lambda bi, i: (bi, 0, i)),
            pl.BlockSpec((1, 1, 1, ROUTER_LANES), lambda bi, i: (bi, i, 0, 0)),
        ],
        out_shape=[
            jax.ShapeDtypeStruct((b, l, d), F32),
            jax.ShapeDtypeStruct((b, l, d + ROUTER_LANES), BF16),
            jax.ShapeDtypeStruct((b, l, ROUTER_LANES), F32),
            jax.ShapeDtypeStruct((b, 8, l), F32),
            jax.ShapeDtypeStruct((b, l // t, 1, ROUTER_LANES), F32),
        ],
        scratch_shapes=[pltpu.VMEM((t + 2 * HALO, d_pool), F32)],
        compiler_params=_params(("arbitrary", "arbitrary")),
        name="merge",
    )(u, u, u, attn, x, pool_w_bf, pool_scale.reshape(1, d_pool), w_out_bf, g1, norm2_g.reshape(1, d),
      sh2, sc2, wr, rb)


def _moe_plan(runs, t, tm):
    counts = runs[:, N_GROUPS:N_GROUPS + N_EXPERTS].astype(jnp.int32)
    n_tiles = counts.shape[0]
    total = jnp.sum(counts, axis=0)
    region = (total + tm - 1) // tm * tm
    region_end = jnp.cumsum(region)
    start = region_end - region
    dst = start[None, :] + jnp.cumsum(counts, axis=0) - counts
    off = jnp.cumsum(counts, axis=1) - counts
    s_max = 2 * n_tiles * t + n_tiles * N_EXPERTS * SLOT_ALIGN + N_EXPERTS * tm
    n_row_tiles = -(-s_max // tm)
    first_row = jnp.arange(n_row_tiles, dtype=jnp.int32) * tm
    valid = first_row < region_end[-1]
    last_valid = jnp.maximum(region_end[-1] // tm - 1, 0)
    block = jnp.where(valid, jnp.arange(n_row_tiles, dtype=jnp.int32), last_valid)
    expert = jnp.sum((region_end[None, :] <= (block * tm)[:, None]).astype(jnp.int32), axis=1)
    expert = jnp.minimum(expert, N_EXPERTS - 1)
    plan = dict(dst=dst.reshape(-1), off=off.reshape(-1), n=(counts // SLOT_ALIGN).reshape(-1),
                slack_dst=start + total, slack_n=(region - total) // SLOT_ALIGN,
                expert=expert, block=block, valid=valid.astype(jnp.int32))
    return plan, n_row_tiles


def _start_runs(n_ref, tile, src_first, dst_first, src_rows, dst_rows, sem):
    for e in range(N_EXPERTS):
        n = n_ref[tile * N_EXPERTS + e]
        s0, d0 = src_first(e), dst_first(e)

        def issue(j, carry, s0=s0, d0=d0):
            step = j * SLOT_ALIGN
            pltpu.make_async_copy(src_rows(s0 + step), dst_rows(d0 + step), sem).start()
            return carry

        lax.fori_loop(0, n, issue, 0)


def _wait_runs(n_ref, tile, src, dst, sem):
    total = 0
    for e in range(N_EXPERTS):
        total = total + n_ref[tile * N_EXPERTS + e]

    def drain(j, carry):
        pltpu.make_async_copy(src, dst, sem).wait()
        return carry

    lax.fori_loop(0, total, drain, 0)


def _span(first):
    return pl.ds(pl.multiple_of(first, SLOT_ALIGN), SLOT_ALIGN)


def _dispatch_kernel(dst_ref, off_ref, n_ref, sdst_ref, sn_ref, xa_ref, lm_ref, xs_hbm, buf, zbuf, sems, *, p):
    i = pl.program_id(0)
    last = pl.num_programs(0) - 1
    slot = i % 2
    t = xa_ref.shape[0]
    one_src, one_dst = buf.at[0, pl.ds(0, SLOT_ALIGN)], xs_hbm.at[pl.ds(0, SLOT_ALIGN)]

    @pl.when(i >= 2)
    def _():
        _wait_runs(n_ref, i - 2, one_src, one_dst, sems.at[slot])

    pos1 = lm_ref[0, 0:1, :].astype(jnp.int32)
    pos2 = lm_ref[0, 1:2, :].astype(jnp.int32)
    xa = xa_ref[...]
    for r0 in range(0, p, PERM_CHUNK):
        row = r0 + lax.broadcasted_iota(jnp.int32, (PERM_CHUNK, t), 0)
        perm = ((row == pos1) | (row == pos2)).astype(BF16)
        buf[slot, r0:r0 + PERM_CHUNK] = jnp.dot(perm, xa, preferred_element_type=F32).astype(BF16)
    _start_runs(n_ref, i, lambda e: off_ref[i * N_EXPERTS + e], lambda e: dst_ref[i * N_EXPERTS + e],
                lambda first: buf.at[slot, _span(first)], lambda first: xs_hbm.at[_span(first)], sems.at[slot])

    @pl.when(i == last)
    def _():
        @pl.when(i >= 1)
        def _():
            _wait_runs(n_ref, i - 1, one_src, one_dst, sems.at[1 - slot])

        _wait_runs(n_ref, i, one_src, one_dst, sems.at[slot])
        zbuf[...] = jnp.zeros(zbuf.shape, BF16)
        _start_runs(sn_ref, 0, lambda e: 0, lambda e: sdst_ref[e],
                    lambda first: zbuf, lambda first: xs_hbm.at[_span(first)], sems.at[slot])
        _wait_runs(sn_ref, 0, one_src, one_dst, sems.at[slot])


def _expert_kernel(e_ref, blk_ref, valid_ref, xs_ref, wg_ref, wu_ref, wd_ref, ys_ref, *, d):
    r = pl.program_id(0)

    @pl.when(valid_ref[r] == 1)
    def _():
        x = xs_ref[:, :d]
        aux = xs_ref[:, d:].astype(F32)
        is_first = aux[:, 4:5] == e_ref[r].astype(F32)
        gate = jnp.where(is_first, aux[:, 0:1] + aux[:, 1:2], aux[:, 2:3] + aux[:, 3:4])
        a = jnp.dot(x, wg_ref[0, 0], preferred_element_type=F32)
        bu = jnp.dot(x, wu_ref[0, 0], preferred_element_type=F32)
        hid = (a * jax.nn.sigmoid(a)) * bu * gate
        ys_ref[...] = jnp.dot(hid.astype(BF16), wd_ref[0, 0], preferred_element_type=F32).astype(BF16)


def _combine_kernel(*refs, final_norm):
    if final_norm:
        dst_ref, off_ref, n_ref, ys_hbm, tm_ref, x_ref, g2_ref, fg_ref, o_ref, buf, sems = refs
    else:
        dst_ref, off_ref, n_ref, ys_hbm, tm_ref, x_ref, g2_ref, o_ref, buf, sems = refs
    i = pl.program_id(0)
    slot = i % 2
    t = x_ref.shape[0]
    p = buf.shape[1]

    def fetch(tile, s):
        buf[s] = jnp.zeros(buf.shape[1:], BF16)
        _start_runs(n_ref, tile, lambda e: dst_ref[tile * N_EXPERTS + e], lambda e: off_ref[tile * N_EXPERTS + e],
                    lambda first: ys_hbm.at[_span(first)], lambda first: buf.at[s, _span(first)], sems.at[s])

    @pl.when(i == 0)
    def _():
        fetch(i, slot)

    @pl.when(i + 1 < pl.num_programs(0))
    def _():
        fetch(i + 1, 1 - slot)

    _wait_runs(n_ref, i, ys_hbm.at[pl.ds(0, SLOT_ALIGN)], buf.at[0, pl.ds(0, SLOT_ALIGN)], sems.at[slot])
    pos1 = tm_ref[:, 0:1].astype(jnp.int32)
    pos2 = tm_ref[:, 1:2].astype(jnp.int32)
    f = None
    for r0 in range(0, p, PERM_CHUNK):
        col = r0 + lax.broadcasted_iota(jnp.int32, (t, PERM_CHUNK), 1)
        perm_t = ((col == pos1) | (col == pos2)).astype(BF16)
        part = jnp.dot(perm_t, buf[slot, r0:r0 + PERM_CHUNK], preferred_element_type=F32)
        f = part if f is None else f + part
    y = x_ref[...] + g2_ref[0] * f
    if final_norm:
        y = _rmsnorm(y, fg_ref[...])
    o_ref[...] = y


def _moe(x, xa, info_tm, info_lm, runs, g2, wg_bf, wu_bf, wd_bf, layer, final_g):
    b, l, d = x.shape
    da = xa.shape[2]
    f = wg_bf.shape[3]
    t = min(MOE_TOKEN_TILE, l)
    tiles_per_batch = l // t
    n_tiles = b * tiles_per_batch
    p = 2 * t + N_EXPERTS * SLOT_ALIGN
    tm = MOE_ROW_TILE
    plan, n_row_tiles = _moe_plan(runs.reshape(n_tiles, ROUTER_LANES), t, tm)
    s_rows = n_row_tiles * tm
    n = b * l

    xs = pl.pallas_call(
        functools.partial(_dispatch_kernel, p=p),
        grid_spec=pltpu.PrefetchScalarGridSpec(
            num_scalar_prefetch=5,
            grid=(n_tiles,),
            in_specs=[
                pl.BlockSpec((t, da), lambda i, *_: (i, 0)),
                pl.BlockSpec((1, 8, t), lambda i, *_: (i // tiles_per_batch, 0, i % tiles_per_batch)),
            ],
            out_specs=pl.BlockSpec(memory_space=pl.ANY),
            scratch_shapes=[pltpu.VMEM((2, p, da), BF16), pltpu.VMEM((SLOT_ALIGN, da), BF16),
                            pltpu.SemaphoreType.DMA((2,))],
        ),
        out_shape=jax.ShapeDtypeStruct((s_rows, da), BF16),
        compiler_params=_params(("arbitrary",)),
        name="moe_dispatch",
    )(plan["dst"], plan["off"], plan["n"], plan["slack_dst"], plan["slack_n"], xa.reshape(n, da), info_lm)

    w_map = lambda r, e_ref, blk_ref, valid_ref: (layer, e_ref[r], 0, 0)
    row_map = lambda r, e_ref, blk_ref, valid_ref: (blk_ref[r], 0)
    ys = pl.pallas_call(
        functools.partial(_expert_kernel, d=d),
        grid_spec=pltpu.PrefetchScalarGridSpec(
            num_scalar_prefetch=3,
            grid=(n_row_tiles,),
            in_specs=[
                pl.BlockSpec((tm, da), row_map),
                pl.BlockSpec((1, 1, d, f), w_map),
                pl.BlockSpec((1, 1, d, f), w_map),
                pl.BlockSpec((1, 1, f, d), w_map),
            ],
            out_specs=pl.BlockSpec((tm, d), row_map),
        ),
        out_shape=jax.ShapeDtypeStruct((s_rows, d), BF16),
        compiler_params=_params(("arbitrary",)),
        name="moe_experts",
    )(plan["expert"], plan["block"], plan["valid"], xs, wg_bf, wu_bf, wd_bf)

    final_norm = final_g is not None
    in_specs = [
        pl.BlockSpec(memory_space=pl.ANY),
        pl.BlockSpec((t, ROUTER_LANES), lambda i, *_: (i, 0)),
        pl.BlockSpec((t, d), lambda i, *_: (i, 0)),
        pl.BlockSpec((1, 1, d), lambda i, *_: (i // tiles_per_batch, 0, 0)),
    ]
    args = [ys, info_tm.reshape(n, ROUTER_LANES), x.reshape(n, d), g2]
    if final_norm:
        in_specs.append(pl.BlockSpec((1, d), lambda i, *_: (0, 0)))
        args.append(final_g.reshape(1, d))
    out = pl.pallas_call(
        functools.partial(_combine_kernel, final_norm=final_norm),
        grid_spec=pltpu.PrefetchScalarGridSpec(
            num_scalar_prefetch=3,
            grid=(n_tiles,),
            in_specs=in_specs,
            out_specs=pl.BlockSpec((t, d), lambda i, *_: (i, 0)),
            scratch_shapes=[pltpu.VMEM((2, p, d), BF16), pltpu.SemaphoreType.DMA((2,))],
        ),
        out_shape=jax.ShapeDtypeStruct((n, d), F32),
        compiler_params=_params(("arbitrary",)),
        name="moe_combine",
    )(plan["dst"], plan["off"], plan["n"], *args)
    return out.reshape(b, l, d)


def kernel(x, c, ctx, c_ctx, ada_w, ada_b, norm1_g, norm2_g, w_in, pool_w, pool_scale, lambda_q1, lambda_k1,
           lambda_q2, lambda_k2, subln_g, w_out, router_coarse_w, router_coarse_b, router_fine_w,
           router_fine_b, w_gate, w_up, w_down, final_g):
    b, l, d = x.shape
    depth = ada_w.shape[0]
    assert b + 1 <= MOD_ROWS and l % GRID_W == 0

    tables = _rope_tables(l)
    cc = jnp.zeros((MOD_ROWS, d), F32).at[:b].set(c).at[b].set(c_ctx)
    d_pool = d // 2
    d_attn = (w_in.shape[-1] - d_pool) // 3
    w_uk = jnp.concatenate([w_in[..., :d_pool], w_in[..., d_pool + d_attn:d_pool + 2 * d_attn]],
                           axis=-1).astype(BF16)
    w_qvt = jnp.swapaxes(jnp.concatenate([w_in[..., d_pool:d_pool + d_attn], w_in[..., d_pool + 2 * d_attn:]],
                                         axis=-1), 1, 2).astype(BF16)
    pool_w_bf = pool_w.astype(BF16)
    w_out_bf = w_out.astype(BF16)
    n_e = w_gate.shape[1] * w_gate.shape[2]
    f = w_gate.shape[-1]
    wg_bf = w_gate.astype(BF16).reshape(depth, n_e, d, f)
    wu_bf = w_up.astype(BF16).reshape(depth, n_e, d, f)
    wd_bf = w_down.astype(BF16).reshape(depth, n_e, f, d)
    pad = ROUTER_LANES - N_GROUPS - N_EXPERTS
    wr = jnp.concatenate([router_coarse_w, router_fine_w, jnp.zeros((depth, d, pad), F32)], axis=-1)
    wr_hi = wr.astype(BF16)
    wr = jnp.stack([wr_hi, (wr - wr_hi.astype(F32)).astype(BF16)], axis=1)
    rb = jnp.concatenate([router_coarse_b, router_fine_b, jnp.zeros((depth, pad), F32)], axis=-1)

    xc = ctx
    for layer in range(depth):
        last = layer == depth - 1
        lam_init = 0.8 - 0.6 * math.exp(-0.3 * layer)
        mod = _ada(cc, ada_w, ada_b, layer)
        sh1, sc1, g1, sh2, sc2, g2 = jnp.split(mod[:b, None, :], 6, axis=-1)
        csh1, csc1, cg1, csh2, csc2, cg2 = jnp.split(
            jnp.broadcast_to(mod[b][None, None, :], (b, 1, mod.shape[1])), 6, axis=-1)
        lam4 = jnp.stack([lambda_q1[layer], lambda_k1[layer], lambda_q2[layer], lambda_k2[layer]])

        u, q, k, v = _inproj(x, sh1, sc1, norm1_g[layer], w_uk[layer], w_qvt[layer], tables)
        uc, qc, kc, vc = _inproj(xc, csh1, csc1, norm1_g[layer], w_uk[layer], w_qvt[layer], None)
        attn = _attention(q, k, v, kc, vc, lam4, subln_g[layer], lam_init)
        merge_w = (pool_w_bf[layer], pool_scale[layer], w_out_bf[layer])
        route_w = (wr[layer], rb[layer].reshape(1, ROUTER_LANES))
        x, *routed = _merge(u, attn, x, *merge_w, g1, norm2_g[layer], sh2, sc2, *route_w)
        if not last:
            attn_c = _attention(qc, kc, vc, None, None, lam4, subln_g[layer], lam_init)
            xc, *routed_c = _merge(uc, attn_c, xc, *merge_w, cg1, norm2_g[layer], csh2, csc2, *route_w)
            xc = _moe(xc, *routed_c, cg2, wg_bf, wu_bf, wd_bf, layer, None)
        x = _moe(x, *routed, g2, wg_bf, wu_bf, wd_bf, layer, final_g if last else None)
    return x
```

```python
import functools
import math

import jax
import jax.numpy as jnp
from jax import lax
from jax.experimental import pallas as pl
from jax.experimental.pallas import tpu as pltpu

F32 = jnp.float32
BF16 = jnp.bfloat16

GRID_W = 64
POOL_WINDOWS = (2, 4, 8, 16)
HEAD_DIM = 64
HEAD_PAIR = 2 * HEAD_DIM
ROPE_THETA = 10000.0
N_GROUPS = 4
EXPERTS_PER_GROUP = 8
N_EXPERTS = N_GROUPS * EXPERTS_PER_GROUP
EPS = 1e-6
HALO = 16
ROUTER_LANES = 128
MOD_ROWS = 8
ONES_ROWS = 16
MAX_EXCESS = 24.0
KEY_GROUP = 16
SLOT_ALIGN = 16
MOE_TOKEN_TILE = 512
MOE_ROW_TILE = 512
PERM_CHUNK = 256

VMEM_LIMIT = 48 * 1024 * 1024


def _params(sem):
    return pltpu.CompilerParams(dimension_semantics=sem, vmem_limit_bytes=VMEM_LIMIT)


def _ada_kernel(cc_ref, w_ref, b_ref, o_ref):
    s = cc_ref[...]
    s = s * jax.nn.sigmoid(s)
    o_ref[...] = jnp.dot(s, w_ref[0], precision=lax.Precision.HIGHEST,
                         preferred_element_type=F32) + b_ref[0]


def _ada(cc, ada_w, ada_b, layer):
    _, d, n = ada_w.shape
    bn = 1024
    return pl.pallas_call(
        _ada_kernel,
        grid=(n // bn,),
        in_specs=[
            pl.BlockSpec((MOD_ROWS, d), lambda j: (0, 0)),
            pl.BlockSpec((1, d, bn), lambda j: (layer, 0, j)),
            pl.BlockSpec((1, 1, bn), lambda j: (layer, 0, j)),
        ],
        out_specs=pl.BlockSpec((MOD_ROWS, bn), lambda j: (0, j)),
        out_shape=jax.ShapeDtypeStruct((MOD_ROWS, n), F32),
        compiler_params=_params(("arbitrary",)),
        name="ada",
    )(cc, ada_w, ada_b.reshape(ada_b.shape[0], 1, n))


def _rope_tables(n):
    pos = jnp.arange(n, dtype=jnp.int32)
    row = (pos // GRID_W).astype(F32)
    col = (pos % GRID_W).astype(F32)
    n_freq = HEAD_DIM // 4
    inv = ROPE_THETA ** (-jnp.arange(n_freq, dtype=F32) / n_freq)
    ang_r = row[:, None] * inv
    ang_c = col[:, None] * inv
    cos = jnp.concatenate([jnp.cos(ang_r), jnp.cos(ang_r), jnp.cos(ang_c), jnp.cos(ang_c)], axis=-1)
    sin = jnp.concatenate([-jnp.sin(ang_r), jnp.sin(ang_r), -jnp.sin(ang_c), jnp.sin(ang_c)], axis=-1)
    cos, sin = jnp.tile(cos, (1, 2)), jnp.tile(sin, (1, 2))
    return cos, sin, cos.T, sin.T


def _rmsnorm(x, g):
    ms = jnp.mean(x * x, axis=-1, keepdims=True)
    return x * lax.rsqrt(ms + EPS) * g


def _inproj_kernel(*refs, rope, d_pool, d_attn, q_scale):
    if rope:
        (x_ref, sh_ref, sc_ref, g_ref, wuk_ref, wqv_ref, cos_ref, sin_ref, cost_ref, sint_ref,
         u_ref, qt_ref, k_ref, vt_ref) = refs
    else:
        x_ref, sh_ref, sc_ref, g_ref, wuk_ref, wqv_ref, u_ref, qt_ref, k_ref, vt_ref = refs
    h = (_rmsnorm(x_ref[0], g_ref[...]) * (1.0 + sc_ref[0]) + sh_ref[0]).astype(BF16)
    p = jnp.dot(h, wuk_ref[...], preferred_element_type=F32)
    pt = lax.dot_general(wqv_ref[...], h, (((1,), (1,)), ((), ())), preferred_element_type=F32)
    u_ref[0] = p[:, :d_pool]
    vt_ref[0] = pt[d_attn:].astype(BF16)
    half = HEAD_DIM // 4
    if rope:
        cos, sin = cos_ref[...], sin_ref[...]
        cost, sint = cost_ref[...], sint_ref[...]
        lane = lax.broadcasted_iota(jnp.int32, cos.shape, 1)
        first_half = (lane % (2 * half)) < half
    for h0 in range(0, d_attn, HEAD_PAIR):
        c = p[:, d_pool + h0: d_pool + h0 + HEAD_PAIR]
        ct = pt[h0:h0 + HEAD_PAIR]
        if rope:
            partner = jnp.where(first_half, pltpu.roll(c, HEAD_PAIR - half, 1), pltpu.roll(c, half, 1))
            c = c * cos + partner * sin
            swapped = []
            for r0 in range(0, HEAD_PAIR, 2 * half):
                swapped += [ct[r0 + half:r0 + 2 * half], ct[r0:r0 + half]]
            ct = ct * cost + jnp.concatenate(swapped, axis=0) * sint
        k_ref[0, :, h0:h0 + HEAD_PAIR] = c.astype(BF16)
        qt_ref[0, h0:h0 + HEAD_PAIR, :] = (ct * q_scale).astype(BF16)


def _inproj(x, shift, scale, g, w_uk, w_qvt, tables):
    b, l, d = x.shape
    d_pool = d // 2
    d_attn = w_qvt.shape[0] // 2
    t = min(512, l)
    rope = tables is not None
    row = lambda bi, i: (bi, i, 0)
    col = lambda bi, i: (bi, 0, i)
    per_batch = lambda bi, i: (bi, 0, 0)
    const = lambda bi, i: (0, 0)
    in_specs = [
        pl.BlockSpec((1, t, d), row),
        pl.BlockSpec((1, 1, d), per_batch),
        pl.BlockSpec((1, 1, d), per_batch),
        pl.BlockSpec((1, d), const),
        pl.BlockSpec(w_uk.shape, const),
        pl.BlockSpec(w_qvt.shape, const),
    ]
    args = [x, shift, scale, g.reshape(1, d), w_uk, w_qvt]
    if rope:
        in_specs += [pl.BlockSpec((t, HEAD_PAIR), lambda bi, i: (i, 0))] * 2
        in_specs += [pl.BlockSpec((HEAD_PAIR, t), lambda bi, i: (0, i))] * 2
        args += list(tables)
    return pl.pallas_call(
        functools.partial(_inproj_kernel, rope=rope, d_pool=d_pool, d_attn=d_attn,
                          q_scale=HEAD_DIM ** -0.5 * math.log2(math.e)),
        grid=(b, l // t),
        in_specs=in_specs,
        out_specs=[pl.BlockSpec((1, t, d_pool), row), pl.BlockSpec((1, d_attn, t), col),
                   pl.BlockSpec((1, t, d_attn), row), pl.BlockSpec((1, d_attn, t), col)],
        out_shape=[jax.ShapeDtypeStruct((b, l, d_pool), F32), jax.ShapeDtypeStruct((b, d_attn, l), BF16),
                   jax.ShapeDtypeStruct((b, l, d_attn), BF16), jax.ShapeDtypeStruct((b, d_attn, l), BF16)],
        compiler_params=_params(("arbitrary", "arbitrary")),
        name="inproj",
    )(*args)


def _attn_kernel(*refs, tq, tk, cb, n_chunks, group, has_ctx, lam_init):
    if has_ctx:
        (lam_ref, qt_ref, k_ref, vt_ref, kc_ref, vct_ref, g_ref, o_ref,
         qs_scr, m_scr, l_scr, acc_scr, s_scr) = refs
    else:
        lam_ref, qt_ref, k_ref, vt_ref, g_ref, o_ref, qs_scr, m_scr, l_scr, acc_scr, s_scr = refs
    qt = qt_ref[0]
    sub = lax.broadcasted_iota(jnp.int32, qt.shape, 0)
    zero = jnp.zeros_like(qt)
    qs_scr[:, :tq] = jnp.where(sub < HEAD_DIM, qt, zero)
    qs_scr[:, tq:] = jnp.where(sub >= HEAD_DIM, qt, zero)
    m_scr[...] = jnp.full(m_scr.shape, -jnp.inf, F32)
    l_scr[...] = jnp.zeros(l_scr.shape, F32)
    acc_scr[...] = jnp.zeros(acc_scr.shape, F32)

    n_blocks = 2 * tq // cb
    blocks = [slice(c * cb, (c + 1) * cb) for c in range(n_blocks)]

    def scores(kj, c):
        return jnp.dot(kj, qs_scr[:, blocks[c]], preferred_element_type=F32)

    def with_ones(vtj):
        return jnp.concatenate([vtj, jnp.ones((ONES_ROWS, vtj.shape[1]), BF16)], axis=0)

    def span(j):
        return pl.ds(j * tk if isinstance(j, int) else pl.multiple_of(j * tk, tk), tk)

    def keys(j):
        return k_ref[0, span(j), :]

    def values_t(j):
        return vt_ref[0, :, span(j)]

    def online_chunk(kj, vtj):
        vta = with_ones(vtj)
        for c, cols in enumerate(blocks):
            s = scores(kj, c)
            m_old = m_scr[:, cols]
            m_new = jnp.maximum(m_old, jnp.max(s, axis=0, keepdims=True))
            alpha = jnp.exp2(m_old - m_new)
            p = jnp.exp2(s - m_new).astype(BF16)
            r = jnp.dot(vta, p, preferred_element_type=F32)
            l_scr[:, cols] = alpha * l_scr[:, cols] + r[HEAD_PAIR:HEAD_PAIR + 1]
            acc_scr[:, cols] = alpha * acc_scr[:, cols] + r[:HEAD_PAIR]
            m_scr[:, cols] = m_new

    def group_fast(j0, j_next):
        items = [(c, jj) for c in range(n_blocks) for jj in range(group)]
        s_next = s_scr[...]
        sums, excess = [], None
        for idx, (c, jj) in enumerate(items):
            s = s_next
            if idx + 1 < len(items):
                c2, jj2 = items[idx + 1]
                s_next = scores(keys(j0 + jj2), c2)
            elif j_next is not None:
                s_scr[...] = scores(keys(j_next), 0)
            m_ref = m_scr[:, blocks[c]]
            top = jnp.max(s, axis=0, keepdims=True)
            p = jnp.exp2(s - m_ref).astype(BF16)
            r = jnp.dot(with_ones(values_t(j0 + jj)), p, preferred_element_type=F32)
            r_c, top_c = (r, top) if jj == 0 else (r_c + r, jnp.maximum(top_c, top))
            if jj == group - 1:
                sums.append(r_c)
                over = jnp.max(top_c - m_ref)
                excess = over if excess is None else jnp.maximum(excess, over)
        ok = excess <= MAX_EXCESS

        @pl.when(ok)
        def _():
            for cols, r_c in zip(blocks, sums):
                l_scr[:, cols] += r_c[HEAD_PAIR:HEAD_PAIR + 1]
                acc_scr[:, cols] += r_c[:HEAD_PAIR]

        return ok

    def group_step(j0, j_next):
        ok = group_fast(j0, j_next)

        @pl.when(jnp.logical_not(ok))
        def _():
            def redo(jj, carry):
                online_chunk(keys(j0 + jj), values_t(j0 + jj))
                return carry

            lax.fori_loop(0, group, redo, 0)

    n_groups = n_chunks // group
    if has_ctx:
        online_chunk(kc_ref[0], vct_ref[0])
    s_scr[...] = scores(keys(0), 0)
    if n_groups > 1:
        def body(g, carry):
            group_step(g * group, (g + 1) * group)
            return carry

        lax.fori_loop(0, n_groups - 1, body, 0)
    group_step((n_groups - 1) * group, None)

    o = acc_scr[...] / l_scr[...]
    lv = lam_ref[...]
    lam = (jnp.exp(jnp.sum(lv[0:1] * lv[1:2], axis=1, keepdims=True))
           - jnp.exp(jnp.sum(lv[2:3] * lv[3:4], axis=1, keepdims=True)) + lam_init)
    a = (o[:, :tq] - lam * o[:, tq:]).T
    o_ref[0] = (_rmsnorm(a, g_ref[...]) * (1.0 - lam_init)).astype(BF16)


def _attention(qt, k, vt, kc, vct, lam4, subln_g, lam_init):
    b, d_attn, lq = qt.shape
    lk = k.shape[1]
    n_heads = d_attn // HEAD_PAIR
    tq = min(1024, lq)
    tk = min(512, lk)
    cb = min(512, 2 * tq)
    has_ctx = kc is not None
    qmap = lambda bi, h, i: (bi, h, i)
    kmap = lambda bi, h, i: (bi, 0, h)
    vmap = lambda bi, h, i: (bi, h, 0)
    const = lambda bi, h, i: (0, 0)
    in_specs = [
        pl.BlockSpec(lam4.shape, const),
        pl.BlockSpec((1, HEAD_PAIR, tq), qmap),
        pl.BlockSpec((1, lk, HEAD_PAIR), kmap),
        pl.BlockSpec((1, HEAD_PAIR, lk), vmap),
    ]
    args = [lam4, qt, k, vt]
    if has_ctx:
        lc = kc.shape[1]
        in_specs += [pl.BlockSpec((1, lc, HEAD_PAIR), kmap), pl.BlockSpec((1, HEAD_PAIR, lc), vmap)]
        args += [kc, vct]
    in_specs.append(pl.BlockSpec((1, HEAD_PAIR), const))
    args.append(subln_g.reshape(1, HEAD_PAIR))
    return pl.pallas_call(
        functools.partial(_attn_kernel, tq=tq, tk=tk, cb=cb, n_chunks=lk // tk,
                          group=math.gcd(KEY_GROUP, lk // tk), has_ctx=has_ctx,
                          lam_init=lam_init),
        grid=(b, n_heads, lq // tq),
        in_specs=in_specs,
        out_specs=pl.BlockSpec((1, tq, HEAD_PAIR), lambda bi, h, i: (bi, i, h)),
        out_shape=jax.ShapeDtypeStruct((b, lq, d_attn), BF16),
        scratch_shapes=[
            pltpu.VMEM((HEAD_PAIR, 2 * tq), BF16),
            pltpu.VMEM((1, 2 * tq), F32),
            pltpu.VMEM((1, 2 * tq), F32),
            pltpu.VMEM((HEAD_PAIR, 2 * tq), F32),
            pltpu.VMEM((tk, cb), F32),
        ],
        compiler_params=_params(("arbitrary", "arbitrary", "arbitrary")),
        name="diff_attn",
    )(*args)


def _route(logits):
    lane = lax.broadcasted_iota(jnp.int32, logits.shape, 1)
    big = jnp.int32(ROUTER_LANES)
    neg = jnp.float32(-jnp.inf)
    lc = jnp.where(lane < N_GROUPS, logits, neg)
    mc = jnp.max(lc, axis=1, keepdims=True)
    p_group = 1.0 / jnp.sum(jnp.exp(lc - mc), axis=1, keepdims=True)
    g_idx = jnp.min(jnp.where(lc == mc, lane, big), axis=1, keepdims=True)
    lo = N_GROUPS + EXPERTS_PER_GROUP * g_idx
    lf = jnp.where((lane >= lo) & (lane < lo + EXPERTS_PER_GROUP), logits, neg)
    v1 = jnp.max(lf, axis=1, keepdims=True)
    i1 = jnp.min(jnp.where(lf == v1, lane, big), axis=1, keepdims=True)
    lf2 = jnp.where(lane == i1, neg, lf)
    v2 = jnp.max(lf2, axis=1, keepdims=True)
    i2 = jnp.min(jnp.where(lf2 == v2, lane, big), axis=1, keepdims=True)
    e2 = jnp.exp(v2 - v1)
    w1 = p_group / (1.0 + e2)
    w2 = p_group * e2 / (1.0 + e2)
    return lane, i1, i2, w1, w2


def _split_bf16(x):
    hi = x.astype(BF16)
    return hi, (x - hi.astype(F32)).astype(BF16)


def _sorted_positions(lane, i1, i2):
    t = lane.shape[0]
    oh1 = lane == i1
    oh2 = lane == i2
    oh = jnp.concatenate([oh1, oh2], axis=1).astype(BF16)
    r = lax.broadcasted_iota(jnp.int32, (t, t), 0)
    c = lax.broadcasted_iota(jnp.int32, (t, t), 1)
    before = (c < r).astype(BF16)
    rank = jnp.dot(before, oh, preferred_element_type=F32)
    cnt1 = jnp.sum(oh1.astype(F32), axis=0, keepdims=True)
    cnt2 = jnp.sum(oh2.astype(F32), axis=0, keepdims=True)
    run = jnp.ceil((cnt1 + cnt2) * (1.0 / SLOT_ALIGN)) * SLOT_ALIGN
    lr = lax.broadcasted_iota(jnp.int32, (ROUTER_LANES, ROUTER_LANES), 0)
    lc = lax.broadcasted_iota(jnp.int32, (ROUTER_LANES, ROUTER_LANES), 1)
    start = jnp.dot(jnp.broadcast_to(run, (8, ROUTER_LANES)).astype(BF16), (lr < lc).astype(BF16),
                    preferred_element_type=F32)[0:1]
    pos1 = jnp.sum(jnp.where(oh1, start + rank[:, :ROUTER_LANES], 0.0), axis=1, keepdims=True)
    pos2 = jnp.sum(jnp.where(oh2, start + cnt1 + rank[:, ROUTER_LANES:], 0.0), axis=1, keepdims=True)
    return pos1, pos2, run


def _merge_kernel(u_ref, up_ref, un_ref, a_ref, x_ref, pw_ref, ps_ref, wo_ref, g1_ref, n2_ref, sh_ref,
                  sc_ref, wr_ref, rb_ref, xo_ref, xa_ref, tm_ref, lm_ref, run_ref, ext_scr, *, t, seq_len,
                  d_pool):
    i = pl.program_id(1)
    n_tiles = pl.num_programs(1)
    u = u_ref[0]
    ext_scr[0:HALO] = jnp.where(i > 0, up_ref[0], 0.0)
    ext_scr[HALO:HALO + t] = u
    ext_scr[HALO + t:] = jnp.where(i < n_tiles - 1, un_ref[0], 0.0)

    pos = i * t + lax.broadcasted_iota(jnp.int32, (t, 1), 0)
    gc = d_pool // len(POOL_WINDOWS)
    mix = jnp.zeros((t, wo_ref.shape[1]), F32)
    for gi, w in enumerate(POOL_WINDOWS):
        left = w // 2
        right = w - 1 - left
        cols = slice(gi * gc, (gi + 1) * gc)
        s = ext_scr[pl.ds(HALO - left, t), cols]
        for o in range(-left + 1, right + 1):
            s = s + ext_scr[pl.ds(HALO + o, t), cols]
        cnt = (jnp.minimum(pos + right + 1, seq_len) - jnp.maximum(pos - left, 0)).astype(F32)
        pooled = s / cnt - u[:, cols]
        mixed = jnp.dot(pooled.astype(BF16), pw_ref[gi], preferred_element_type=F32) * ps_ref[:, cols]
        mix = mix + jnp.dot(mixed.astype(BF16), wo_ref[cols, :], preferred_element_type=F32)
    mix = mix + jnp.dot(a_ref[0], wo_ref[d_pool:, :], preferred_element_type=F32)

    x_new = x_ref[0] + g1_ref[0] * mix
    xo_ref[0] = x_new
    h2 = _rmsnorm(x_new, n2_ref[...]) * (1.0 + sc_ref[0]) + sh_ref[0]
    h_hi, h_lo = _split_bf16(h2)
    d = h2.shape[1]
    logits = (jnp.dot(h_hi, wr_ref[0], preferred_element_type=F32)
              + jnp.dot(h_lo, wr_ref[0], preferred_element_type=F32)
              + jnp.dot(h_hi, wr_ref[1], preferred_element_type=F32)) + rb_ref[...]
    lane, i1, i2, w1, w2 = _route(logits)
    pos1, pos2, run = _sorted_positions(lane, i1, i2)
    w1_hi = w1.astype(BF16).astype(F32)
    w2_hi = w2.astype(BF16).astype(F32)
    aux = jnp.zeros(lane.shape, F32)
    for k, val in enumerate((w1_hi, w1 - w1_hi, w2_hi, w2 - w2_hi, (i1 - N_GROUPS).astype(F32))):
        aux = jnp.where(lane == k, val, aux)
    xa_ref[0, :, :d] = h_hi
    xa_ref[0, :, d:] = aux.astype(BF16)
    info = jnp.where(lane == 0, pos1, jnp.where(lane == 1, pos2, 0.0))
    tm_ref[0] = info
    lm_ref[0] = info.T[:8]
    run_ref[0, 0] = run


def _merge(u, attn, x, pool_w_bf, pool_scale, w_out_bf, g1, norm2_g, sh2, sc2, wr, rb):
    b, l, d = x.shape
    d_pool = u.shape[2]
    t = min(MOE_TOKEN_TILE, l)
    hb = t // HALO
    n_halo = l // HALO
    row = lambda bi, i: (bi, i, 0)
    per_batch = lambda bi, i: (bi, 0, 0)
    const2 = lambda bi, i: (0, 0)
    const3 = lambda bi, i: (0, 0, 0)
    return pl.pallas_call(
        functools.partial(_merge_kernel, t=t, seq_len=l, d_pool=d_pool),
        grid=(b, l // t),
        in_specs=[
            pl.BlockSpec((1, t, d_pool), row),
            pl.BlockSpec((1, HALO, d_pool), lambda bi, i: (bi, jnp.maximum(i * hb - 1, 0), 0)),
            pl.BlockSpec((1, HALO, d_pool), lambda bi, i: (bi, jnp.minimum((i + 1) * hb, n_halo - 1), 0)),
            pl.BlockSpec((1, t, attn.shape[2]), row),
            pl.BlockSpec((1, t, d), row),
            pl.BlockSpec(pool_w_bf.shape, const3),
            pl.BlockSpec((1, d_pool), const2),
            pl.BlockSpec(w_out_bf.shape, const2),
            pl.BlockSpec((1, 1, d), per_batch),
            pl.BlockSpec((1, d), const2),
            pl.BlockSpec((1, 1, d), per_batch),
            pl.BlockSpec((1, 1, d), per_batch),
            pl.BlockSpec(wr.shape, const3),
            pl.BlockSpec((1, ROUTER_LANES), const2),
        ],
        out_specs=[
            pl.BlockSpec((1, t, d), row),
            pl.BlockSpec((1, t, d + ROUTER_LANES), row),
            pl.BlockSpec((1, t, ROUTER_LANES), row),
            pl.BlockSpec((1, 8, t), lambda bi, i: (bi, 0, i)),
            pl.BlockSpec((1, 1, 1, ROUTER_LANES), lambda bi, i: (bi, i, 0, 0)),
        ],
        out_shape=[
            jax.ShapeDtypeStruct((b, l, d), F32),
            jax.ShapeDtypeStruct((b, l, d + ROUTER_LANES), BF16),
            jax.ShapeDtypeStruct((b, l, ROUTER_LANES), F32),
            jax.ShapeDtypeStruct((b, 8, l), F32),
            jax.ShapeDtypeStruct((b, l // t, 1, ROUTER_LANES), F32),
        ],
        scratch_shapes=[pltpu.VMEM((t + 2 * HALO, d_pool), F32)],
        compiler_params=_params(("arbitrary", "arbitrary")),
        name="merge",
    )(u, u, u, attn, x, pool_w_bf, pool_scale.reshape(1, d_pool), w_out_bf, g1, norm2_g.reshape(1, d),
      sh2, sc2, wr, rb)


def _moe_plan(runs, t, tm):
    counts = runs[:, N_GROUPS:N_GROUPS + N_EXPERTS].astype(jnp.int32)
    n_tiles = counts.shape[0]
    total = jnp.sum(counts, axis=0)
    region = (total + tm - 1) // tm * tm
    region_end = jnp.cumsum(region)
    start = region_end - region
    dst = start[None, :] + jnp.cumsum(counts, axis=0) - counts
    off = jnp.cumsum(counts, axis=1) - counts
    s_max = 2 * n_tiles * t + n_tiles * N_EXPERTS * SLOT_ALIGN + N_EXPERTS * tm
    n_row_tiles = -(-s_max // tm)
    first_row = jnp.arange(n_row_tiles, dtype=jnp.int32) * tm
    valid = first_row < region_end[-1]
    last_valid = jnp.maximum(region_end[-1] // tm - 1, 0)
    block = jnp.where(valid, jnp.arange(n_row_tiles, dtype=jnp.int32), last_valid)
    expert = jnp.sum((region_end[None, :] <= (block * tm)[:, None]).astype(jnp.int32), axis=1)
    expert = jnp.minimum(expert, N_EXPERTS - 1)
    plan = dict(dst=dst.reshape(-1), off=off.reshape(-1), n=(counts // SLOT_ALIGN).reshape(-1),
                slack_dst=start + total, slack_n=(region - total) // SLOT_ALIGN,
                expert=expert, block=block, valid=valid.astype(jnp.int32))
    return plan, n_row_tiles


def _start_runs(n_ref, tile, src_first, dst_first, src_rows, dst_rows, sem):
    for e in range(N_EXPERTS):
        n = n_ref[tile * N_EXPERTS + e]
        s0, d0 = src_first(e), dst_first(e)

        def issue(j, carry, s0=s0, d0=d0):
            step = j * SLOT_ALIGN
            pltpu.make_async_copy(src_rows(s0 + step), dst_rows(d0 + step), sem).start()
            return carry

        lax.fori_loop(0, n, issue, 0)


def _wait_runs(n_ref, tile, src, dst, sem):
    total = 0
    for e in range(N_EXPERTS):
        total = total + n_ref[tile * N_EXPERTS + e]

    def drain(j, carry):
        pltpu.make_async_copy(src, dst, sem).wait()
        return carry

    lax.fori_loop(0, total, drain, 0)


def _span(first):
    return pl.ds(pl.multiple_of(first, SLOT_ALIGN), SLOT_ALIGN)


def _dispatch_kernel(dst_ref, off_ref, n_ref, sdst_ref, sn_ref, xa_ref, lm_ref, xs_hbm, buf, zbuf, sems, *, p):
    i = pl.program_id(0)
    last = pl.num_programs(0) - 1
    slot = i % 2
    t = xa_ref.shape[0]
    one_src, one_dst = buf.at[0, pl.ds(0, SLOT_ALIGN)], xs_hbm.at[pl.ds(0, SLOT_ALIGN)]

    @pl.when(i >= 2)
    def _():
        _wait_runs(n_ref, i - 2, one_src, one_dst, sems.at[slot])

    pos1 = lm_ref[0, 0:1, :].astype(jnp.int32)
    pos2 = lm_ref[0, 1:2, :].astype(jnp.int32)
    xa = xa_ref[...]
    for r0 in range(0, p, PERM_CHUNK):
        row = r0 + lax.broadcasted_iota(jnp.int32, (PERM_CHUNK, t), 0)
        perm = ((row == pos1) | (row == pos2)).astype(BF16)
        buf[slot, r0:r0 + PERM_CHUNK] = jnp.dot(perm, xa, preferred_element_type=F32).astype(BF16)
    _start_runs(n_ref, i, lambda e: off_ref[i * N_EXPERTS + e], lambda e: dst_ref[i * N_EXPERTS + e],
                lambda first: buf.at[slot, _span(first)], lambda first: xs_hbm.at[_span(first)], sems.at[slot])

    @pl.when(i == last)
    def _():
        @pl.when(i >= 1)
        def _():
            _wait_runs(n_ref, i - 1, one_src, one_dst, sems.at[1 - slot])

        _wait_runs(n_ref, i, one_src, one_dst, sems.at[slot])
        zbuf[...] = jnp.zeros(zbuf.shape, BF16)
        _start_runs(sn_ref, 0, lambda e: 0, lambda e: sdst_ref[e],
                    lambda first: zbuf, lambda first: xs_hbm.at[_span(first)], sems.at[slot])
        _wait_runs(sn_ref, 0, one_src, one_dst, sems.at[slot])


def _expert_kernel(e_ref, blk_ref, valid_ref, xs_ref, wg_ref, wu_ref, wd_ref, ys_ref, *, d):
    r = pl.program_id(0)

    @pl.when(valid_ref[r] == 1)
    def _():
        x = xs_ref[:, :d]
        aux = xs_ref[:, d:].astype(F32)
        is_first = aux[:, 4:5] == e_ref[r].astype(F32)
        gate = jnp.where(is_first, aux[:, 0:1] + aux[:, 1:2], aux[:, 2:3] + aux[:, 3:4])
        a = jnp.dot(x, wg_ref[0, 0], preferred_element_type=F32)
        bu = jnp.dot(x, wu_ref[0, 0], preferred_element_type=F32)
        hid = (a * jax.nn.sigmoid(a)) * bu * gate
        ys_ref[...] = jnp.dot(hid.astype(BF16), wd_ref[0, 0], preferred_element_type=F32).astype(BF16)


def _combine_kernel(*refs, final_norm):
    if final_norm:
        dst_ref, off_ref, n_ref, ys_hbm, tm_ref, x_ref, g2_ref, fg_ref, o_ref, buf, sems = refs
    else:
        dst_ref, off_ref, n_ref, ys_hbm, tm_ref, x_ref, g2_ref, o_ref, buf, sems = refs
    i = pl.program_id(0)
    slot = i % 2
    t = x_ref.shape[0]
    p = buf.shape[1]

    def fetch(tile, s):
        buf[s] = jnp.zeros(buf.shape[1:], BF16)
        _start_runs(n_ref, tile, lambda e: dst_ref[tile * N_EXPERTS + e], lambda e: off_ref[tile * N_EXPERTS + e],
                    lambda first: ys_hbm.at[_span(first)], lambda first: buf.at[s, _span(first)], sems.at[s])

    @pl.when(i == 0)
    def _():
        fetch(i, slot)

    @pl.when(i + 1 < pl.num_programs(0))
    def _():
        fetch(i + 1, 1 - slot)

    _wait_runs(n_ref, i, ys_hbm.at[pl.ds(0, SLOT_ALIGN)], buf.at[0, pl.ds(0, SLOT_ALIGN)], sems.at[slot])
    pos1 = tm_ref[:, 0:1].astype(jnp.int32)
    pos2 = tm_ref[:, 1:2].astype(jnp.int32)
    f = None
    for r0 in range(0, p, PERM_CHUNK):
        col = r0 + lax.broadcasted_iota(jnp.int32, (t, PERM_CHUNK), 1)
        perm_t = ((col == pos1) | (col == pos2)).astype(BF16)
        part = jnp.dot(perm_t, buf[slot, r0:r0 + PERM_CHUNK], preferred_element_type=F32)
        f = part if f is None else f + part
    y = x_ref[...] + g2_ref[0] * f
    if final_norm:
        y = _rmsnorm(y, fg_ref[...])
    o_ref[...] = y


def _moe(x, xa, info_tm, info_lm, runs, g2, wg_bf, wu_bf, wd_bf, layer, final_g):
    b, l, d = x.shape
    da = xa.shape[2]
    f = wg_bf.shape[3]
    t = min(MOE_TOKEN_TILE, l)
    tiles_per_batch = l // t
    n_tiles = b * tiles_per_batch
    p = 2 * t + N_EXPERTS * SLOT_ALIGN
    n = b * l
    tm = MOE_ROW_TILE if 2 * n >= N_EXPERTS * MOE_ROW_TILE else MOE_ROW_TILE // 4
    plan, n_row_tiles = _moe_plan(runs.reshape(n_tiles, ROUTER_LANES), t, tm)
    s_rows = n_row_tiles * tm

    xs = pl.pallas_call(
        functools.partial(_dispatch_kernel, p=p),
        grid_spec=pltpu.PrefetchScalarGridSpec(
            num_scalar_prefetch=5,
            grid=(n_tiles,),
            in_specs=[
                pl.BlockSpec((t, da), lambda i, *_: (i, 0)),
                pl.BlockSpec((1, 8, t), lambda i, *_: (i // tiles_per_batch, 0, i % tiles_per_batch)),
            ],
            out_specs=pl.BlockSpec(memory_space=pl.ANY),
            scratch_shapes=[pltpu.VMEM((2, p, da), BF16), pltpu.VMEM((SLOT_ALIGN, da), BF16),
                            pltpu.SemaphoreType.DMA((2,))],
        ),
        out_shape=jax.ShapeDtypeStruct((s_rows, da), BF16),
        compiler_params=_params(("arbitrary",)),
        name="moe_dispatch",
    )(plan["dst"], plan["off"], plan["n"], plan["slack_dst"], plan["slack_n"], xa.reshape(n, da), info_lm)

    w_map = lambda r, e_ref, blk_ref, valid_ref: (layer, e_ref[r], 0, 0)
    row_map = lambda r, e_ref, blk_ref, valid_ref: (blk_ref[r], 0)
    ys = pl.pallas_call(
        functools.partial(_expert_kernel, d=d),
        grid_spec=pltpu.PrefetchScalarGridSpec(
            num_scalar_prefetch=3,
            grid=(n_row_tiles,),
            in_specs=[
                pl.BlockSpec((tm, da), row_map),
                pl.BlockSpec((1, 1, d, f), w_map),
                pl.BlockSpec((1, 1, d, f), w_map),
                pl.BlockSpec((1, 1, f, d), w_map),
            ],
            out_specs=pl.BlockSpec((tm, d), row_map),
        ),
        out_shape=jax.ShapeDtypeStruct((s_rows, d), BF16),
        compiler_params=_params(("arbitrary",)),
        name="moe_experts",
    )(plan["expert"], plan["block"], plan["valid"], xs, wg_bf, wu_bf, wd_bf)

    final_norm = final_g is not None
    in_specs = [
        pl.BlockSpec(memory_space=pl.ANY),
        pl.BlockSpec((t, ROUTER_LANES), lambda i, *_: (i, 0)),
        pl.BlockSpec((t, d), lambda i, *_: (i, 0)),
        pl.BlockSpec((1, 1, d), lambda i, *_: (i // tiles_per_batch, 0, 0)),
    ]
    args = [ys, info_tm.reshape(n, ROUTER_LANES), x.reshape(n, d), g2]
    if final_norm:
        in_specs.append(pl.BlockSpec((1, d), lambda i, *_: (0, 0)))
        args.append(final_g.reshape(1, d))
    out = pl.pallas_call(
        functools.partial(_combine_kernel, final_norm=final_norm),
        grid_spec=pltpu.PrefetchScalarGridSpec(
            num_scalar_prefetch=3,
            grid=(n_tiles,),
            in_specs=in_specs,
            out_specs=pl.BlockSpec((t, d), lambda i, *_: (i, 0)),
            scratch_shapes=[pltpu.VMEM((2, p, d), BF16), pltpu.SemaphoreType.DMA((2,))],
        ),
        out_shape=jax.ShapeDtypeStruct((n, d), F32),
        compiler_params=_params(("arbitrary",)),
        name="moe_combine",
    )(plan["dst"], plan["off"], plan["n"], *args)
    return out.reshape(b, l, d)


def kernel(x, c, ctx, c_ctx, ada_w, ada_b, norm1_g, norm2_g, w_in, pool_w, pool_scale, lambda_q1, lambda_k1,
           lambda_q2, lambda_k2, subln_g, w_out, router_coarse_w, router_coarse_b, router_fine_w,
           router_fine_b, w_gate, w_up, w_down, final_g):
    b, l, d = x.shape
    depth = ada_w.shape[0]
    assert b + 1 <= MOD_ROWS and l % GRID_W == 0

    tables = _rope_tables(l)
    cc = jnp.zeros((MOD_ROWS, d), F32).at[:b].set(c).at[b].set(c_ctx)
    d_pool = d // 2
    d_attn = (w_in.shape[-1] - d_pool) // 3
    w_uk = jnp.concatenate([w_in[..., :d_pool], w_in[..., d_pool + d_attn:d_pool + 2 * d_attn]],
                           axis=-1).astype(BF16)
    w_qvt = jnp.swapaxes(jnp.concatenate([w_in[..., d_pool:d_pool + d_attn], w_in[..., d_pool + 2 * d_attn:]],
                                         axis=-1), 1, 2).astype(BF16)
    pool_w_bf = pool_w.astype(BF16)
    w_out_bf = w_out.astype(BF16)
    n_e = w_gate.shape[1] * w_gate.shape[2]
    f = w_gate.shape[-1]
    wg_bf = w_gate.astype(BF16).reshape(depth, n_e, d, f)
    wu_bf = w_up.astype(BF16).reshape(depth, n_e, d, f)
    wd_bf = w_down.astype(BF16).reshape(depth, n_e, f, d)
    pad = ROUTER_LANES - N_GROUPS - N_EXPERTS
    wr = jnp.concatenate([router_coarse_w, router_fine_w, jnp.zeros((depth, d, pad), F32)], axis=-1)
    wr_hi = wr.astype(BF16)
    wr = jnp.stack([wr_hi, (wr - wr_hi.astype(F32)).astype(BF16)], axis=1)
    rb = jnp.concatenate([router_coarse_b, router_fine_b, jnp.zeros((depth, pad), F32)], axis=-1)

    xc = ctx
    for layer in range(depth):
        last = layer == depth - 1
        lam_init = 0.8 - 0.6 * math.exp(-0.3 * layer)
        mod = _ada(cc, ada_w, ada_b, layer)
        sh1, sc1, g1, sh2, sc2, g2 = jnp.split(mod[:b, None, :], 6, axis=-1)
        csh1, csc1, cg1, csh2, csc2, cg2 = jnp.split(
            jnp.broadcast_to(mod[b][None, None, :], (b, 1, mod.shape[1])), 6, axis=-1)
        lam4 = jnp.stack([lambda_q1[layer], lambda_k1[layer], lambda_q2[layer], lambda_k2[layer]])

        u, q, k, v = _inproj(x, sh1, sc1, norm1_g[layer], w_uk[layer], w_qvt[layer], tables)
        uc, qc, kc, vc = _inproj(xc, csh1, csc1, norm1_g[layer], w_uk[layer], w_qvt[layer], None)
        attn = _attention(q, k, v, kc, vc, lam4, subln_g[layer], lam_init)
        merge_w = (pool_w_bf[layer], pool_scale[layer], w_out_bf[layer])
        route_w = (wr[layer], rb[layer].reshape(1, ROUTER_LANES))
        x, *routed = _merge(u, attn, x, *merge_w, g1, norm2_g[layer], sh2, sc2, *route_w)
        if not last:
            attn_c = _attention(qc, kc, vc, None, None, lam4, subln_g[layer], lam_init)
            xc, *routed_c = _merge(uc, attn_c, xc, *merge_w, cg1, norm2_g[layer], csh2, csc2, *route_w)
            xc = _moe(xc, *routed_c, cg2, wg_bf, wu_bf, wd_bf, layer, None)
        x = _moe(x, *routed, g2, wg_bf, wu_bf, wd_bf, layer, final_g if last else None)
    return x
```

```python
import functools
import math

import jax
import jax.numpy as jnp
from jax import lax
from jax.experimental import pallas as pl
from jax.experimental.pallas import tpu as pltpu

F32 = jnp.float32
BF16 = jnp.bfloat16

GRID_W = 64
POOL_WINDOWS = (2, 4, 8, 16)
HEAD_DIM = 64
HEAD_PAIR = 2 * HEAD_DIM
ROPE_THETA = 10000.0
N_GROUPS = 4
EXPERTS_PER_GROUP = 8
N_EXPERTS = N_GROUPS * EXPERTS_PER_GROUP
EPS = 1e-6
HALO = 16
ROUTER_LANES = 128
MOD_ROWS = 8
ONES_ROWS = 16
MAX_EXCESS = 24.0
KEY_GROUP = 16
SLOT_ALIGN = 16
MOE_TOKEN_TILE = 512
MOE_ROW_TILE = 512
PERM_CHUNK = 256
MERGE_PARTS = 2

VMEM_LIMIT = 48 * 1024 * 1024


def _params(sem):
    return pltpu.CompilerParams(dimension_semantics=sem, vmem_limit_bytes=VMEM_LIMIT)


def _ada_kernel(cc_ref, w_ref, b_ref, o_ref):
    s = cc_ref[...]
    s = s * jax.nn.sigmoid(s)
    o_ref[...] = jnp.dot(s, w_ref[0], precision=lax.Precision.HIGHEST,
                         preferred_element_type=F32) + b_ref[0]


def _ada(cc, ada_w, ada_b, layer):
    _, d, n = ada_w.shape
    bn = 1024
    return pl.pallas_call(
        _ada_kernel,
        grid=(n // bn,),
        in_specs=[
            pl.BlockSpec((MOD_ROWS, d), lambda j: (0, 0)),
            pl.BlockSpec((1, d, bn), lambda j: (layer, 0, j)),
            pl.BlockSpec((1, 1, bn), lambda j: (layer, 0, j)),
        ],
        out_specs=pl.BlockSpec((MOD_ROWS, bn), lambda j: (0, j)),
        out_shape=jax.ShapeDtypeStruct((MOD_ROWS, n), F32),
        compiler_params=_params(("arbitrary",)),
        name="ada",
    )(cc, ada_w, ada_b.reshape(ada_b.shape[0], 1, n))


def _rope_tables(n):
    pos = jnp.arange(n, dtype=jnp.int32)
    row = (pos // GRID_W).astype(F32)
    col = (pos % GRID_W).astype(F32)
    n_freq = HEAD_DIM // 4
    inv = ROPE_THETA ** (-jnp.arange(n_freq, dtype=F32) / n_freq)
    ang_r = row[:, None] * inv
    ang_c = col[:, None] * inv
    cos = jnp.concatenate([jnp.cos(ang_r), jnp.cos(ang_r), jnp.cos(ang_c), jnp.cos(ang_c)], axis=-1)
    sin = jnp.concatenate([-jnp.sin(ang_r), jnp.sin(ang_r), -jnp.sin(ang_c), jnp.sin(ang_c)], axis=-1)
    cos, sin = jnp.tile(cos, (1, 2)), jnp.tile(sin, (1, 2))
    return cos, sin, cos.T, sin.T


def _rmsnorm(x, g):
    ms = jnp.mean(x * x, axis=-1, keepdims=True)
    return x * lax.rsqrt(ms + EPS) * g


def _inproj_kernel(*refs, rope, d_pool, d_attn, q_scale):
    if rope:
        (x_ref, sh_ref, sc_ref, g_ref, wuk_ref, wqv_ref, cos_ref, sin_ref, cost_ref, sint_ref,
         u_ref, qt_ref, k_ref, vt_ref) = refs
    else:
        x_ref, sh_ref, sc_ref, g_ref, wuk_ref, wqv_ref, u_ref, qt_ref, k_ref, vt_ref = refs
    h = (_rmsnorm(x_ref[0], g_ref[...]) * (1.0 + sc_ref[0]) + sh_ref[0]).astype(BF16)
    p = jnp.dot(h, wuk_ref[...], preferred_element_type=F32)
    pt = lax.dot_general(wqv_ref[...], h, (((1,), (1,)), ((), ())), preferred_element_type=F32)
    u_ref[0] = p[:, :d_pool]
    vt_ref[0] = pt[d_attn:].astype(BF16)
    half = HEAD_DIM // 4
    if rope:
        cos, sin = cos_ref[...], sin_ref[...]
        cost, sint = cost_ref[...], sint_ref[...]
        lane = lax.broadcasted_iota(jnp.int32, cos.shape, 1)
        first_half = (lane % (2 * half)) < half
    for h0 in range(0, d_attn, HEAD_PAIR):
        c = p[:, d_pool + h0: d_pool + h0 + HEAD_PAIR]
        ct = pt[h0:h0 + HEAD_PAIR]
        if rope:
            partner = jnp.where(first_half, pltpu.roll(c, HEAD_PAIR - half, 1), pltpu.roll(c, half, 1))
            c = c * cos + partner * sin
            swapped = []
            for r0 in range(0, HEAD_PAIR, 2 * half):
                swapped += [ct[r0 + half:r0 + 2 * half], ct[r0:r0 + half]]
            ct = ct * cost + jnp.concatenate(swapped, axis=0) * sint
        k_ref[0, :, h0:h0 + HEAD_PAIR] = c.astype(BF16)
        qt_ref[0, h0:h0 + HEAD_PAIR, :] = (ct * q_scale).astype(BF16)


def _inproj(x, shift, scale, g, w_uk, w_qvt, tables):
    b, l, d = x.shape
    d_pool = d // 2
    d_attn = w_qvt.shape[0] // 2
    t = min(512, l)
    rope = tables is not None
    row = lambda bi, i: (bi, i, 0)
    col = lambda bi, i: (bi, 0, i)
    per_batch = lambda bi, i: (bi, 0, 0)
    const = lambda bi, i: (0, 0)
    in_specs = [
        pl.BlockSpec((1, t, d), row),
        pl.BlockSpec((1, 1, d), per_batch),
        pl.BlockSpec((1, 1, d), per_batch),
        pl.BlockSpec((1, d), const),
        pl.BlockSpec(w_uk.shape, const),
        pl.BlockSpec(w_qvt.shape, const),
    ]
    args = [x, shift, scale, g.reshape(1, d), w_uk, w_qvt]
    if rope:
        in_specs += [pl.BlockSpec((t, HEAD_PAIR), lambda bi, i: (i, 0))] * 2
        in_specs += [pl.BlockSpec((HEAD_PAIR, t), lambda bi, i: (0, i))] * 2
        args += list(tables)
    return pl.pallas_call(
        functools.partial(_inproj_kernel, rope=rope, d_pool=d_pool, d_attn=d_attn,
                          q_scale=HEAD_DIM ** -0.5 * math.log2(math.e)),
        grid=(b, l // t),
        in_specs=in_specs,
        out_specs=[pl.BlockSpec((1, t, d_pool), row), pl.BlockSpec((1, d_attn, t), col),
                   pl.BlockSpec((1, t, d_attn), row), pl.BlockSpec((1, d_attn, t), col)],
        out_shape=[jax.ShapeDtypeStruct((b, l, d_pool), F32), jax.ShapeDtypeStruct((b, d_attn, l), BF16),
                   jax.ShapeDtypeStruct((b, l, d_attn), BF16), jax.ShapeDtypeStruct((b, d_attn, l), BF16)],
        compiler_params=_params(("arbitrary", "arbitrary")),
        name="inproj",
    )(*args)


def _attn_kernel(*refs, tq, tk, cb, n_chunks, group, has_ctx, lam_init):
    if has_ctx:
        (lam_ref, qt_ref, k_ref, vt_ref, kc_ref, vct_ref, g_ref, o_ref,
         qs_scr, m_scr, l_scr, acc_scr, s_scr) = refs
    else:
        lam_ref, qt_ref, k_ref, vt_ref, g_ref, o_ref, qs_scr, m_scr, l_scr, acc_scr, s_scr = refs
    qt = qt_ref[0]
    sub = lax.broadcasted_iota(jnp.int32, qt.shape, 0)
    zero = jnp.zeros_like(qt)
    qs_scr[:, :tq] = jnp.where(sub < HEAD_DIM, qt, zero)
    qs_scr[:, tq:] = jnp.where(sub >= HEAD_DIM, qt, zero)
    m_scr[...] = jnp.full(m_scr.shape, -jnp.inf, F32)
    l_scr[...] = jnp.zeros(l_scr.shape, F32)
    acc_scr[...] = jnp.zeros(acc_scr.shape, F32)

    n_blocks = 2 * tq // cb
    blocks = [slice(c * cb, (c + 1) * cb) for c in range(n_blocks)]

    def scores(kj, c):
        return jnp.dot(kj, qs_scr[:, blocks[c]], preferred_element_type=F32)

    def with_ones(vtj):
        return jnp.concatenate([vtj, jnp.ones((ONES_ROWS, vtj.shape[1]), BF16)], axis=0)

    def span(j):
        return pl.ds(j * tk if isinstance(j, int) else pl.multiple_of(j * tk, tk), tk)

    def keys(j):
        return k_ref[0, span(j), :]

    def values_t(j):
        return vt_ref[0, :, span(j)]

    def online_chunk(kj, vtj):
        vta = with_ones(vtj)
        s_next = scores(kj, 0)
        for c, cols in enumerate(blocks):
            s = s_next
            if c + 1 < n_blocks:
                s_next = scores(kj, c + 1)
            m_old = m_scr[:, cols]
            m_new = jnp.maximum(m_old, jnp.max(s, axis=0, keepdims=True))
            alpha = jnp.exp2(m_old - m_new)
            p = jnp.exp2(s - m_new).astype(BF16)
            r = jnp.dot(vta, p, preferred_element_type=F32)
            l_scr[:, cols] = alpha * l_scr[:, cols] + r[HEAD_PAIR:HEAD_PAIR + 1]
            acc_scr[:, cols] = alpha * acc_scr[:, cols] + r[:HEAD_PAIR]
            m_scr[:, cols] = m_new

    def group_fast(j0, j_next):
        items = [(c, jj) for c in range(n_blocks) for jj in range(group)]
        s_next = s_scr[...]
        sums, excess = [], None
        for idx, (c, jj) in enumerate(items):
            s = s_next
            if idx + 1 < len(items):
                c2, jj2 = items[idx + 1]
                s_next = scores(keys(j0 + jj2), c2)
            elif j_next is not None:
                s_scr[...] = scores(keys(j_next), 0)
            m_ref = m_scr[:, blocks[c]]
            top = jnp.max(s, axis=0, keepdims=True)
            p = jnp.exp2(s - m_ref).astype(BF16)
            r = jnp.dot(with_ones(values_t(j0 + jj)), p, preferred_element_type=F32)
            r_c, top_c = (r, top) if jj == 0 else (r_c + r, jnp.maximum(top_c, top))
            if jj == group - 1:
                sums.append(r_c)
                over = jnp.max(top_c - m_ref)
                excess = over if excess is None else jnp.maximum(excess, over)
        ok = excess <= MAX_EXCESS

        @pl.when(ok)
        def _():
            for cols, r_c in zip(blocks, sums):
                l_scr[:, cols] += r_c[HEAD_PAIR:HEAD_PAIR + 1]
                acc_scr[:, cols] += r_c[:HEAD_PAIR]

        return ok

    def group_step(j0, j_next):
        ok = group_fast(j0, j_next)

        @pl.when(jnp.logical_not(ok))
        def _():
            def redo(jj, carry):
                online_chunk(keys(j0 + jj), values_t(j0 + jj))
                return carry

            lax.fori_loop(0, group, redo, 0)

    n_groups = n_chunks // group
    if has_ctx:
        online_chunk(kc_ref[0], vct_ref[0])
    s_scr[...] = scores(keys(0), 0)
    if n_groups > 1:
        def body(g, carry):
            group_step(g * group, (g + 1) * group)
            return carry

        lax.fori_loop(0, n_groups - 1, body, 0)
    group_step((n_groups - 1) * group, None)

    o = acc_scr[...] / l_scr[...]
    lv = lam_ref[...]
    lam = (jnp.exp(jnp.sum(lv[0:1] * lv[1:2], axis=1, keepdims=True))
           - jnp.exp(jnp.sum(lv[2:3] * lv[3:4], axis=1, keepdims=True)) + lam_init)
    a = (o[:, :tq] - lam * o[:, tq:]).T
    o_ref[0] = (_rmsnorm(a, g_ref[...]) * (1.0 - lam_init)).astype(BF16)


def _attention(qt, k, vt, kc, vct, lam4, subln_g, lam_init):
    b, d_attn, lq = qt.shape
    lk = k.shape[1]
    n_heads = d_attn // HEAD_PAIR
    tq = min(1024, lq)
    tk = min(512, lk)
    cb = min(512, 2 * tq)
    has_ctx = kc is not None
    qmap = lambda bi, h, i: (bi, h, i)
    kmap = lambda bi, h, i: (bi, 0, h)
    vmap = lambda bi, h, i: (bi, h, 0)
    const = lambda bi, h, i: (0, 0)
    in_specs = [
        pl.BlockSpec(lam4.shape, const),
        pl.BlockSpec((1, HEAD_PAIR, tq), qmap),
        pl.BlockSpec((1, lk, HEAD_PAIR), kmap),
        pl.BlockSpec((1, HEAD_PAIR, lk), vmap),
    ]
    args = [lam4, qt, k, vt]
    if has_ctx:
        lc = kc.shape[1]
        in_specs += [pl.BlockSpec((1, lc, HEAD_PAIR), kmap), pl.BlockSpec((1, HEAD_PAIR, lc), vmap)]
        args += [kc, vct]
    in_specs.append(pl.BlockSpec((1, HEAD_PAIR), const))
    args.append(subln_g.reshape(1, HEAD_PAIR))
    return pl.pallas_call(
        functools.partial(_attn_kernel, tq=tq, tk=tk, cb=cb, n_chunks=lk // tk,
                          group=math.gcd(KEY_GROUP, lk // tk), has_ctx=has_ctx,
                          lam_init=lam_init),
        grid=(b, n_heads, lq // tq),
        in_specs=in_specs,
        out_specs=pl.BlockSpec((1, tq, HEAD_PAIR), lambda bi, h, i: (bi, i, h)),
        out_shape=jax.ShapeDtypeStruct((b, lq, d_attn), BF16),
        scratch_shapes=[
            pltpu.VMEM((HEAD_PAIR, 2 * tq), BF16),
            pltpu.VMEM((1, 2 * tq), F32),
            pltpu.VMEM((1, 2 * tq), F32),
            pltpu.VMEM((HEAD_PAIR, 2 * tq), F32),
            pltpu.VMEM((tk, cb), F32),
        ],
        compiler_params=_params(("arbitrary", "arbitrary", "arbitrary")),
        name="diff_attn",
    )(*args)


def _route(logits):
    lane = lax.broadcasted_iota(jnp.int32, logits.shape, 1)
    big = jnp.int32(ROUTER_LANES)
    neg = jnp.float32(-jnp.inf)
    lc = jnp.where(lane < N_GROUPS, logits, neg)
    mc = jnp.max(lc, axis=1, keepdims=True)
    p_group = 1.0 / jnp.sum(jnp.exp(lc - mc), axis=1, keepdims=True)
    g_idx = jnp.min(jnp.where(lc == mc, lane, big), axis=1, keepdims=True)
    lo = N_GROUPS + EXPERTS_PER_GROUP * g_idx
    lf = jnp.where((lane >= lo) & (lane < lo + EXPERTS_PER_GROUP), logits, neg)
    v1 = jnp.max(lf, axis=1, keepdims=True)
    i1 = jnp.min(jnp.where(lf == v1, lane, big), axis=1, keepdims=True)
    lf2 = jnp.where(lane == i1, neg, lf)
    v2 = jnp.max(lf2, axis=1, keepdims=True)
    i2 = jnp.min(jnp.where(lf2 == v2, lane, big), axis=1, keepdims=True)
    e2 = jnp.exp(v2 - v1)
    w1 = p_group / (1.0 + e2)
    w2 = p_group * e2 / (1.0 + e2)
    return lane, i1, i2, w1, w2


def _split_bf16(x):
    hi = x.astype(BF16)
    return hi, (x - hi.astype(F32)).astype(BF16)


def _sorted_positions(lane, i1, i2):
    t = lane.shape[0]
    oh1 = lane == i1
    oh2 = lane == i2
    oh = jnp.concatenate([oh1, oh2], axis=1).astype(BF16)
    r = lax.broadcasted_iota(jnp.int32, (t, t), 0)
    c = lax.broadcasted_iota(jnp.int32, (t, t), 1)
    before = (c < r).astype(BF16)
    rank = jnp.dot(before, oh, preferred_element_type=F32)
    cnt1 = jnp.sum(oh1.astype(F32), axis=0, keepdims=True)
    cnt2 = jnp.sum(oh2.astype(F32), axis=0, keepdims=True)
    run = jnp.ceil((cnt1 + cnt2) * (1.0 / SLOT_ALIGN)) * SLOT_ALIGN
    lr = lax.broadcasted_iota(jnp.int32, (ROUTER_LANES, ROUTER_LANES), 0)
    lc = lax.broadcasted_iota(jnp.int32, (ROUTER_LANES, ROUTER_LANES), 1)
    start = jnp.dot(jnp.broadcast_to(run, (8, ROUTER_LANES)).astype(BF16), (lr < lc).astype(BF16),
                    preferred_element_type=F32)[0:1]
    pos1 = jnp.sum(jnp.where(oh1, start + rank[:, :ROUTER_LANES], 0.0), axis=1, keepdims=True)
    pos2 = jnp.sum(jnp.where(oh2, start + cnt1 + rank[:, ROUTER_LANES:], 0.0), axis=1, keepdims=True)
    return pos1, pos2, run


def _merge_kernel(u_ref, up_ref, un_ref, a_ref, x_ref, pw_ref, ps_ref, wo_ref, g1_ref, n2_ref, sh_ref,
                  sc_ref, wr_ref, rb_ref, xo_ref, xa_ref, tm_ref, lm_ref, run_ref, ext_scr, *, t, seq_len,
                  d_pool):
    i = pl.program_id(1)
    n_tiles = pl.num_programs(1)
    u = u_ref[0]
    ext_scr[0:HALO] = jnp.where(i > 0, up_ref[0], 0.0)
    ext_scr[HALO:HALO + t] = u
    ext_scr[HALO + t:] = jnp.where(i < n_tiles - 1, un_ref[0], 0.0)

    gc = d_pool // len(POOL_WINDOWS)
    d = x_ref.shape[2]
    tr = t // MERGE_PARTS
    parts = [slice(r0, r0 + tr) for r0 in range(0, t, tr)]

    def pool(rows):
        pos = i * t + rows.start + lax.broadcasted_iota(jnp.int32, (tr, 1), 0)
        pooled = []
        for gi, w in enumerate(POOL_WINDOWS):
            left = w // 2
            right = w - 1 - left
            cols = slice(gi * gc, (gi + 1) * gc)
            s = ext_scr[pl.ds(rows.start + HALO - left, tr), cols]
            for o in range(-left + 1, right + 1):
                s = s + ext_scr[pl.ds(rows.start + HALO + o, tr), cols]
            cnt = (jnp.minimum(pos + right + 1, seq_len) - jnp.maximum(pos - left, 0)).astype(F32)
            pooled.append((s / cnt - u_ref[0, rows, cols]).astype(BF16))
        return pooled

    def project(rows, pooled):
        mix = jnp.dot(a_ref[0, rows], wo_ref[d_pool:, :], preferred_element_type=F32)
        for gi, pg in enumerate(pooled):
            cols = slice(gi * gc, (gi + 1) * gc)
            mixed = jnp.dot(pg, pw_ref[gi], preferred_element_type=F32) * ps_ref[:, cols]
            mix = mix + jnp.dot(mixed.astype(BF16), wo_ref[cols, :], preferred_element_type=F32)
        return mix

    def renorm(rows, mix):
        x_new = x_ref[0, rows] + g1_ref[0] * mix
        xo_ref[0, rows] = x_new
        h2 = _rmsnorm(x_new, n2_ref[...]) * (1.0 + sc_ref[0]) + sh_ref[0]
        h_hi, h_lo = _split_bf16(h2)
        xa_ref[0, rows, :d] = h_hi
        return h_hi, h_lo

    def router(h_hi, h_lo):
        return (jnp.dot(h_hi, wr_ref[0], preferred_element_type=F32)
                + jnp.dot(h_lo, wr_ref[0], preferred_element_type=F32)
                + jnp.dot(h_hi, wr_ref[1], preferred_element_type=F32))

    n_p = len(parts)
    pooled = [None] * n_p
    mix = [None] * n_p
    split = [None] * n_p
    logits = [None] * n_p
    pooled[0] = pool(parts[0])
    for k in range(n_p):
        mix[k] = project(parts[k], pooled[k])
        if k + 1 < n_p:
            pooled[k + 1] = pool(parts[k + 1])
        if k >= 1:
            logits[k - 1] = router(*split[k - 1])
        split[k] = renorm(parts[k], mix[k])
    logits[n_p - 1] = router(*split[n_p - 1])
    logits = jnp.concatenate(logits, axis=0) + rb_ref[...]
    lane, i1, i2, w1, w2 = _route(logits)
    pos1, pos2, run = _sorted_positions(lane, i1, i2)
    w1_hi = w1.astype(BF16).astype(F32)
    w2_hi = w2.astype(BF16).astype(F32)
    aux = jnp.zeros(lane.shape, F32)
    for k, val in enumerate((w1_hi, w1 - w1_hi, w2_hi, w2 - w2_hi, (i1 - N_GROUPS).astype(F32))):
        aux = jnp.where(lane == k, val, aux)
    xa_ref[0, :, d:] = aux.astype(BF16)
    info = jnp.where(lane == 0, pos1, jnp.where(lane == 1, pos2, 0.0))
    tm_ref[0] = info
    lm_ref[0] = info.T[:8]
    run_ref[0, 0] = run


def _merge(u, attn, x, pool_w_bf, pool_scale, w_out_bf, g1, norm2_g, sh2, sc2, wr, rb):
    b, l, d = x.shape
    d_pool = u.shape[2]
    t = min(MOE_TOKEN_TILE, l)
    hb = t // HALO
    n_halo = l // HALO
    row = lambda bi, i: (bi, i, 0)
    per_batch = lambda bi, i: (bi, 0, 0)
    const2 = lambda bi, i: (0, 0)
    const3 = lambda bi, i: (0, 0, 0)
    return pl.pallas_call(
        functools.partial(_merge_kernel, t=t, seq_len=l, d_pool=d_pool),
        grid=(b, l // t),
        in_specs=[
            pl.BlockSpec((1, t, d_pool), row),
            pl.BlockSpec((1, HALO, d_pool), lambda bi, i: (bi, jnp.maximum(i * hb - 1, 0), 0)),
            pl.BlockSpec((1, HALO, d_pool), lambda bi, i: (bi, jnp.minimum((i + 1) * hb, n_halo - 1), 0)),
            pl.BlockSpec((1, t, attn.shape[2]), row),
            pl.BlockSpec((1, t, d), row),
            pl.BlockSpec(pool_w_bf.shape, const3),
            pl.BlockSpec((1, d_pool), const2),
            pl.BlockSpec(w_out_bf.shape, const2),
            pl.BlockSpec((1, 1, d), per_batch),
            pl.BlockSpec((1, d), const2),
            pl.BlockSpec((1, 1, d), per_batch),
            pl.BlockSpec((1, 1, d), per_batch),
            pl.BlockSpec(wr.shape, const3),
            pl.BlockSpec((1, ROUTER_LANES), const2),
        ],
        out_specs=[
            pl.BlockSpec((1, t, d), row),
            pl.BlockSpec((1, t, d + ROUTER_LANES), row),
            pl.BlockSpec((1, t, ROUTER_LANES), row),
            pl.BlockSpec((1, 8, t), lambda bi, i: (bi, 0, i)),
            pl.BlockSpec((1, 1, 1, ROUTER_LANES), lambda bi, i: (bi, i, 0, 0)),
        ],
        out_shape=[
            jax.ShapeDtypeStruct((b, l, d), F32),
            jax.ShapeDtypeStruct((b, l, d + ROUTER_LANES), BF16),
            jax.ShapeDtypeStruct((b, l, ROUTER_LANES), F32),
            jax.ShapeDtypeStruct((b, 8, l), F32),
            jax.ShapeDtypeStruct((b, l // t, 1, ROUTER_LANES), F32),
        ],
        scratch_shapes=[pltpu.VMEM((t + 2 * HALO, d_pool), F32)],
        compiler_params=_params(("arbitrary", "arbitrary")),
        name="merge",
    )(u, u, u, attn, x, pool_w_bf, pool_scale.reshape(1, d_pool), w_out_bf, g1, norm2_g.reshape(1, d),
      sh2, sc2, wr, rb)


def _moe_plan(runs, t, tm):
    counts = runs[:, N_GROUPS:N_GROUPS + N_EXPERTS].astype(jnp.int32)
    n_tiles = counts.shape[0]
    total = jnp.sum(counts, axis=0)
    region = (total + tm - 1) // tm * tm
    region_end = jnp.cumsum(region)
    start = region_end - region
    dst = start[None, :] + jnp.cumsum(counts, axis=0) - counts
    off = jnp.cumsum(counts, axis=1) - counts
    s_max = 2 * n_tiles * t + n_tiles * N_EXPERTS * SLOT_ALIGN + N_EXPERTS * tm
    n_row_tiles = -(-s_max // tm)
    first_row = jnp.arange(n_row_tiles, dtype=jnp.int32) * tm
    valid = first_row < region_end[-1]
    last_valid = jnp.maximum(region_end[-1] // tm - 1, 0)
    block = jnp.where(valid, jnp.arange(n_row_tiles, dtype=jnp.int32), last_valid)
    expert = jnp.sum((region_end[None, :] <= (block * tm)[:, None]).astype(jnp.int32), axis=1)
    expert = jnp.minimum(expert, N_EXPERTS - 1)
    plan = dict(dst=dst.reshape(-1), off=off.reshape(-1), n=(counts // SLOT_ALIGN).reshape(-1),
                slack_dst=start + total, slack_n=(region - total) // SLOT_ALIGN,
                expert=expert, block=block, valid=valid.astype(jnp.int32))
    return plan, n_row_tiles


def _start_runs(n_ref, tile, src_first, dst_first, src_rows, dst_rows, sem):
    for e in range(N_EXPERTS):
        n = n_ref[tile * N_EXPERTS + e]
        s0, d0 = src_first(e), dst_first(e)

        def issue(j, carry, s0=s0, d0=d0):
            step = j * SLOT_ALIGN
            pltpu.make_async_copy(src_rows(s0 + step), dst_rows(d0 + step), sem).start()
            return carry

        lax.fori_loop(0, n, issue, 0)


def _wait_runs(n_ref, tile, src, dst, sem):
    total = 0
    for e in range(N_EXPERTS):
        total = total + n_ref[tile * N_EXPERTS + e]

    def drain(j, carry):
        pltpu.make_async_copy(src, dst, sem).wait()
        return carry

    lax.fori_loop(0, total, drain, 0)


def _rows_used(off_ref, n_ref, tile):
    last = tile * N_EXPERTS + N_EXPERTS - 1
    return off_ref[last] + n_ref[last] * SLOT_ALIGN


def _span(first):
    return pl.ds(pl.multiple_of(first, SLOT_ALIGN), SLOT_ALIGN)


def _dispatch_kernel(dst_ref, off_ref, n_ref, sdst_ref, sn_ref, xa_ref, lm_ref, xs_hbm, buf, zbuf, sems, *, p):
    i = pl.program_id(0)
    last = pl.num_programs(0) - 1
    slot = i % 2
    t = xa_ref.shape[0]
    one_src, one_dst = buf.at[0, pl.ds(0, SLOT_ALIGN)], xs_hbm.at[pl.ds(0, SLOT_ALIGN)]

    @pl.when(i >= 2)
    def _():
        _wait_runs(n_ref, i - 2, one_src, one_dst, sems.at[slot])

    pos1 = lm_ref[0, 0:1, :].astype(jnp.int32)
    pos2 = lm_ref[0, 1:2, :].astype(jnp.int32)
    used = _rows_used(off_ref, n_ref, i)

    def sort_chunk(r0):
        row = r0 + lax.broadcasted_iota(jnp.int32, (PERM_CHUNK, t), 0)
        perm = ((row == pos1) | (row == pos2)).astype(BF16)
        buf[slot, r0:r0 + PERM_CHUNK] = jnp.dot(perm, xa_ref[...], preferred_element_type=F32).astype(BF16)

    for r0 in range(0, p, PERM_CHUNK):
        if r0 < 2 * t:
            sort_chunk(r0)
        else:
            pl.when(r0 < used)(functools.partial(sort_chunk, r0))
    _start_runs(n_ref, i, lambda e: off_ref[i * N_EXPERTS + e], lambda e: dst_ref[i * N_EXPERTS + e],
                lambda first: buf.at[slot, _span(first)], lambda first: xs_hbm.at[_span(first)], sems.at[slot])

    @pl.when(i == last)
    def _():
        @pl.when(i >= 1)
        def _():
            _wait_runs(n_ref, i - 1, one_src, one_dst, sems.at[1 - slot])

        _wait_runs(n_ref, i, one_src, one_dst, sems.at[slot])
        zbuf[...] = jnp.zeros(zbuf.shape, BF16)
        _start_runs(sn_ref, 0, lambda e: 0, lambda e: sdst_ref[e],
                    lambda first: zbuf, lambda first: xs_hbm.at[_span(first)], sems.at[slot])
        _wait_runs(sn_ref, 0, one_src, one_dst, sems.at[slot])


def _expert_kernel(e_ref, blk_ref, valid_ref, xs_ref, wg_ref, wu_ref, wd_ref, ys_ref, wg_bf, wu_bf, wd_bf, *, d):
    r = pl.program_id(0)
    e = e_ref[r]

    @pl.when(jnp.logical_or(r == 0, e != e_ref[jnp.maximum(r - 1, 0)]))
    def _():
        wg_bf[...] = wg_ref[0, 0, 0].astype(BF16)
        wu_bf[...] = wu_ref[0, 0, 0].astype(BF16)
        wd_bf[...] = wd_ref[0, 0, 0].astype(BF16)

    @pl.when(valid_ref[r] == 1)
    def _():
        x = xs_ref[:, :d]
        aux = xs_ref[:, d:].astype(F32)
        is_first = aux[:, 4:5] == e.astype(F32)
        gate = jnp.where(is_first, aux[:, 0:1] + aux[:, 1:2], aux[:, 2:3] + aux[:, 3:4])
        a = jnp.dot(x, wg_bf[...], preferred_element_type=F32)
        bu = jnp.dot(x, wu_bf[...], preferred_element_type=F32)
        hid = (a * jax.nn.sigmoid(a)) * bu * gate
        ys_ref[...] = jnp.dot(hid.astype(BF16), wd_bf[...], preferred_element_type=F32).astype(BF16)


def _combine_kernel(*refs, final_norm):
    if final_norm:
        dst_ref, off_ref, n_ref, ys_hbm, tm_ref, x_ref, g2_ref, fg_ref, o_ref, buf, sems = refs
    else:
        dst_ref, off_ref, n_ref, ys_hbm, tm_ref, x_ref, g2_ref, o_ref, buf, sems = refs
    i = pl.program_id(0)
    slot = i % 2
    t = x_ref.shape[0]
    p = buf.shape[1]

    def zero_chunk(s, r0):
        buf[s, r0:r0 + PERM_CHUNK] = jnp.zeros((PERM_CHUNK, buf.shape[2]), BF16)

    def fetch(tile, s):
        used_t = _rows_used(off_ref, n_ref, tile)
        for r0 in range(0, p, PERM_CHUNK):
            if r0 < 2 * t:
                zero_chunk(s, r0)
            else:
                pl.when(r0 < used_t)(functools.partial(zero_chunk, s, r0))
        _start_runs(n_ref, tile, lambda e: dst_ref[tile * N_EXPERTS + e], lambda e: off_ref[tile * N_EXPERTS + e],
                    lambda first: ys_hbm.at[_span(first)], lambda first: buf.at[s, _span(first)], sems.at[s])

    @pl.when(i == 0)
    def _():
        fetch(i, slot)

    @pl.when(i + 1 < pl.num_programs(0))
    def _():
        fetch(i + 1, 1 - slot)

    _wait_runs(n_ref, i, ys_hbm.at[pl.ds(0, SLOT_ALIGN)], buf.at[0, pl.ds(0, SLOT_ALIGN)], sems.at[slot])
    pos1 = tm_ref[:, 0:1].astype(jnp.int32)
    pos2 = tm_ref[:, 1:2].astype(jnp.int32)
    used = _rows_used(off_ref, n_ref, i)

    def unsort_chunk(r0):
        col = r0 + lax.broadcasted_iota(jnp.int32, (t, PERM_CHUNK), 1)
        perm_t = ((col == pos1) | (col == pos2)).astype(BF16)
        return jnp.dot(perm_t, buf[slot, r0:r0 + PERM_CHUNK], preferred_element_type=F32)

    def add_chunk(r0):
        o_ref[...] += unsort_chunk(r0)

    f = None
    for r0 in range(0, min(p, 2 * t), PERM_CHUNK):
        part = unsort_chunk(r0)
        f = part if f is None else f + part
    o_ref[...] = f
    for r0 in range(2 * t, p, PERM_CHUNK):
        pl.when(r0 < used)(functools.partial(add_chunk, r0))
    y = x_ref[...] + g2_ref[0] * o_ref[...]
    if final_norm:
        y = _rmsnorm(y, fg_ref[...])
    o_ref[...] = y


def _moe(x, xa, info_tm, info_lm, runs, g2, w_gate, w_up, w_down, layer, final_g):
    b, l, d = x.shape
    da = xa.shape[2]
    f = w_gate.shape[-1]
    t = min(MOE_TOKEN_TILE, l)
    tiles_per_batch = l // t
    n_tiles = b * tiles_per_batch
    p = 2 * t + N_EXPERTS * SLOT_ALIGN
    n = b * l
    tm = MOE_ROW_TILE if 2 * n >= N_EXPERTS * MOE_ROW_TILE else MOE_ROW_TILE // 4
    plan, n_row_tiles = _moe_plan(runs.reshape(n_tiles, ROUTER_LANES), t, tm)
    s_rows = n_row_tiles * tm

    xs = pl.pallas_call(
        functools.partial(_dispatch_kernel, p=p),
        grid_spec=pltpu.PrefetchScalarGridSpec(
            num_scalar_prefetch=5,
            grid=(n_tiles,),
            in_specs=[
                pl.BlockSpec((t, da), lambda i, *_: (i, 0)),
                pl.BlockSpec((1, 8, t), lambda i, *_: (i // tiles_per_batch, 0, i % tiles_per_batch)),
            ],
            out_specs=pl.BlockSpec(memory_space=pl.ANY),
            scratch_shapes=[pltpu.VMEM((2, p, da), BF16), pltpu.VMEM((SLOT_ALIGN, da), BF16),
                            pltpu.SemaphoreType.DMA((2,))],
        ),
        out_shape=jax.ShapeDtypeStruct((s_rows, da), BF16),
        compiler_params=_params(("arbitrary",)),
        name="moe_dispatch",
    )(plan["dst"], plan["off"], plan["n"], plan["slack_dst"], plan["slack_n"], xa.reshape(n, da), info_lm)

    w_map = lambda r, e_ref, blk_ref, valid_ref: (
        layer, e_ref[r] // EXPERTS_PER_GROUP, e_ref[r] % EXPERTS_PER_GROUP, 0, 0)
    row_map = lambda r, e_ref, blk_ref, valid_ref: (blk_ref[r], 0)
    ys = pl.pallas_call(
        functools.partial(_expert_kernel, d=d),
        grid_spec=pltpu.PrefetchScalarGridSpec(
            num_scalar_prefetch=3,
            grid=(n_row_tiles,),
            in_specs=[
                pl.BlockSpec((tm, da), row_map),
                pl.BlockSpec((1, 1, 1, d, f), w_map),
                pl.BlockSpec((1, 1, 1, d, f), w_map),
                pl.BlockSpec((1, 1, 1, f, d), w_map),
            ],
            out_specs=pl.BlockSpec((tm, d), row_map),
            scratch_shapes=[pltpu.VMEM((d, f), BF16), pltpu.VMEM((d, f), BF16), pltpu.VMEM((f, d), BF16)],
        ),
        out_shape=jax.ShapeDtypeStruct((s_rows, d), BF16),
        compiler_params=_params(("arbitrary",)),
        name="moe_experts",
    )(plan["expert"], plan["block"], plan["valid"], xs, w_gate, w_up, w_down)

    final_norm = final_g is not None
    in_specs = [
        pl.BlockSpec(memory_space=pl.ANY),
        pl.BlockSpec((t, ROUTER_LANES), lambda i, *_: (i, 0)),
        pl.BlockSpec((t, d), lambda i, *_: (i, 0)),
        pl.BlockSpec((1, 1, d), lambda i, *_: (i // tiles_per_batch, 0, 0)),
    ]
    args = [ys, info_tm.reshape(n, ROUTER_LANES), x.reshape(n, d), g2]
    if final_norm:
        in_specs.append(pl.BlockSpec((1, d), lambda i, *_: (0, 0)))
        args.append(final_g.reshape(1, d))
    out = pl.pallas_call(
        functools.partial(_combine_kernel, final_norm=final_norm),
        grid_spec=pltpu.PrefetchScalarGridSpec(
            num_scalar_prefetch=3,
            grid=(n_tiles,),
            in_specs=in_specs,
            out_specs=pl.BlockSpec((t, d), lambda i, *_: (i, 0)),
            scratch_shapes=[pltpu.VMEM((2, p, d), BF16), pltpu.SemaphoreType.DMA((2,))],
        ),
        out_shape=jax.ShapeDtypeStruct((n, d), F32),
        compiler_params=_params(("arbitrary",)),
        name="moe_combine",
    )(plan["dst"], plan["off"], plan["n"], *args)
    return out.reshape(b, l, d)


def kernel(x, c, ctx, c_ctx, ada_w, ada_b, norm1_g, norm2_g, w_in, pool_w, pool_scale, lambda_q1, lambda_k1,
           lambda_q2, lambda_k2, subln_g, w_out, router_coarse_w, router_coarse_b, router_fine_w,
           router_fine_b, w_gate, w_up, w_down, final_g):
    b, l, d = x.shape
    depth = ada_w.shape[0]
    assert b + 1 <= MOD_ROWS and l % GRID_W == 0

    tables = _rope_tables(l)
    cc = jnp.zeros((MOD_ROWS, d), F32).at[:b].set(c).at[b].set(c_ctx)
    d_pool = d // 2
    d_attn = (w_in.shape[-1] - d_pool) // 3
    w_uk = jnp.concatenate([w_in[..., :d_pool], w_in[..., d_pool + d_attn:d_pool + 2 * d_attn]],
                           axis=-1).astype(BF16)
    w_qvt = jnp.swapaxes(jnp.concatenate([w_in[..., d_pool:d_pool + d_attn], w_in[..., d_pool + 2 * d_attn:]],
                                         axis=-1), 1, 2).astype(BF16)
    pool_w_bf = pool_w.astype(BF16)
    w_out_bf = w_out.astype(BF16)
    pad = ROUTER_LANES - N_GROUPS - N_EXPERTS
    wr = jnp.concatenate([router_coarse_w, router_fine_w, jnp.zeros((depth, d, pad), F32)], axis=-1)
    wr_hi = wr.astype(BF16)
    wr = jnp.stack([wr_hi, (wr - wr_hi.astype(F32)).astype(BF16)], axis=1)
    rb = jnp.concatenate([router_coarse_b, router_fine_b, jnp.zeros((depth, pad), F32)], axis=-1)

    xc = ctx
    for layer in range(depth):
        last = layer == depth - 1
        lam_init = 0.8 - 0.6 * math.exp(-0.3 * layer)
        mod = _ada(cc, ada_w, ada_b, layer)
        sh1, sc1, g1, sh2, sc2, g2 = jnp.split(mod[:b, None, :], 6, axis=-1)
        csh1, csc1, cg1, csh2, csc2, cg2 = jnp.split(
            jnp.broadcast_to(mod[b][None, None, :], (b, 1, mod.shape[1])), 6, axis=-1)
        lam4 = jnp.stack([lambda_q1[layer], lambda_k1[layer], lambda_q2[layer], lambda_k2[layer]])

        u, q, k, v = _inproj(x, sh1, sc1, norm1_g[layer], w_uk[layer], w_qvt[layer], tables)
        uc, qc, kc, vc = _inproj(xc, csh1, csc1, norm1_g[layer], w_uk[layer], w_qvt[layer], None)
        attn = _attention(q, k, v, kc, vc, lam4, subln_g[layer], lam_init)
        merge_w = (pool_w_bf[layer], pool_scale[layer], w_out_bf[layer])
        route_w = (wr[layer], rb[layer].reshape(1, ROUTER_LANES))
        x, *routed = _merge(u, attn, x, *merge_w, g1, norm2_g[layer], sh2, sc2, *route_w)
        if not last:
            attn_c = _attention(qc, kc, vc, None, None, lam4, subln_g[layer], lam_init)
            xc, *routed_c = _merge(uc, attn_c, xc, *merge_w, cg1, norm2_g[layer], csh2, csc2, *route_w)
            xc = _moe(xc, *routed_c, cg2, w_gate, w_up, w_down, layer, None)
        x = _moe(x, *routed, g2, w_gate, w_up, w_down, layer, final_g if last else None)
    return x
```

```python
import functools
import math

import jax
import jax.numpy as jnp
from jax import lax
from jax.experimental import pallas as pl
from jax.experimental.pallas import tpu as pltpu

F32 = jnp.float32
BF16 = jnp.bfloat16

GRID_W = 64
POOL_WINDOWS = (2, 4, 8, 16)
HEAD_DIM = 64
HEAD_PAIR = 2 * HEAD_DIM
ROPE_THETA = 10000.0
N_GROUPS = 4
EXPERTS_PER_GROUP = 8
N_EXPERTS = N_GROUPS * EXPERTS_PER_GROUP
EPS = 1e-6
HALO = 16
ROUTER_LANES = 128
MOD_ROWS = 8
ONES_ROWS = 16
MAX_EXCESS = 24.0
KEY_GROUP = 16
SLOT_ALIGN = 16
MOE_TOKEN_TILE = 512
MOE_ROW_TILE = 512
PERM_CHUNK = 256
MERGE_PARTS = 2

VMEM_LIMIT = 48 * 1024 * 1024


def _params(sem):
    return pltpu.CompilerParams(dimension_semantics=sem, vmem_limit_bytes=VMEM_LIMIT)


def _ada_kernel(cc_ref, w_ref, b_ref, o_ref):
    s = cc_ref[...]
    s = s * jax.nn.sigmoid(s)
    o_ref[...] = jnp.dot(s, w_ref[0], precision=lax.Precision.HIGHEST,
                         preferred_element_type=F32) + b_ref[0]


def _ada(cc, ada_w, ada_b, layer):
    _, d, n = ada_w.shape
    bn = 1024
    return pl.pallas_call(
        _ada_kernel,
        grid=(n // bn,),
        in_specs=[
            pl.BlockSpec((MOD_ROWS, d), lambda j: (0, 0)),
            pl.BlockSpec((1, d, bn), lambda j: (layer, 0, j)),
            pl.BlockSpec((1, 1, bn), lambda j: (layer, 0, j)),
        ],
        out_specs=pl.BlockSpec((MOD_ROWS, bn), lambda j: (0, j)),
        out_shape=jax.ShapeDtypeStruct((MOD_ROWS, n), F32),
        compiler_params=_params(("arbitrary",)),
        name="ada",
    )(cc, ada_w, ada_b.reshape(ada_b.shape[0], 1, n))


def _rope_tables(n):
    pos = jnp.arange(n, dtype=jnp.int32)
    row = (pos // GRID_W).astype(F32)
    col = (pos % GRID_W).astype(F32)
    n_freq = HEAD_DIM // 4
    inv = ROPE_THETA ** (-jnp.arange(n_freq, dtype=F32) / n_freq)
    ang_r = row[:, None] * inv
    ang_c = col[:, None] * inv
    cos = jnp.concatenate([jnp.cos(ang_r), jnp.cos(ang_r), jnp.cos(ang_c), jnp.cos(ang_c)], axis=-1)
    sin = jnp.concatenate([-jnp.sin(ang_r), jnp.sin(ang_r), -jnp.sin(ang_c), jnp.sin(ang_c)], axis=-1)
    cos, sin = jnp.tile(cos, (1, 2)), jnp.tile(sin, (1, 2))
    return cos, sin, cos.T, sin.T


def _rmsnorm(x, g):
    ms = jnp.mean(x * x, axis=-1, keepdims=True)
    return x * lax.rsqrt(ms + EPS) * g


def _inproj_kernel(*refs, rope, d_pool, d_attn, q_scale):
    if rope:
        (x_ref, sh_ref, sc_ref, g_ref, wuk_ref, wqv_ref, cos_ref, sin_ref, cost_ref, sint_ref,
         u_ref, qt_ref, k_ref, vt_ref) = refs
    else:
        x_ref, sh_ref, sc_ref, g_ref, wuk_ref, wqv_ref, u_ref, qt_ref, k_ref, vt_ref = refs
    h = (_rmsnorm(x_ref[0], g_ref[...]) * (1.0 + sc_ref[0]) + sh_ref[0]).astype(BF16)
    p = jnp.dot(h, wuk_ref[...], preferred_element_type=F32)
    pt = lax.dot_general(wqv_ref[...], h, (((1,), (1,)), ((), ())), preferred_element_type=F32)
    u_ref[0] = p[:, :d_pool]
    vt_ref[0] = pt[d_attn:].astype(BF16)
    half = HEAD_DIM // 4
    if rope:
        cos, sin = cos_ref[...], sin_ref[...]
        cost, sint = cost_ref[...], sint_ref[...]
        lane = lax.broadcasted_iota(jnp.int32, cos.shape, 1)
        first_half = (lane % (2 * half)) < half
    for h0 in range(0, d_attn, HEAD_PAIR):
        c = p[:, d_pool + h0: d_pool + h0 + HEAD_PAIR]
        ct = pt[h0:h0 + HEAD_PAIR]
        if rope:
            partner = jnp.where(first_half, pltpu.roll(c, HEAD_PAIR - half, 1), pltpu.roll(c, half, 1))
            c = c * cos + partner * sin
            swapped = []
            for r0 in range(0, HEAD_PAIR, 2 * half):
                swapped += [ct[r0 + half:r0 + 2 * half], ct[r0:r0 + half]]
            ct = ct * cost + jnp.concatenate(swapped, axis=0) * sint
        k_ref[0, :, h0:h0 + HEAD_PAIR] = c.astype(BF16)
        qt_ref[0, h0:h0 + HEAD_PAIR, :] = (ct * q_scale).astype(BF16)


def _inproj(x, shift, scale, g, w_uk, w_qvt, tables):
    b, l, d = x.shape
    d_pool = d // 2
    d_attn = w_qvt.shape[0] // 2
    t = min(512, l)
    rope = tables is not None
    row = lambda bi, i: (bi, i, 0)
    col = lambda bi, i: (bi, 0, i)
    per_batch = lambda bi, i: (bi, 0, 0)
    const = lambda bi, i: (0, 0)
    in_specs = [
        pl.BlockSpec((1, t, d), row),
        pl.BlockSpec((1, 1, d), per_batch),
        pl.BlockSpec((1, 1, d), per_batch),
        pl.BlockSpec((1, d), const),
        pl.BlockSpec(w_uk.shape, const),
        pl.BlockSpec(w_qvt.shape, const),
    ]
    args = [x, shift, scale, g.reshape(1, d), w_uk, w_qvt]
    if rope:
        in_specs += [pl.BlockSpec((t, HEAD_PAIR), lambda bi, i: (i, 0))] * 2
        in_specs += [pl.BlockSpec((HEAD_PAIR, t), lambda bi, i: (0, i))] * 2
        args += list(tables)
    return pl.pallas_call(
        functools.partial(_inproj_kernel, rope=rope, d_pool=d_pool, d_attn=d_attn,
                          q_scale=HEAD_DIM ** -0.5 * math.log2(math.e)),
        grid=(b, l // t),
        in_specs=in_specs,
        out_specs=[pl.BlockSpec((1, t, d_pool), row), pl.BlockSpec((1, d_attn, t), col),
                   pl.BlockSpec((1, t, d_attn), row), pl.BlockSpec((1, d_attn, t), col)],
        out_shape=[jax.ShapeDtypeStruct((b, l, d_pool), F32), jax.ShapeDtypeStruct((b, d_attn, l), BF16),
                   jax.ShapeDtypeStruct((b, l, d_attn), BF16), jax.ShapeDtypeStruct((b, d_attn, l), BF16)],
        compiler_params=_params(("arbitrary", "arbitrary")),
        name="inproj",
    )(*args)


def _attn_kernel(*refs, tq, tk, cb, n_chunks, group, has_ctx, lam_init):
    if has_ctx:
        (lam_ref, qt_ref, k_ref, vt_ref, kc_ref, vct_ref, g_ref, o_ref,
         qs_scr, m_scr, l_scr, acc_scr, s_scr) = refs
    else:
        lam_ref, qt_ref, k_ref, vt_ref, g_ref, o_ref, qs_scr, m_scr, l_scr, acc_scr, s_scr = refs
    qt = qt_ref[0]
    sub = lax.broadcasted_iota(jnp.int32, qt.shape, 0)
    zero = jnp.zeros_like(qt)
    qs_scr[:, :tq] = jnp.where(sub < HEAD_DIM, qt, zero)
    qs_scr[:, tq:] = jnp.where(sub >= HEAD_DIM, qt, zero)
    m_scr[...] = jnp.full(m_scr.shape, -jnp.inf, F32)
    l_scr[...] = jnp.zeros(l_scr.shape, F32)
    acc_scr[...] = jnp.zeros(acc_scr.shape, F32)

    n_blocks = 2 * tq // cb
    blocks = [slice(c * cb, (c + 1) * cb) for c in range(n_blocks)]

    def scores(kj, c):
        return jnp.dot(kj, qs_scr[:, blocks[c]], preferred_element_type=F32)

    def with_ones(vtj):
        return jnp.concatenate([vtj, jnp.ones((ONES_ROWS, vtj.shape[1]), BF16)], axis=0)

    def span(j):
        return pl.ds(j * tk if isinstance(j, int) else pl.multiple_of(j * tk, tk), tk)

    def keys(j):
        return k_ref[0, span(j), :]

    def values_t(j):
        return vt_ref[0, :, span(j)]

    def online_chunk(kj, vtj):
        vta = with_ones(vtj)
        s_next = scores(kj, 0)
        for c, cols in enumerate(blocks):
            s = s_next
            if c + 1 < n_blocks:
                s_next = scores(kj, c + 1)
            m_old = m_scr[:, cols]
            m_new = jnp.maximum(m_old, jnp.max(s, axis=0, keepdims=True))
            alpha = jnp.exp2(m_old - m_new)
            p = jnp.exp2(s - m_new).astype(BF16)
            r = jnp.dot(vta, p, preferred_element_type=F32)
            l_scr[:, cols] = alpha * l_scr[:, cols] + r[HEAD_PAIR:HEAD_PAIR + 1]
            acc_scr[:, cols] = alpha * acc_scr[:, cols] + r[:HEAD_PAIR]
            m_scr[:, cols] = m_new

    def group_fast(j0, j_next):
        items = [(c, jj) for c in range(n_blocks) for jj in range(group)]
        s_next = s_scr[...]
        sums, excess = [], None
        for idx, (c, jj) in enumerate(items):
            s = s_next
            if idx + 1 < len(items):
                c2, jj2 = items[idx + 1]
                s_next = scores(keys(j0 + jj2), c2)
            elif j_next is not None:
                s_scr[...] = scores(keys(j_next), 0)
            m_ref = m_scr[:, blocks[c]]
            top = jnp.max(s, axis=0, keepdims=True)
            p = jnp.exp2(s - m_ref).astype(BF16)
            r = jnp.dot(with_ones(values_t(j0 + jj)), p, preferred_element_type=F32)
            r_c, top_c = (r, top) if jj == 0 else (r_c + r, jnp.maximum(top_c, top))
            if jj == group - 1:
                sums.append(r_c)
                over = jnp.max(top_c - m_ref)
                excess = over if excess is None else jnp.maximum(excess, over)
        ok = excess <= MAX_EXCESS

        @pl.when(ok)
        def _():
            for cols, r_c in zip(blocks, sums):
                l_scr[:, cols] += r_c[HEAD_PAIR:HEAD_PAIR + 1]
                acc_scr[:, cols] += r_c[:HEAD_PAIR]

        return ok

    def group_step(j0, j_next):
        ok = group_fast(j0, j_next)

        @pl.when(jnp.logical_not(ok))
        def _():
            def redo(jj, carry):
                online_chunk(keys(j0 + jj), values_t(j0 + jj))
                return carry

            lax.fori_loop(0, group, redo, 0)

    n_groups = n_chunks // group
    if has_ctx:
        online_chunk(kc_ref[0], vct_ref[0])
    s_scr[...] = scores(keys(0), 0)
    if n_groups > 1:
        def body(g, carry):
            group_step(g * group, (g + 1) * group)
            return carry

        lax.fori_loop(0, n_groups - 1, body, 0)
    group_step((n_groups - 1) * group, None)

    o = acc_scr[...] / l_scr[...]
    lv = lam_ref[...]
    lam = (jnp.exp(jnp.sum(lv[0:1] * lv[1:2], axis=1, keepdims=True))
           - jnp.exp(jnp.sum(lv[2:3] * lv[3:4], axis=1, keepdims=True)) + lam_init)
    a = (o[:, :tq] - lam * o[:, tq:]).T
    o_ref[0] = (_rmsnorm(a, g_ref[...]) * (1.0 - lam_init)).astype(BF16)


def _attention(qt, k, vt, kc, vct, lam4, subln_g, lam_init):
    b, d_attn, lq = qt.shape
    lk = k.shape[1]
    n_heads = d_attn // HEAD_PAIR
    tq = min(1024, lq)
    tk = min(512, lk)
    cb = min(512, 2 * tq)
    has_ctx = kc is not None
    qmap = lambda bi, h, i: (bi, h, i)
    kmap = lambda bi, h, i: (bi, 0, h)
    vmap = lambda bi, h, i: (bi, h, 0)
    const = lambda bi, h, i: (0, 0)
    in_specs = [
        pl.BlockSpec(lam4.shape, const),
        pl.BlockSpec((1, HEAD_PAIR, tq), qmap),
        pl.BlockSpec((1, lk, HEAD_PAIR), kmap),
        pl.BlockSpec((1, HEAD_PAIR, lk), vmap),
    ]
    args = [lam4, qt, k, vt]
    if has_ctx:
        lc = kc.shape[1]
        in_specs += [pl.BlockSpec((1, lc, HEAD_PAIR), kmap), pl.BlockSpec((1, HEAD_PAIR, lc), vmap)]
        args += [kc, vct]
    in_specs.append(pl.BlockSpec((1, HEAD_PAIR), const))
    args.append(subln_g.reshape(1, HEAD_PAIR))
    return pl.pallas_call(
        functools.partial(_attn_kernel, tq=tq, tk=tk, cb=cb, n_chunks=lk // tk,
                          group=math.gcd(KEY_GROUP, lk // tk), has_ctx=has_ctx,
                          lam_init=lam_init),
        grid=(b, n_heads, lq // tq),
        in_specs=in_specs,
        out_specs=pl.BlockSpec((1, tq, HEAD_PAIR), lambda bi, h, i: (bi, i, h)),
        out_shape=jax.ShapeDtypeStruct((b, lq, d_attn), BF16),
        scratch_shapes=[
            pltpu.VMEM((HEAD_PAIR, 2 * tq), BF16),
            pltpu.VMEM((1, 2 * tq), F32),
            pltpu.VMEM((1, 2 * tq), F32),
            pltpu.VMEM((HEAD_PAIR, 2 * tq), F32),
            pltpu.VMEM((tk, cb), F32),
        ],
        compiler_params=_params(("arbitrary", "arbitrary", "arbitrary")),
        name="diff_attn",
    )(*args)


def _route(logits):
    lane = lax.broadcasted_iota(jnp.int32, logits.shape, 1)
    big = jnp.int32(ROUTER_LANES)
    neg = jnp.float32(-jnp.inf)
    lc = jnp.where(lane < N_GROUPS, logits, neg)
    mc = jnp.max(lc, axis=1, keepdims=True)
    p_group = 1.0 / jnp.sum(jnp.exp(lc - mc), axis=1, keepdims=True)
    g_idx = jnp.min(jnp.where(lc == mc, lane, big), axis=1, keepdims=True)
    lo = N_GROUPS + EXPERTS_PER_GROUP * g_idx
    lf = jnp.where((lane >= lo) & (lane < lo + EXPERTS_PER_GROUP), logits, neg)
    v1 = jnp.max(lf, axis=1, keepdims=True)
    i1 = jnp.min(jnp.where(lf == v1, lane, big), axis=1, keepdims=True)
    lf2 = jnp.where(lane == i1, neg, lf)
    v2 = jnp.max(lf2, axis=1, keepdims=True)
    i2 = jnp.min(jnp.where(lf2 == v2, lane, big), axis=1, keepdims=True)
    e2 = jnp.exp(v2 - v1)
    w1 = p_group / (1.0 + e2)
    w2 = p_group * e2 / (1.0 + e2)
    return lane, i1, i2, w1, w2


def _split_bf16(x):
    hi = x.astype(BF16)
    return hi, (x - hi.astype(F32)).astype(BF16)


def _sorted_positions(lane, i1, i2):
    t = lane.shape[0]
    oh1 = lane == i1
    oh2 = lane == i2
    oh = jnp.concatenate([oh1, oh2], axis=1).astype(BF16)
    r = lax.broadcasted_iota(jnp.int32, (t, t), 0)
    c = lax.broadcasted_iota(jnp.int32, (t, t), 1)
    before = (c < r).astype(BF16)
    rank = jnp.dot(before, oh, preferred_element_type=F32)
    cnt1 = jnp.sum(oh1.astype(F32), axis=0, keepdims=True)
    cnt2 = jnp.sum(oh2.astype(F32), axis=0, keepdims=True)
    run = jnp.ceil((cnt1 + cnt2) * (1.0 / SLOT_ALIGN)) * SLOT_ALIGN
    lr = lax.broadcasted_iota(jnp.int32, (ROUTER_LANES, ROUTER_LANES), 0)
    lc = lax.broadcasted_iota(jnp.int32, (ROUTER_LANES, ROUTER_LANES), 1)
    start = jnp.dot(jnp.broadcast_to(run, (8, ROUTER_LANES)).astype(BF16), (lr < lc).astype(BF16),
                    preferred_element_type=F32)[0:1]
    pos1 = jnp.sum(jnp.where(oh1, start + rank[:, :ROUTER_LANES], 0.0), axis=1, keepdims=True)
    pos2 = jnp.sum(jnp.where(oh2, start + cnt1 + rank[:, ROUTER_LANES:], 0.0), axis=1, keepdims=True)
    return pos1, pos2, run


def _merge_kernel(u_ref, up_ref, un_ref, a_ref, x_ref, pw_ref, ps_ref, wo_ref, g1_ref, n2_ref, sh_ref,
                  sc_ref, wr_ref, rb_ref, xo_ref, xa_ref, tm_ref, lm_ref, run_ref, ext_scr, *, t, seq_len,
                  d_pool):
    i = pl.program_id(1)
    n_tiles = pl.num_programs(1)
    u = u_ref[0]
    ext_scr[0:HALO] = jnp.where(i > 0, up_ref[0], 0.0)
    ext_scr[HALO:HALO + t] = u
    ext_scr[HALO + t:] = jnp.where(i < n_tiles - 1, un_ref[0], 0.0)

    gc = d_pool // len(POOL_WINDOWS)
    d = x_ref.shape[2]
    tr = t // MERGE_PARTS
    parts = [slice(r0, r0 + tr) for r0 in range(0, t, tr)]

    def pool(rows):
        pos = i * t + rows.start + lax.broadcasted_iota(jnp.int32, (tr, 1), 0)
        pooled = []
        for gi, w in enumerate(POOL_WINDOWS):
            left = w // 2
            right = w - 1 - left
            cols = slice(gi * gc, (gi + 1) * gc)
            s = ext_scr[pl.ds(rows.start + HALO - left, tr), cols]
            for o in range(-left + 1, right + 1):
                s = s + ext_scr[pl.ds(rows.start + HALO + o, tr), cols]
            cnt = (jnp.minimum(pos + right + 1, seq_len) - jnp.maximum(pos - left, 0)).astype(F32)
            pooled.append((s / cnt - u_ref[0, rows, cols]).astype(BF16))
        return pooled

    def project(rows, pooled):
        mix = jnp.dot(a_ref[0, rows], wo_ref[d_pool:, :], preferred_element_type=F32)
        for gi, pg in enumerate(pooled):
            cols = slice(gi * gc, (gi + 1) * gc)
            mixed = jnp.dot(pg, pw_ref[gi], preferred_element_type=F32) * ps_ref[:, cols]
            mix = mix + jnp.dot(mixed.astype(BF16), wo_ref[cols, :], preferred_element_type=F32)
        return mix

    def renorm(rows, mix):
        x_new = x_ref[0, rows] + g1_ref[0] * mix
        xo_ref[0, rows] = x_new
        h2 = _rmsnorm(x_new, n2_ref[...]) * (1.0 + sc_ref[0]) + sh_ref[0]
        h_hi, h_lo = _split_bf16(h2)
        xa_ref[0, rows, :d] = h_hi
        return h_hi, h_lo

    def router(h_hi, h_lo):
        return (jnp.dot(h_hi, wr_ref[0], preferred_element_type=F32)
                + jnp.dot(h_lo, wr_ref[0], preferred_element_type=F32)
                + jnp.dot(h_hi, wr_ref[1], preferred_element_type=F32))

    n_p = len(parts)
    pooled = [None] * n_p
    mix = [None] * n_p
    split = [None] * n_p
    logits = [None] * n_p
    pooled[0] = pool(parts[0])
    for k in range(n_p):
        mix[k] = project(parts[k], pooled[k])
        if k + 1 < n_p:
            pooled[k + 1] = pool(parts[k + 1])
        if k >= 1:
            logits[k - 1] = router(*split[k - 1])
        split[k] = renorm(parts[k], mix[k])
    logits[n_p - 1] = router(*split[n_p - 1])
    logits = jnp.concatenate(logits, axis=0) + rb_ref[...]
    lane, i1, i2, w1, w2 = _route(logits)
    pos1, pos2, run = _sorted_positions(lane, i1, i2)
    w1_hi = w1.astype(BF16).astype(F32)
    w2_hi = w2.astype(BF16).astype(F32)
    aux = jnp.zeros(lane.shape, F32)
    for k, val in enumerate((w1_hi, w1 - w1_hi, w2_hi, w2 - w2_hi, (i1 - N_GROUPS).astype(F32))):
        aux = jnp.where(lane == k, val, aux)
    xa_ref[0, :, d:] = aux.astype(BF16)
    info = jnp.where(lane == 0, pos1, jnp.where(lane == 1, pos2, 0.0))
    tm_ref[0] = info
    lm_ref[0] = info.T[:8]
    run_ref[0, 0] = run


def _merge(u, attn, x, pool_w_bf, pool_scale, w_out_bf, g1, norm2_g, sh2, sc2, wr, rb):
    b, l, d = x.shape
    d_pool = u.shape[2]
    t = min(MOE_TOKEN_TILE, l)
    hb = t // HALO
    n_halo = l // HALO
    row = lambda bi, i: (bi, i, 0)
    per_batch = lambda bi, i: (bi, 0, 0)
    const2 = lambda bi, i: (0, 0)
    const3 = lambda bi, i: (0, 0, 0)
    return pl.pallas_call(
        functools.partial(_merge_kernel, t=t, seq_len=l, d_pool=d_pool),
        grid=(b, l // t),
        in_specs=[
            pl.BlockSpec((1, t, d_pool), row),
            pl.BlockSpec((1, HALO, d_pool), lambda bi, i: (bi, jnp.maximum(i * hb - 1, 0), 0)),
            pl.BlockSpec((1, HALO, d_pool), lambda bi, i: (bi, jnp.minimum((i + 1) * hb, n_halo - 1), 0)),
            pl.BlockSpec((1, t, attn.shape[2]), row),
            pl.BlockSpec((1, t, d), row),
            pl.BlockSpec(pool_w_bf.shape, const3),
            pl.BlockSpec((1, d_pool), const2),
            pl.BlockSpec(w_out_bf.shape, const2),
            pl.BlockSpec((1, 1, d), per_batch),
            pl.BlockSpec((1, d), const2),
            pl.BlockSpec((1, 1, d), per_batch),
            pl.BlockSpec((1, 1, d), per_batch),
            pl.BlockSpec(wr.shape, const3),
            pl.BlockSpec((1, ROUTER_LANES), const2),
        ],
        out_specs=[
            pl.BlockSpec((1, t, d), row),
            pl.BlockSpec((1, t, d + ROUTER_LANES), row),
            pl.BlockSpec((1, t, ROUTER_LANES), row),
            pl.BlockSpec((1, 8, t), lambda bi, i: (bi, 0, i)),
            pl.BlockSpec((1, 1, 1, ROUTER_LANES), lambda bi, i: (bi, i, 0, 0)),
        ],
        out_shape=[
            jax.ShapeDtypeStruct((b, l, d), F32),
            jax.ShapeDtypeStruct((b, l, d + ROUTER_LANES), BF16),
            jax.ShapeDtypeStruct((b, l, ROUTER_LANES), F32),
            jax.ShapeDtypeStruct((b, 8, l), F32),
            jax.ShapeDtypeStruct((b, l // t, 1, ROUTER_LANES), F32),
        ],
        scratch_shapes=[pltpu.VMEM((t + 2 * HALO, d_pool), F32)],
        compiler_params=_params(("arbitrary", "arbitrary")),
        name="merge",
    )(u, u, u, attn, x, pool_w_bf, pool_scale.reshape(1, d_pool), w_out_bf, g1, norm2_g.reshape(1, d),
      sh2, sc2, wr, rb)


def _moe_plan(runs, t, tm):
    counts = runs[:, N_GROUPS:N_GROUPS + N_EXPERTS].astype(jnp.int32)
    n_tiles = counts.shape[0]
    total = jnp.sum(counts, axis=0)
    region = (total + tm - 1) // tm * tm
    region_end = jnp.cumsum(region)
    start = region_end - region
    dst = start[None, :] + jnp.cumsum(counts, axis=0) - counts
    off = jnp.cumsum(counts, axis=1) - counts
    s_max = 2 * n_tiles * t + n_tiles * N_EXPERTS * SLOT_ALIGN + N_EXPERTS * tm
    n_row_tiles = -(-s_max // tm)
    first_row = jnp.arange(n_row_tiles, dtype=jnp.int32) * tm
    valid = first_row < region_end[-1]
    last_valid = jnp.maximum(region_end[-1] // tm - 1, 0)
    block = jnp.where(valid, jnp.arange(n_row_tiles, dtype=jnp.int32), last_valid)
    expert = jnp.sum((region_end[None, :] <= (block * tm)[:, None]).astype(jnp.int32), axis=1)
    expert = jnp.minimum(expert, N_EXPERTS - 1)
    gap_dst = jnp.concatenate([start + total, region_end[-1:]])
    gap_n = jnp.concatenate([region - total, n_row_tiles * tm - region_end[-1:]]) // SLOT_ALIGN
    plan = dict(dst=dst.reshape(-1), off=off.reshape(-1), n=(counts // SLOT_ALIGN).reshape(-1),
                gap_dst=gap_dst, gap_n=gap_n, expert=expert, block=block, valid=valid.astype(jnp.int32))
    return plan, n_row_tiles


def _start_runs(n_ref, tile, src_first, dst_first, src_rows, dst_rows, sem, n_runs=N_EXPERTS):
    for e in range(n_runs):
        n = n_ref[tile * n_runs + e]
        s0, d0 = src_first(e), dst_first(e)

        def issue(j, carry, s0=s0, d0=d0):
            step = j * SLOT_ALIGN
            pltpu.make_async_copy(src_rows(s0 + step), dst_rows(d0 + step), sem).start()
            return carry

        lax.fori_loop(0, n, issue, 0)


def _wait_runs(n_ref, tile, src, dst, sem, n_runs=N_EXPERTS):
    total = 0
    for e in range(n_runs):
        total = total + n_ref[tile * n_runs + e]

    def drain(j, carry):
        pltpu.make_async_copy(src, dst, sem).wait()
        return carry

    lax.fori_loop(0, total, drain, 0)


def _span(first):
    return pl.ds(pl.multiple_of(first, SLOT_ALIGN), SLOT_ALIGN)


def _dispatch_kernel(dst_ref, off_ref, n_ref, sdst_ref, sn_ref, xa_ref, lm_ref, xs_hbm, buf, zbuf, sems, *, p):
    i = pl.program_id(0)
    last = pl.num_programs(0) - 1
    slot = i % 2
    t = xa_ref.shape[0]
    one_src, one_dst = buf.at[0, pl.ds(0, SLOT_ALIGN)], xs_hbm.at[pl.ds(0, SLOT_ALIGN)]

    @pl.when(i >= 2)
    def _():
        _wait_runs(n_ref, i - 2, one_src, one_dst, sems.at[slot])

    pos1 = lm_ref[0, 0:1, :].astype(jnp.int32)
    pos2 = lm_ref[0, 1:2, :].astype(jnp.int32)
    xa = xa_ref[...]
    for r0 in range(0, p, PERM_CHUNK):
        row = r0 + lax.broadcasted_iota(jnp.int32, (PERM_CHUNK, t), 0)
        perm = ((row == pos1) | (row == pos2)).astype(BF16)
        buf[slot, r0:r0 + PERM_CHUNK] = jnp.dot(perm, xa, preferred_element_type=F32).astype(BF16)
    _start_runs(n_ref, i, lambda e: off_ref[i * N_EXPERTS + e], lambda e: dst_ref[i * N_EXPERTS + e],
                lambda first: buf.at[slot, _span(first)], lambda first: xs_hbm.at[_span(first)], sems.at[slot])

    @pl.when(i == last)
    def _():
        @pl.when(i >= 1)
        def _():
            _wait_runs(n_ref, i - 1, one_src, one_dst, sems.at[1 - slot])

        _wait_runs(n_ref, i, one_src, one_dst, sems.at[slot])
        zbuf[...] = jnp.zeros(zbuf.shape, BF16)
        _start_runs(sn_ref, 0, lambda e: 0, lambda e: sdst_ref[e],
                    lambda first: zbuf, lambda first: xs_hbm.at[_span(first)], sems.at[slot], N_EXPERTS + 1)
        _wait_runs(sn_ref, 0, one_src, one_dst, sems.at[slot], N_EXPERTS + 1)


def _expert_kernel(e_ref, blk_ref, valid_ref, xs_ref, wg_ref, wu_ref, wd_ref, ys_ref, wg_bf, wu_bf, wd_bf, *, d):
    r = pl.program_id(0)
    e = e_ref[r]

    @pl.when(jnp.logical_or(r == 0, e != e_ref[jnp.maximum(r - 1, 0)]))
    def _():
        wg_bf[...] = wg_ref[0, 0, 0].astype(BF16)
        wu_bf[...] = wu_ref[0, 0, 0].astype(BF16)
        wd_bf[...] = wd_ref[0, 0, 0].astype(BF16)

    @pl.when(valid_ref[r] == 1)
    def _():
        x = xs_ref[:, :d]
        aux = xs_ref[:, d:].astype(F32)
        is_first = aux[:, 4:5] == e.astype(F32)
        gate = jnp.where(is_first, aux[:, 0:1] + aux[:, 1:2], aux[:, 2:3] + aux[:, 3:4])
        a = jnp.dot(x, wg_bf[...], preferred_element_type=F32)
        bu = jnp.dot(x, wu_bf[...], preferred_element_type=F32)
        hid = (a * jax.nn.sigmoid(a)) * bu * gate
        ys_ref[...] = jnp.dot(hid.astype(BF16), wd_bf[...], preferred_element_type=F32).astype(BF16)

    @pl.when(valid_ref[r] == 0)
    def _():
        ys_ref[...] = jnp.zeros(ys_ref.shape, BF16)


def _combine_kernel(*refs, final_norm):
    if final_norm:
        dst_ref, off_ref, n_ref, ys_hbm, tm_ref, x_ref, g2_ref, fg_ref, o_ref, buf, sems = refs
    else:
        dst_ref, off_ref, n_ref, ys_hbm, tm_ref, x_ref, g2_ref, o_ref, buf, sems = refs
    i = pl.program_id(0)
    slot = i % 2
    t = x_ref.shape[0]
    p = buf.shape[1]

    def fetch(tile, s):
        buf[s] = jnp.zeros(buf.shape[1:], BF16)
        _start_runs(n_ref, tile, lambda e: dst_ref[tile * N_EXPERTS + e], lambda e: off_ref[tile * N_EXPERTS + e],
                    lambda first: ys_hbm.at[_span(first)], lambda first: buf.at[s, _span(first)], sems.at[s])

    @pl.when(i == 0)
    def _():
        fetch(i, slot)

    @pl.when(i + 1 < pl.num_programs(0))
    def _():
        fetch(i + 1, 1 - slot)

    _wait_runs(n_ref, i, ys_hbm.at[pl.ds(0, SLOT_ALIGN)], buf.at[0, pl.ds(0, SLOT_ALIGN)], sems.at[slot])
    pos1 = tm_ref[:, 0:1].astype(jnp.int32)
    pos2 = tm_ref[:, 1:2].astype(jnp.int32)
    f = None
    for r0 in range(0, p, PERM_CHUNK):
        col = r0 + lax.broadcasted_iota(jnp.int32, (t, PERM_CHUNK), 1)
        perm_t = ((col == pos1) | (col == pos2)).astype(BF16)
        part = jnp.dot(perm_t, buf[slot, r0:r0 + PERM_CHUNK], preferred_element_type=F32)
        f = part if f is None else f + part
    y = x_ref[...] + g2_ref[0] * f
    if final_norm:
        y = _rmsnorm(y, fg_ref[...])
    o_ref[...] = y


def _moe(x, xa, info_tm, info_lm, runs, g2, w_gate, w_up, w_down, layer, final_g):
    b, l, d = x.shape
    da = xa.shape[2]
    f = w_gate.shape[-1]
    t = min(MOE_TOKEN_TILE, l)
    tiles_per_batch = l // t
    n_tiles = b * tiles_per_batch
    p = 2 * t + N_EXPERTS * SLOT_ALIGN
    n = b * l
    tm = MOE_ROW_TILE if 2 * n >= N_EXPERTS * MOE_ROW_TILE else MOE_ROW_TILE // 4
    plan, n_row_tiles = _moe_plan(runs.reshape(n_tiles, ROUTER_LANES), t, tm)
    s_rows = n_row_tiles * tm

    xs = pl.pallas_call(
        functools.partial(_dispatch_kernel, p=p),
        grid_spec=pltpu.PrefetchScalarGridSpec(
            num_scalar_prefetch=5,
            grid=(n_tiles,),
            in_specs=[
                pl.BlockSpec((t, da), lambda i, *_: (i, 0)),
                pl.BlockSpec((1, 8, t), lambda i, *_: (i // tiles_per_batch, 0, i % tiles_per_batch)),
            ],
            out_specs=pl.BlockSpec(memory_space=pl.ANY),
            scratch_shapes=[pltpu.VMEM((2, p, da), BF16), pltpu.VMEM((SLOT_ALIGN, da), BF16),
                            pltpu.SemaphoreType.DMA((2,))],
        ),
        out_shape=jax.ShapeDtypeStruct((s_rows, da), BF16),
        compiler_params=_params(("arbitrary",)),
        name="moe_dispatch",
    )(plan["dst"], plan["off"], plan["n"], plan["gap_dst"], plan["gap_n"], xa.reshape(n, da), info_lm)

    w_map = lambda r, e_ref, blk_ref, valid_ref: (
        layer, e_ref[r] // EXPERTS_PER_GROUP, e_ref[r] % EXPERTS_PER_GROUP, 0, 0)
    row_map = lambda r, e_ref, blk_ref, valid_ref: (blk_ref[r], 0)
    ys = pl.pallas_call(
        functools.partial(_expert_kernel, d=d),
        grid_spec=pltpu.PrefetchScalarGridSpec(
            num_scalar_prefetch=3,
            grid=(n_row_tiles,),
            in_specs=[
                pl.BlockSpec((tm, da), row_map),
                pl.BlockSpec((1, 1, 1, d, f), w_map),
                pl.BlockSpec((1, 1, 1, d, f), w_map),
                pl.BlockSpec((1, 1, 1, f, d), w_map),
            ],
            out_specs=pl.BlockSpec((tm, d), lambda r, *_: (r, 0)),
            scratch_shapes=[pltpu.VMEM((d, f), BF16), pltpu.VMEM((d, f), BF16), pltpu.VMEM((f, d), BF16)],
        ),
        out_shape=jax.ShapeDtypeStruct((s_rows, d), BF16),
        compiler_params=_params(("arbitrary",)),
        name="moe_experts",
    )(plan["expert"], plan["block"], plan["valid"], xs, w_gate, w_up, w_down)

    final_norm = final_g is not None
    in_specs = [
        pl.BlockSpec(memory_space=pl.ANY),
        pl.BlockSpec((t, ROUTER_LANES), lambda i, *_: (i, 0)),
        pl.BlockSpec((t, d), lambda i, *_: (i, 0)),
        pl.BlockSpec((1, 1, d), lambda i, *_: (i // tiles_per_batch, 0, 0)),
    ]
    args = [ys, info_tm.reshape(n, ROUTER_LANES), x.reshape(n, d), g2]
    if final_norm:
        in_specs.append(pl.BlockSpec((1, d), lambda i, *_: (0, 0)))
        args.append(final_g.reshape(1, d))
    out = pl.pallas_call(
        functools.partial(_combine_kernel, final_norm=final_norm),
        grid_spec=pltpu.PrefetchScalarGridSpec(
            num_scalar_prefetch=3,
            grid=(n_tiles,),
            in_specs=in_specs,
            out_specs=pl.BlockSpec((t, d), lambda i, *_: (i, 0)),
            scratch_shapes=[pltpu.VMEM((2, p, d), BF16), pltpu.SemaphoreType.DMA((2,))],
        ),
        out_shape=jax.ShapeDtypeStruct((n, d), F32),
        compiler_params=_params(("arbitrary",)),
        name="moe_combine",
    )(plan["dst"], plan["off"], plan["n"], *args)
    return out.reshape(b, l, d)


def kernel(x, c, ctx, c_ctx, ada_w, ada_b, norm1_g, norm2_g, w_in, pool_w, pool_scale, lambda_q1, lambda_k1,
           lambda_q2, lambda_k2, subln_g, w_out, router_coarse_w, router_coarse_b, router_fine_w,
           router_fine_b, w_gate, w_up, w_down, final_g):
    b, l, d = x.shape
    depth = ada_w.shape[0]
    assert b + 1 <= MOD_ROWS and l % GRID_W == 0

    tables = _rope_tables(l)
    cc = jnp.zeros((MOD_ROWS, d), F32).at[:b].set(c).at[b].set(c_ctx)
    d_pool = d // 2
    d_attn = (w_in.shape[-1] - d_pool) // 3
    w_uk = jnp.concatenate([w_in[..., :d_pool], w_in[..., d_pool + d_attn:d_pool + 2 * d_attn]],
                           axis=-1).astype(BF16)
    w_qvt = jnp.swapaxes(jnp.concatenate([w_in[..., d_pool:d_pool + d_attn], w_in[..., d_pool + 2 * d_attn:]],
                                         axis=-1), 1, 2).astype(BF16)
    pool_w_bf = pool_w.astype(BF16)
    w_out_bf = w_out.astype(BF16)
    pad = ROUTER_LANES - N_GROUPS - N_EXPERTS
    wr = jnp.concatenate([router_coarse_w, router_fine_w, jnp.zeros((depth, d, pad), F32)], axis=-1)
    wr_hi = wr.astype(BF16)
    wr = jnp.stack([wr_hi, (wr - wr_hi.astype(F32)).astype(BF16)], axis=1)
    rb = jnp.concatenate([router_coarse_b, router_fine_b, jnp.zeros((depth, pad), F32)], axis=-1)

    xc = ctx
    for layer in range(depth):
        last = layer == depth - 1
        lam_init = 0.8 - 0.6 * math.exp(-0.3 * layer)
        mod = _ada(cc, ada_w, ada_b, layer)
        sh1, sc1, g1, sh2, sc2, g2 = jnp.split(mod[:b, None, :], 6, axis=-1)
        csh1, csc1, cg1, csh2, csc2, cg2 = jnp.split(
            jnp.broadcast_to(mod[b][None, None, :], (b, 1, mod.shape[1])), 6, axis=-1)
        lam4 = jnp.stack([lambda_q1[layer], lambda_k1[layer], lambda_q2[layer], lambda_k2[layer]])

        u, q, k, v = _inproj(x, sh1, sc1, norm1_g[layer], w_uk[layer], w_qvt[layer], tables)
        uc, qc, kc, vc = _inproj(xc, csh1, csc1, norm1_g[layer], w_uk[layer], w_qvt[layer], None)
        attn = _attention(q, k, v, kc, vc, lam4, subln_g[layer], lam_init)
        merge_w = (pool_w_bf[layer], pool_scale[layer], w_out_bf[layer])
        route_w = (wr[layer], rb[layer].reshape(1, ROUTER_LANES))
        x, *routed = _merge(u, attn, x, *merge_w, g1, norm2_g[layer], sh2, sc2, *route_w)
        if not last:
            attn_c = _attention(qc, kc, vc, None, None, lam4, subln_g[layer], lam_init)
            xc, *routed_c = _merge(uc, attn_c, xc, *merge_w, cg1, norm2_g[layer], csh2, csc2, *route_w)
            xc = _moe(xc, *routed_c, cg2, w_gate, w_up, w_down, layer, None)
        x = _moe(x, *routed, g2, w_gate, w_up, w_down, layer, final_g if last else None)
    return x
```

```python
import functools
import math

import jax
import jax.numpy as jnp
from jax import lax
from jax.experimental import pallas as pl
from jax.experimental.pallas import tpu as pltpu

F32 = jnp.float32
BF16 = jnp.bfloat16

GRID_W = 64
POOL_WINDOWS = (2, 4, 8, 16)
HEAD_DIM = 64
HEAD_PAIR = 2 * HEAD_DIM
ROPE_THETA = 10000.0
N_GROUPS = 4
EXPERTS_PER_GROUP = 8
N_EXPERTS = N_GROUPS * EXPERTS_PER_GROUP
EPS = 1e-6
HALO = 16
ROUTER_LANES = 128
MOD_ROWS = 8
ONES_ROWS = 16
MAX_EXCESS = 24.0
KEY_GROUP = 32
SLOT_ALIGN = 16
MOE_TOKEN_TILE = 512
MOE_ROW_TILE = 512
PERM_CHUNK = 256
MERGE_PARTS = 2

VMEM_LIMIT = 48 * 1024 * 1024


def _params(sem):
    return pltpu.CompilerParams(dimension_semantics=sem, vmem_limit_bytes=VMEM_LIMIT)


def _ada_kernel(cc_ref, w_ref, b_ref, o_ref):
    s = cc_ref[...]
    s = s * jax.nn.sigmoid(s)
    o_ref[...] = jnp.dot(s, w_ref[0], precision=lax.Precision.HIGHEST,
                         preferred_element_type=F32) + b_ref[0]


def _ada(cc, ada_w, ada_b, layer):
    _, d, n = ada_w.shape
    bn = 1024
    return pl.pallas_call(
        _ada_kernel,
        grid=(n // bn,),
        in_specs=[
            pl.BlockSpec((MOD_ROWS, d), lambda j: (0, 0)),
            pl.BlockSpec((1, d, bn), lambda j: (layer, 0, j)),
            pl.BlockSpec((1, 1, bn), lambda j: (layer, 0, j)),
        ],
        out_specs=pl.BlockSpec((MOD_ROWS, bn), lambda j: (0, j)),
        out_shape=jax.ShapeDtypeStruct((MOD_ROWS, n), F32),
        compiler_params=_params(("arbitrary",)),
        name="ada",
    )(cc, ada_w, ada_b.reshape(ada_b.shape[0], 1, n))


def _rope_tables(n):
    pos = jnp.arange(n, dtype=jnp.int32)
    row = (pos // GRID_W).astype(F32)
    col = (pos % GRID_W).astype(F32)
    n_freq = HEAD_DIM // 4
    inv = ROPE_THETA ** (-jnp.arange(n_freq, dtype=F32) / n_freq)
    ang_r = row[:, None] * inv
    ang_c = col[:, None] * inv
    cos = jnp.concatenate([jnp.cos(ang_r), jnp.cos(ang_r), jnp.cos(ang_c), jnp.cos(ang_c)], axis=-1)
    sin = jnp.concatenate([-jnp.sin(ang_r), jnp.sin(ang_r), -jnp.sin(ang_c), jnp.sin(ang_c)], axis=-1)
    cos, sin = jnp.tile(cos, (1, 2)), jnp.tile(sin, (1, 2))
    return cos, sin, cos.T, sin.T


def _rmsnorm(x, g):
    ms = jnp.mean(x * x, axis=-1, keepdims=True)
    return x * lax.rsqrt(ms + EPS) * g


def _inproj_kernel(*refs, rope, d_pool, d_attn, q_scale):
    if rope:
        (x_ref, sh_ref, sc_ref, g_ref, wuk_ref, wqv_ref, cos_ref, sin_ref, cost_ref, sint_ref,
         u_ref, qt_ref, k_ref, vt_ref) = refs
    else:
        x_ref, sh_ref, sc_ref, g_ref, wuk_ref, wqv_ref, u_ref, qt_ref, k_ref, vt_ref = refs
    h = (_rmsnorm(x_ref[0], g_ref[...]) * (1.0 + sc_ref[0]) + sh_ref[0]).astype(BF16)
    p = jnp.dot(h, wuk_ref[...], preferred_element_type=F32)
    pt = lax.dot_general(wqv_ref[...], h, (((1,), (1,)), ((), ())), preferred_element_type=F32)
    u_ref[0] = p[:, :d_pool]
    vt_ref[0] = pt[d_attn:].astype(BF16)
    half = HEAD_DIM // 4
    if rope:
        cos, sin = cos_ref[...], sin_ref[...]
        cost, sint = cost_ref[...], sint_ref[...]
        lane = lax.broadcasted_iota(jnp.int32, cos.shape, 1)
        first_half = (lane % (2 * half)) < half
    for h0 in range(0, d_attn, HEAD_PAIR):
        c = p[:, d_pool + h0: d_pool + h0 + HEAD_PAIR]
        ct = pt[h0:h0 + HEAD_PAIR]
        if rope:
            partner = jnp.where(first_half, pltpu.roll(c, HEAD_PAIR - half, 1), pltpu.roll(c, half, 1))
            c = c * cos + partner * sin
            swapped = []
            for r0 in range(0, HEAD_PAIR, 2 * half):
                swapped += [ct[r0 + half:r0 + 2 * half], ct[r0:r0 + half]]
            ct = ct * cost + jnp.concatenate(swapped, axis=0) * sint
        k_ref[0, :, h0:h0 + HEAD_PAIR] = c.astype(BF16)
        qt_ref[0, h0:h0 + HEAD_PAIR, :] = (ct * q_scale).astype(BF16)


def _inproj(x, shift, scale, g, w_uk, w_qvt, tables):
    b, l, d = x.shape
    d_pool = d // 2
    d_attn = w_qvt.shape[0] // 2
    t = min(512, l)
    rope = tables is not None
    row = lambda bi, i: (bi, i, 0)
    col = lambda bi, i: (bi, 0, i)
    per_batch = lambda bi, i: (bi, 0, 0)
    const = lambda bi, i: (0, 0)
    in_specs = [
        pl.BlockSpec((1, t, d), row),
        pl.BlockSpec((1, 1, d), per_batch),
        pl.BlockSpec((1, 1, d), per_batch),
        pl.BlockSpec((1, d), const),
        pl.BlockSpec(w_uk.shape, const),
        pl.BlockSpec(w_qvt.shape, const),
    ]
    args = [x, shift, scale, g.reshape(1, d), w_uk, w_qvt]
    if rope:
        in_specs += [pl.BlockSpec((t, HEAD_PAIR), lambda bi, i: (i, 0))] * 2
        in_specs += [pl.BlockSpec((HEAD_PAIR, t), lambda bi, i: (0, i))] * 2
        args += list(tables)
    return pl.pallas_call(
        functools.partial(_inproj_kernel, rope=rope, d_pool=d_pool, d_attn=d_attn,
                          q_scale=HEAD_DIM ** -0.5 * math.log2(math.e)),
        grid=(b, l // t),
        in_specs=in_specs,
        out_specs=[pl.BlockSpec((1, t, d_pool), row), pl.BlockSpec((1, d_attn, t), col),
                   pl.BlockSpec((1, t, d_attn), row), pl.BlockSpec((1, d_attn, t), col)],
        out_shape=[jax.ShapeDtypeStruct((b, l, d_pool), F32), jax.ShapeDtypeStruct((b, d_attn, l), BF16),
                   jax.ShapeDtypeStruct((b, l, d_attn), BF16), jax.ShapeDtypeStruct((b, d_attn, l), BF16)],
        compiler_params=_params(("arbitrary", "arbitrary")),
        name="inproj",
    )(*args)


def _attn_kernel(*refs, tq, tk, cb, n_chunks, group, has_ctx, lam_init):
    if has_ctx:
        (lam_ref, qt_ref, k_ref, vt_ref, kc_ref, vct_ref, g_ref, o_ref,
         qs_scr, m_scr, l_scr, acc_scr, s_scr) = refs
    else:
        lam_ref, qt_ref, k_ref, vt_ref, g_ref, o_ref, qs_scr, m_scr, l_scr, acc_scr, s_scr = refs
    qt = qt_ref[0]
    sub = lax.broadcasted_iota(jnp.int32, qt.shape, 0)
    zero = jnp.zeros_like(qt)
    qs_scr[:, :tq] = jnp.where(sub < HEAD_DIM, qt, zero)
    qs_scr[:, tq:] = jnp.where(sub >= HEAD_DIM, qt, zero)
    m_scr[...] = jnp.full(m_scr.shape, -jnp.inf, F32)
    l_scr[...] = jnp.zeros(l_scr.shape, F32)
    acc_scr[...] = jnp.zeros(acc_scr.shape, F32)

    n_blocks = 2 * tq // cb
    blocks = [slice(c * cb, (c + 1) * cb) for c in range(n_blocks)]

    def scores(kj, c):
        return jnp.dot(kj, qs_scr[:, blocks[c]], preferred_element_type=F32)

    def with_ones(vtj):
        return jnp.concatenate([vtj, jnp.ones((ONES_ROWS, vtj.shape[1]), BF16)], axis=0)

    def span(j):
        return pl.ds(j * tk if isinstance(j, int) else pl.multiple_of(j * tk, tk), tk)

    def keys(j):
        return k_ref[0, span(j), :]

    def values_t(j):
        return vt_ref[0, :, span(j)]

    def online_chunk(kj, vtj):
        vta = with_ones(vtj)
        s_next = scores(kj, 0)
        for c, cols in enumerate(blocks):
            s = s_next
            if c + 1 < n_blocks:
                s_next = scores(kj, c + 1)
            m_old = m_scr[:, cols]
            m_new = jnp.maximum(m_old, jnp.max(s, axis=0, keepdims=True))
            alpha = jnp.exp2(m_old - m_new)
            p = jnp.exp2(s - m_new).astype(BF16)
            r = jnp.dot(vta, p, preferred_element_type=F32)
            l_scr[:, cols] = alpha * l_scr[:, cols] + r[HEAD_PAIR:HEAD_PAIR + 1]
            acc_scr[:, cols] = alpha * acc_scr[:, cols] + r[:HEAD_PAIR]
            m_scr[:, cols] = m_new

    def group_fast(j0, j_next):
        items = [(c, jj) for c in range(n_blocks) for jj in range(group)]
        s_next = s_scr[...]
        sums, excess = [], None
        for idx, (c, jj) in enumerate(items):
            s = s_next
            if idx + 1 < len(items):
                c2, jj2 = items[idx + 1]
                s_next = scores(keys(j0 + jj2), c2)
            elif j_next is not None:
                s_scr[...] = scores(keys(j_next), 0)
            m_ref = m_scr[:, blocks[c]]
            top = jnp.max(s, axis=0, keepdims=True)
            p = jnp.exp2(s - m_ref).astype(BF16)
            r = jnp.dot(with_ones(values_t(j0 + jj)), p, preferred_element_type=F32)
            r_c, top_c = (r, top) if jj == 0 else (r_c + r, jnp.maximum(top_c, top))
            if jj == group - 1:
                sums.append(r_c)
                over = jnp.max(top_c - m_ref)
                excess = over if excess is None else jnp.maximum(excess, over)
        ok = excess <= MAX_EXCESS

        @pl.when(ok)
        def _():
            for cols, r_c in zip(blocks, sums):
                l_scr[:, cols] += r_c[HEAD_PAIR:HEAD_PAIR + 1]
                acc_scr[:, cols] += r_c[:HEAD_PAIR]

        return ok

    def group_step(j0, j_next):
        ok = group_fast(j0, j_next)

        @pl.when(jnp.logical_not(ok))
        def _():
            def redo(jj, carry):
                online_chunk(keys(j0 + jj), values_t(j0 + jj))
                return carry

            lax.fori_loop(0, group, redo, 0)

    n_groups = n_chunks // group
    if has_ctx:
        online_chunk(kc_ref[0], vct_ref[0])
    s_scr[...] = scores(keys(0), 0)
    if n_groups > 1:
        def body(g, carry):
            group_step(g * group, (g + 1) * group)
            return carry

        lax.fori_loop(0, n_groups - 1, body, 0)
    group_step((n_groups - 1) * group, None)

    o = acc_scr[...] / l_scr[...]
    lv = lam_ref[...]
    lam = (jnp.exp(jnp.sum(lv[0:1] * lv[1:2], axis=1, keepdims=True))
           - jnp.exp(jnp.sum(lv[2:3] * lv[3:4], axis=1, keepdims=True)) + lam_init)
    a = (o[:, :tq] - lam * o[:, tq:]).T
    o_ref[0] = (_rmsnorm(a, g_ref[...]) * (1.0 - lam_init)).astype(BF16)


def _attention(qt, k, vt, kc, vct, lam4, subln_g, lam_init):
    b, d_attn, lq = qt.shape
    lk = k.shape[1]
    n_heads = d_attn // HEAD_PAIR
    tq = min(1024, lq)
    tk = min(512, lk)
    cb = min(512, 2 * tq)
    has_ctx = kc is not None
    qmap = lambda bi, h, i: (bi, h, i)
    kmap = lambda bi, h, i: (bi, 0, h)
    vmap = lambda bi, h, i: (bi, h, 0)
    const = lambda bi, h, i: (0, 0)
    in_specs = [
        pl.BlockSpec(lam4.shape, const),
        pl.BlockSpec((1, HEAD_PAIR, tq), qmap),
        pl.BlockSpec((1, lk, HEAD_PAIR), kmap),
        pl.BlockSpec((1, HEAD_PAIR, lk), vmap),
    ]
    args = [lam4, qt, k, vt]
    if has_ctx:
        lc = kc.shape[1]
        in_specs += [pl.BlockSpec((1, lc, HEAD_PAIR), kmap), pl.BlockSpec((1, HEAD_PAIR, lc), vmap)]
        args += [kc, vct]
    in_specs.append(pl.BlockSpec((1, HEAD_PAIR), const))
    args.append(subln_g.reshape(1, HEAD_PAIR))
    return pl.pallas_call(
        functools.partial(_attn_kernel, tq=tq, tk=tk, cb=cb, n_chunks=lk // tk,
                          group=math.gcd(KEY_GROUP, lk // tk), has_ctx=has_ctx,
                          lam_init=lam_init),
        grid=(b, n_heads, lq // tq),
        in_specs=in_specs,
        out_specs=pl.BlockSpec((1, tq, HEAD_PAIR), lambda bi, h, i: (bi, i, h)),
        out_shape=jax.ShapeDtypeStruct((b, lq, d_attn), BF16),
        scratch_shapes=[
            pltpu.VMEM((HEAD_PAIR, 2 * tq), BF16),
            pltpu.VMEM((1, 2 * tq), F32),
            pltpu.VMEM((1, 2 * tq), F32),
            pltpu.VMEM((HEAD_PAIR, 2 * tq), F32),
            pltpu.VMEM((tk, cb), F32),
        ],
        compiler_params=_params(("arbitrary", "arbitrary", "arbitrary")),
        name="diff_attn",
    )(*args)


def _route(logits):
    lane = lax.broadcasted_iota(jnp.int32, logits.shape, 1)
    big = jnp.int32(ROUTER_LANES)
    neg = jnp.float32(-jnp.inf)
    lc = jnp.where(lane < N_GROUPS, logits, neg)
    mc = jnp.max(lc, axis=1, keepdims=True)
    p_group = 1.0 / jnp.sum(jnp.exp(lc - mc), axis=1, keepdims=True)
    g_idx = jnp.min(jnp.where(lc == mc, lane, big), axis=1, keepdims=True)
    lo = N_GROUPS + EXPERTS_PER_GROUP * g_idx
    lf = jnp.where((lane >= lo) & (lane < lo + EXPERTS_PER_GROUP), logits, neg)
    v1 = jnp.max(lf, axis=1, keepdims=True)
    i1 = jnp.min(jnp.where(lf == v1, lane, big), axis=1, keepdims=True)
    lf2 = jnp.where(lane == i1, neg, lf)
    v2 = jnp.max(lf2, axis=1, keepdims=True)
    i2 = jnp.min(jnp.where(lf2 == v2, lane, big), axis=1, keepdims=True)
    e2 = jnp.exp(v2 - v1)
    w1 = p_group / (1.0 + e2)
    w2 = p_group * e2 / (1.0 + e2)
    return lane, i1, i2, w1, w2


def _split_bf16(x):
    hi = x.astype(BF16)
    return hi, (x - hi.astype(F32)).astype(BF16)


def _sorted_positions(lane, i1, i2):
    t = lane.shape[0]
    oh1 = lane == i1
    oh2 = lane == i2
    oh = jnp.concatenate([oh1, oh2], axis=1).astype(BF16)
    r = lax.broadcasted_iota(jnp.int32, (t, t), 0)
    c = lax.broadcasted_iota(jnp.int32, (t, t), 1)
    before = (c < r).astype(BF16)
    rank = jnp.dot(before, oh, preferred_element_type=F32)
    cnt1 = jnp.sum(oh1.astype(F32), axis=0, keepdims=True)
    cnt2 = jnp.sum(oh2.astype(F32), axis=0, keepdims=True)
    run = jnp.ceil((cnt1 + cnt2) * (1.0 / SLOT_ALIGN)) * SLOT_ALIGN
    lr = lax.broadcasted_iota(jnp.int32, (ROUTER_LANES, ROUTER_LANES), 0)
    lc = lax.broadcasted_iota(jnp.int32, (ROUTER_LANES, ROUTER_LANES), 1)
    start = jnp.dot(jnp.broadcast_to(run, (8, ROUTER_LANES)).astype(BF16), (lr < lc).astype(BF16),
                    preferred_element_type=F32)[0:1]
    pos1 = jnp.sum(jnp.where(oh1, start + rank[:, :ROUTER_LANES], 0.0), axis=1, keepdims=True)
    pos2 = jnp.sum(jnp.where(oh2, start + cnt1 + rank[:, ROUTER_LANES:], 0.0), axis=1, keepdims=True)
    return pos1, pos2, run


def _merge_kernel(u_ref, up_ref, un_ref, a_ref, x_ref, pw_ref, ps_ref, wo_ref, g1_ref, n2_ref, sh_ref,
                  sc_ref, wr_ref, rb_ref, xo_ref, xa_ref, tm_ref, lm_ref, run_ref, ext_scr, *, t, seq_len,
                  d_pool):
    i = pl.program_id(1)
    n_tiles = pl.num_programs(1)
    u = u_ref[0]
    ext_scr[0:HALO] = jnp.where(i > 0, up_ref[0], 0.0)
    ext_scr[HALO:HALO + t] = u
    ext_scr[HALO + t:] = jnp.where(i < n_tiles - 1, un_ref[0], 0.0)

    gc = d_pool // len(POOL_WINDOWS)
    d = x_ref.shape[2]
    tr = t // MERGE_PARTS
    parts = [slice(r0, r0 + tr) for r0 in range(0, t, tr)]

    def pool(rows):
        pos = i * t + rows.start + lax.broadcasted_iota(jnp.int32, (tr, 1), 0)
        pooled = []
        for gi, w in enumerate(POOL_WINDOWS):
            left = w // 2
            right = w - 1 - left
            cols = slice(gi * gc, (gi + 1) * gc)
            s = ext_scr[pl.ds(rows.start + HALO - left, tr), cols]
            for o in range(-left + 1, right + 1):
                s = s + ext_scr[pl.ds(rows.start + HALO + o, tr), cols]
            cnt = (jnp.minimum(pos + right + 1, seq_len) - jnp.maximum(pos - left, 0)).astype(F32)
            pooled.append((s / cnt - u_ref[0, rows, cols]).astype(BF16))
        return pooled

    def project(rows, pooled):
        mix = jnp.dot(a_ref[0, rows], wo_ref[d_pool:, :], preferred_element_type=F32)
        for gi, pg in enumerate(pooled):
            cols = slice(gi * gc, (gi + 1) * gc)
            mixed = jnp.dot(pg, pw_ref[gi], preferred_element_type=F32) * ps_ref[:, cols]
            mix = mix + jnp.dot(mixed.astype(BF16), wo_ref[cols, :], preferred_element_type=F32)
        return mix

    def renorm(rows, mix):
        x_new = x_ref[0, rows] + g1_ref[0] * mix
        xo_ref[0, rows] = x_new
        h2 = _rmsnorm(x_new, n2_ref[...]) * (1.0 + sc_ref[0]) + sh_ref[0]
        h_hi, h_lo = _split_bf16(h2)
        xa_ref[0, rows, :d] = h_hi
        return h_hi, h_lo

    def router(h_hi, h_lo):
        return (jnp.dot(h_hi, wr_ref[0], preferred_element_type=F32)
                + jnp.dot(h_lo, wr_ref[0], preferred_element_type=F32)
                + jnp.dot(h_hi, wr_ref[1], preferred_element_type=F32))

    n_p = len(parts)
    pooled = [None] * n_p
    mix = [None] * n_p
    split = [None] * n_p
    logits = [None] * n_p
    pooled[0] = pool(parts[0])
    for k in range(n_p):
        mix[k] = project(parts[k], pooled[k])
        if k + 1 < n_p:
            pooled[k + 1] = pool(parts[k + 1])
        if k >= 1:
            logits[k - 1] = router(*split[k - 1])
        split[k] = renorm(parts[k], mix[k])
    logits[n_p - 1] = router(*split[n_p - 1])
    logits = jnp.concatenate(logits, axis=0) + rb_ref[...]
    lane, i1, i2, w1, w2 = _route(logits)
    pos1, pos2, run = _sorted_positions(lane, i1, i2)
    w1_hi = w1.astype(BF16).astype(F32)
    w2_hi = w2.astype(BF16).astype(F32)
    aux = jnp.zeros(lane.shape, F32)
    for k, val in enumerate((w1_hi, w1 - w1_hi, w2_hi, w2 - w2_hi, (i1 - N_GROUPS).astype(F32))):
        aux = jnp.where(lane == k, val, aux)
    xa_ref[0, :, d:] = aux.astype(BF16)
    info = jnp.where(lane == 0, pos1, jnp.where(lane == 1, pos2, 0.0))
    tm_ref[0] = info
    lm_ref[0] = info.T[:8]
    run_ref[0, 0] = run


def _merge(u, attn, x, pool_w_bf, pool_scale, w_out_bf, g1, norm2_g, sh2, sc2, wr, rb):
    b, l, d = x.shape
    d_pool = u.shape[2]
    t = min(MOE_TOKEN_TILE, l)
    hb = t // HALO
    n_halo = l // HALO
    row = lambda bi, i: (bi, i, 0)
    per_batch = lambda bi, i: (bi, 0, 0)
    const2 = lambda bi, i: (0, 0)
    const3 = lambda bi, i: (0, 0, 0)
    return pl.pallas_call(
        functools.partial(_merge_kernel, t=t, seq_len=l, d_pool=d_pool),
        grid=(b, l // t),
        in_specs=[
            pl.BlockSpec((1, t, d_pool), row),
            pl.BlockSpec((1, HALO, d_pool), lambda bi, i: (bi, jnp.maximum(i * hb - 1, 0), 0)),
            pl.BlockSpec((1, HALO, d_pool), lambda bi, i: (bi, jnp.minimum((i + 1) * hb, n_halo - 1), 0)),
            pl.BlockSpec((1, t, attn.shape[2]), row),
            pl.BlockSpec((1, t, d), row),
            pl.BlockSpec(pool_w_bf.shape, const3),
            pl.BlockSpec((1, d_pool), const2),
            pl.BlockSpec(w_out_bf.shape, const2),
            pl.BlockSpec((1, 1, d), per_batch),
            pl.BlockSpec((1, d), const2),
            pl.BlockSpec((1, 1, d), per_batch),
            pl.BlockSpec((1, 1, d), per_batch),
            pl.BlockSpec(wr.shape, const3),
            pl.BlockSpec((1, ROUTER_LANES), const2),
        ],
        out_specs=[
            pl.BlockSpec((1, t, d), row),
            pl.BlockSpec((1, t, d + ROUTER_LANES), row),
            pl.BlockSpec((1, t, ROUTER_LANES), row),
            pl.BlockSpec((1, 8, t), lambda bi, i: (bi, 0, i)),
            pl.BlockSpec((1, 1, 1, ROUTER_LANES), lambda bi, i: (bi, i, 0, 0)),
        ],
        out_shape=[
            jax.ShapeDtypeStruct((b, l, d), F32),
            jax.ShapeDtypeStruct((b, l, d + ROUTER_LANES), BF16),
            jax.ShapeDtypeStruct((b, l, ROUTER_LANES), F32),
            jax.ShapeDtypeStruct((b, 8, l), F32),
            jax.ShapeDtypeStruct((b, l // t, 1, ROUTER_LANES), F32),
        ],
        scratch_shapes=[pltpu.VMEM((t + 2 * HALO, d_pool), F32)],
        compiler_params=_params(("arbitrary", "arbitrary")),
        name="merge",
    )(u, u, u, attn, x, pool_w_bf, pool_scale.reshape(1, d_pool), w_out_bf, g1, norm2_g.reshape(1, d),
      sh2, sc2, wr, rb)


def _moe_plan(runs, t, tm):
    counts = runs[:, N_GROUPS:N_GROUPS + N_EXPERTS].astype(jnp.int32)
    n_tiles = counts.shape[0]
    total = jnp.sum(counts, axis=0)
    region = (total + tm - 1) // tm * tm
    region_end = jnp.cumsum(region)
    start = region_end - region
    dst = start[None, :] + jnp.cumsum(counts, axis=0) - counts
    off = jnp.cumsum(counts, axis=1) - counts
    s_max = 2 * n_tiles * t + n_tiles * N_EXPERTS * SLOT_ALIGN + N_EXPERTS * tm
    n_row_tiles = -(-s_max // tm)
    first_row = jnp.arange(n_row_tiles, dtype=jnp.int32) * tm
    valid = first_row < region_end[-1]
    last_valid = jnp.maximum(region_end[-1] // tm - 1, 0)
    block = jnp.where(valid, jnp.arange(n_row_tiles, dtype=jnp.int32), last_valid)
    expert = jnp.sum((region_end[None, :] <= (block * tm)[:, None]).astype(jnp.int32), axis=1)
    expert = jnp.minimum(expert, N_EXPERTS - 1)
    gap_dst = jnp.concatenate([start + total, region_end[-1:]])
    gap_n = jnp.concatenate([(region - total) // SLOT_ALIGN, n_row_tiles - region_end[-1:] // tm])
    plan = dict(dst=dst.reshape(-1), off=off.reshape(-1), n=(counts // SLOT_ALIGN).reshape(-1),
                gap_dst=gap_dst, gap_n=gap_n, expert=expert, block=block, valid=valid.astype(jnp.int32))
    return plan, n_row_tiles


def _start_runs(n_ref, tile, src_first, dst_first, src_rows, dst_rows, sem, n_runs=N_EXPERTS):
    for e in range(n_runs):
        n = n_ref[tile * n_runs + e]
        s0, d0 = src_first(e), dst_first(e)

        def issue(j, carry, s0=s0, d0=d0):
            step = j * SLOT_ALIGN
            pltpu.make_async_copy(src_rows(s0 + step), dst_rows(d0 + step), sem).start()
            return carry

        lax.fori_loop(0, n, issue, 0)


def _wait_runs(n_ref, tile, src, dst, sem, n_runs=N_EXPERTS):
    total = 0
    for e in range(n_runs):
        total = total + n_ref[tile * n_runs + e]

    def drain(j, carry):
        pltpu.make_async_copy(src, dst, sem).wait()
        return carry

    lax.fori_loop(0, total, drain, 0)


def _span(first):
    return pl.ds(pl.multiple_of(first, SLOT_ALIGN), SLOT_ALIGN)


def _dispatch_kernel(dst_ref, off_ref, n_ref, sdst_ref, sn_ref, xa_ref, lm_ref, xs_hbm, buf, zbuf, sems, *, p,
                     tm):
    i = pl.program_id(0)
    last = pl.num_programs(0) - 1
    slot = i % 2
    t = xa_ref.shape[0]
    one_src, one_dst = buf.at[0, pl.ds(0, SLOT_ALIGN)], xs_hbm.at[pl.ds(0, SLOT_ALIGN)]

    @pl.when(i >= 2)
    def _():
        _wait_runs(n_ref, i - 2, one_src, one_dst, sems.at[slot])

    pos1 = lm_ref[0, 0:1, :].astype(jnp.int32)
    pos2 = lm_ref[0, 1:2, :].astype(jnp.int32)
    xa = xa_ref[...]
    for r0 in range(0, p, PERM_CHUNK):
        row = r0 + lax.broadcasted_iota(jnp.int32, (PERM_CHUNK, t), 0)
        perm = ((row == pos1) | (row == pos2)).astype(BF16)
        buf[slot, r0:r0 + PERM_CHUNK] = jnp.dot(perm, xa, preferred_element_type=F32).astype(BF16)
    _start_runs(n_ref, i, lambda e: off_ref[i * N_EXPERTS + e], lambda e: dst_ref[i * N_EXPERTS + e],
                lambda first: buf.at[slot, _span(first)], lambda first: xs_hbm.at[_span(first)], sems.at[slot])

    @pl.when(i == last)
    def _():
        @pl.when(i >= 1)
        def _():
            _wait_runs(n_ref, i - 1, one_src, one_dst, sems.at[1 - slot])

        _wait_runs(n_ref, i, one_src, one_dst, sems.at[slot])
        zbuf[...] = jnp.zeros(zbuf.shape, BF16)
        _start_runs(sn_ref, 0, lambda e: 0, lambda e: sdst_ref[e],
                    lambda first: zbuf, lambda first: xs_hbm.at[_span(first)], sems.at[slot])
        _wait_runs(sn_ref, 0, one_src, one_dst, sems.at[slot])
        buf[slot, 0:tm] = jnp.zeros((tm, buf.shape[2]), BF16)
        tail = lambda j: pltpu.make_async_copy(
            buf.at[slot, pl.ds(0, tm)],
            xs_hbm.at[pl.ds(pl.multiple_of(sdst_ref[N_EXPERTS] + j * tm, SLOT_ALIGN), tm)], sems.at[slot])
        lax.fori_loop(0, sn_ref[N_EXPERTS], lambda j, c: (tail(j).start(), c)[1], 0)
        lax.fori_loop(0, sn_ref[N_EXPERTS], lambda j, c: (tail(j).wait(), c)[1], 0)


def _expert_kernel(e_ref, blk_ref, valid_ref, xs_ref, wg_ref, wu_ref, wd_ref, ys_ref, wg_bf, wu_bf, wd_bf, *, d):
    r = pl.program_id(0)
    e = e_ref[r]

    @pl.when(jnp.logical_or(r == 0, e != e_ref[jnp.maximum(r - 1, 0)]))
    def _():
        wg_bf[...] = wg_ref[0, 0, 0].astype(BF16)
        wu_bf[...] = wu_ref[0, 0, 0].astype(BF16)
        wd_bf[...] = wd_ref[0, 0, 0].astype(BF16)

    @pl.when(valid_ref[r] == 1)
    def _():
        x = xs_ref[:, :d]
        aux = xs_ref[:, d:].astype(F32)
        is_first = aux[:, 4:5] == e.astype(F32)
        gate = jnp.where(is_first, aux[:, 0:1] + aux[:, 1:2], aux[:, 2:3] + aux[:, 3:4])
        a = jnp.dot(x, wg_bf[...], preferred_element_type=F32)
        bu = jnp.dot(x, wu_bf[...], preferred_element_type=F32)
        hid = (a * jax.nn.sigmoid(a)) * bu * gate
        ys_ref[...] = jnp.dot(hid.astype(BF16), wd_bf[...], preferred_element_type=F32).astype(BF16)

    @pl.when(valid_ref[r] == 0)
    def _():
        ys_ref[...] = jnp.zeros(ys_ref.shape, BF16)


def _combine_kernel(*refs, final_norm):
    if final_norm:
        dst_ref, off_ref, n_ref, ys_hbm, tm_ref, x_ref, g2_ref, fg_ref, o_ref, buf, sems = refs
    else:
        dst_ref, off_ref, n_ref, ys_hbm, tm_ref, x_ref, g2_ref, o_ref, buf, sems = refs
    i = pl.program_id(0)
    slot = i % 2
    t = x_ref.shape[0]
    p = buf.shape[1]

    def fetch(tile, s):
        buf[s] = jnp.zeros(buf.shape[1:], BF16)
        _start_runs(n_ref, tile, lambda e: dst_ref[tile * N_EXPERTS + e], lambda e: off_ref[tile * N_EXPERTS + e],
                    lambda first: ys_hbm.at[_span(first)], lambda first: buf.at[s, _span(first)], sems.at[s])

    @pl.when(i == 0)
    def _():
        fetch(i, slot)

    @pl.when(i + 1 < pl.num_programs(0))
    def _():
        fetch(i + 1, 1 - slot)

    _wait_runs(n_ref, i, ys_hbm.at[pl.ds(0, SLOT_ALIGN)], buf.at[0, pl.ds(0, SLOT_ALIGN)], sems.at[slot])
    pos1 = tm_ref[:, 0:1].astype(jnp.int32)
    pos2 = tm_ref[:, 1:2].astype(jnp.int32)
    f = None
    for r0 in range(0, p, PERM_CHUNK):
        col = r0 + lax.broadcasted_iota(jnp.int32, (t, PERM_CHUNK), 1)
        perm_t = ((col == pos1) | (col == pos2)).astype(BF16)
        part = jnp.dot(perm_t, buf[slot, r0:r0 + PERM_CHUNK], preferred_element_type=F32)
        f = part if f is None else f + part
    y = x_ref[...] + g2_ref[0] * f
    if final_norm:
        y = _rmsnorm(y, fg_ref[...])
    o_ref[...] = y


def _moe(x, xa, info_tm, info_lm, runs, g2, w_gate, w_up, w_down, layer, final_g):
    b, l, d = x.shape
    da = xa.shape[2]
    f = w_gate.shape[-1]
    t = min(MOE_TOKEN_TILE, l)
    tiles_per_batch = l // t
    n_tiles = b * tiles_per_batch
    p = 2 * t + N_EXPERTS * SLOT_ALIGN
    n = b * l
    tm = MOE_ROW_TILE if 2 * n >= N_EXPERTS * MOE_ROW_TILE else MOE_ROW_TILE // 4
    plan, n_row_tiles = _moe_plan(runs.reshape(n_tiles, ROUTER_LANES), t, tm)
    s_rows = n_row_tiles * tm

    xs = pl.pallas_call(
        functools.partial(_dispatch_kernel, p=p, tm=tm),
        grid_spec=pltpu.PrefetchScalarGridSpec(
            num_scalar_prefetch=5,
            grid=(n_tiles,),
            in_specs=[
                pl.BlockSpec((t, da), lambda i, *_: (i, 0)),
                pl.BlockSpec((1, 8, t), lambda i, *_: (i // tiles_per_batch, 0, i % tiles_per_batch)),
            ],
            out_specs=pl.BlockSpec(memory_space=pl.ANY),
            scratch_shapes=[pltpu.VMEM((2, p, da), BF16), pltpu.VMEM((SLOT_ALIGN, da), BF16),
                            pltpu.SemaphoreType.DMA((2,))],
        ),
        out_shape=jax.ShapeDtypeStruct((s_rows, da), BF16),
        compiler_params=_params(("arbitrary",)),
        name="moe_dispatch",
    )(plan["dst"], plan["off"], plan["n"], plan["gap_dst"], plan["gap_n"], xa.reshape(n, da), info_lm)

    w_map = lambda r, e_ref, blk_ref, valid_ref: (
        layer, e_ref[r] // EXPERTS_PER_GROUP, e_ref[r] % EXPERTS_PER_GROUP, 0, 0)
    row_map = lambda r, e_ref, blk_ref, valid_ref: (blk_ref[r], 0)
    ys = pl.pallas_call(
        functools.partial(_expert_kernel, d=d),
        grid_spec=pltpu.PrefetchScalarGridSpec(
            num_scalar_prefetch=3,
            grid=(n_row_tiles,),
            in_specs=[
                pl.BlockSpec((tm, da), row_map),
                pl.BlockSpec((1, 1, 1, d, f), w_map),
                pl.BlockSpec((1, 1, 1, d, f), w_map),
                pl.BlockSpec((1, 1, 1, f, d), w_map),
            ],
            out_specs=pl.BlockSpec((tm, d), lambda r, *_: (r, 0)),
            scratch_shapes=[pltpu.VMEM((d, f), BF16), pltpu.VMEM((d, f), BF16), pltpu.VMEM((f, d), BF16)],
        ),
        out_shape=jax.ShapeDtypeStruct((s_rows, d), BF16),
        compiler_params=_params(("arbitrary",)),
        name="moe_experts",
    )(plan["expert"], plan["block"], plan["valid"], xs, w_gate, w_up, w_down)

    final_norm = final_g is not None
    in_specs = [
        pl.BlockSpec(memory_space=pl.ANY),
        pl.BlockSpec((t, ROUTER_LANES), lambda i, *_: (i, 0)),
        pl.BlockSpec((t, d), lambda i, *_: (i, 0)),
        pl.BlockSpec((1, 1, d), lambda i, *_: (i // tiles_per_batch, 0, 0)),
    ]
    args = [ys, info_tm.reshape(n, ROUTER_LANES), x.reshape(n, d), g2]
    if final_norm:
        in_specs.append(pl.BlockSpec((1, d), lambda i, *_: (0, 0)))
        args.append(final_g.reshape(1, d))
    out = pl.pallas_call(
        functools.partial(_combine_kernel, final_norm=final_norm),
        grid_spec=pltpu.PrefetchScalarGridSpec(
            num_scalar_prefetch=3,
            grid=(n_tiles,),
            in_specs=in_specs,
            out_specs=pl.BlockSpec((t, d), lambda i, *_: (i, 0)),
            scratch_shapes=[pltpu.VMEM((2, p, d), BF16), pltpu.SemaphoreType.DMA((2,))],
        ),
        out_shape=jax.ShapeDtypeStruct((n, d), F32),
        compiler_params=_params(("arbitrary",)),
        name="moe_combine",
    )(plan["dst"], plan["off"], plan["n"], *args)
    return out.reshape(b, l, d)


def kernel(x, c, ctx, c_ctx, ada_w, ada_b, norm1_g, norm2_g, w_in, pool_w, pool_scale, lambda_q1, lambda_k1,
           lambda_q2, lambda_k2, subln_g, w_out, router_coarse_w, router_coarse_b, router_fine_w,
           router_fine_b, w_gate, w_up, w_down, final_g):
    b, l, d = x.shape
    depth = ada_w.shape[0]
    assert b + 1 <= MOD_ROWS and l % GRID_W == 0

    tables = _rope_tables(l)
    cc = jnp.zeros((MOD_ROWS, d), F32).at[:b].set(c).at[b].set(c_ctx)
    d_pool = d // 2
    d_attn = (w_in.shape[-1] - d_pool) // 3
    w_uk = jnp.concatenate([w_in[..., :d_pool], w_in[..., d_pool + d_attn:d_pool + 2 * d_attn]],
                           axis=-1).astype(BF16)
    w_qvt = jnp.swapaxes(jnp.concatenate([w_in[..., d_pool:d_pool + d_attn], w_in[..., d_pool + 2 * d_attn:]],
                                         axis=-1), 1, 2).astype(BF16)
    pool_w_bf = pool_w.astype(BF16)
    w_out_bf = w_out.astype(BF16)
    pad = ROUTER_LANES - N_GROUPS - N_EXPERTS
    wr = jnp.concatenate([router_coarse_w, router_fine_w, jnp.zeros((depth, d, pad), F32)], axis=-1)
    wr_hi = wr.astype(BF16)
    wr = jnp.stack([wr_hi, (wr - wr_hi.astype(F32)).astype(BF16)], axis=1)
    rb = jnp.concatenate([router_coarse_b, router_fine_b, jnp.zeros((depth, pad), F32)], axis=-1)

    xc = ctx
    for layer in range(depth):
        last = layer == depth - 1
        lam_init = 0.8 - 0.6 * math.exp(-0.3 * layer)
        mod = _ada(cc, ada_w, ada_b, layer)
        sh1, sc1, g1, sh2, sc2, g2 = jnp.split(mod[:b, None, :], 6, axis=-1)
        csh1, csc1, cg1, csh2, csc2, cg2 = jnp.split(
            jnp.broadcast_to(mod[b][None, None, :], (b, 1, mod.shape[1])), 6, axis=-1)
        lam4 = jnp.stack([lambda_q1[layer], lambda_k1[layer], lambda_q2[layer], lambda_k2[layer]])

        u, q, k, v = _inproj(x, sh1, sc1, norm1_g[layer], w_uk[layer], w_qvt[layer], tables)
        uc, qc, kc, vc = _inproj(xc, csh1, csc1, norm1_g[layer], w_uk[layer], w_qvt[layer], None)
        attn = _attention(q, k, v, kc, vc, lam4, subln_g[layer], lam_init)
        merge_w = (pool_w_bf[layer], pool_scale[layer], w_out_bf[layer])
        route_w = (wr[layer], rb[layer].reshape(1, ROUTER_LANES))
        x, *routed = _merge(u, attn, x, *merge_w, g1, norm2_g[layer], sh2, sc2, *route_w)
        if not last:
            attn_c = _attention(qc, kc, vc, None, None, lam4, subln_g[layer], lam_init)
            xc, *routed_c = _merge(uc, attn_c, xc, *merge_w, cg1, norm2_g[layer], csh2, csc2, *route_w)
            xc = _moe(xc, *routed_c, cg2, w_gate, w_up, w_down, layer, None)
        x = _moe(x, *routed, g2, w_gate, w_up, w_down, layer, final_g if last else None)
    return x
```

```python
import functools
import math

import jax
import jax.numpy as jnp
from jax import lax
from jax.experimental import pallas as pl
from jax.experimental.pallas import tpu as pltpu

F32 = jnp.float32
BF16 = jnp.bfloat16

GRID_W = 64
POOL_WINDOWS = (2, 4, 8, 16)
HEAD_DIM = 64
HEAD_PAIR = 2 * HEAD_DIM
ROPE_THETA = 10000.0
N_GROUPS = 4
EXPERTS_PER_GROUP = 8
N_EXPERTS = N_GROUPS * EXPERTS_PER_GROUP
EPS = 1e-6
HALO = 16
ROUTER_LANES = 128
ROUTE_ROWS = 40
POS_ROW = 5
MOD_ROWS = 8
ONES_ROWS = 16
MAX_EXCESS = 24.0
KEY_GROUP = 32
SLOT_ALIGN = 16
MOE_TOKEN_TILE = 512
MOE_ROW_TILE = 512
PERM_CHUNK = 256
MERGE_PARTS = 2

VMEM_LIMIT = 48 * 1024 * 1024


def _params(sem):
    return pltpu.CompilerParams(dimension_semantics=sem, vmem_limit_bytes=VMEM_LIMIT)


def _ada_kernel(cc_ref, w_ref, b_ref, o_ref):
    s = cc_ref[...]
    s = s * jax.nn.sigmoid(s)
    o_ref[...] = jnp.dot(s, w_ref[0], precision=lax.Precision.HIGHEST,
                         preferred_element_type=F32) + b_ref[0]


def _ada(cc, ada_w, ada_b, layer):
    _, d, n = ada_w.shape
    bn = 1024
    return pl.pallas_call(
        _ada_kernel,
        grid=(n // bn,),
        in_specs=[
            pl.BlockSpec((MOD_ROWS, d), lambda j: (0, 0)),
            pl.BlockSpec((1, d, bn), lambda j: (layer, 0, j)),
            pl.BlockSpec((1, 1, bn), lambda j: (layer, 0, j)),
        ],
        out_specs=pl.BlockSpec((MOD_ROWS, bn), lambda j: (0, j)),
        out_shape=jax.ShapeDtypeStruct((MOD_ROWS, n), F32),
        compiler_params=_params(("arbitrary",)),
        name="ada",
    )(cc, ada_w, ada_b.reshape(ada_b.shape[0], 1, n))


def _rope_tables(n):
    pos = jnp.arange(n, dtype=jnp.int32)
    row = (pos // GRID_W).astype(F32)
    col = (pos % GRID_W).astype(F32)
    n_freq = HEAD_DIM // 4
    inv = ROPE_THETA ** (-jnp.arange(n_freq, dtype=F32) / n_freq)
    ang_r = row[:, None] * inv
    ang_c = col[:, None] * inv
    cos = jnp.concatenate([jnp.cos(ang_r), jnp.cos(ang_r), jnp.cos(ang_c), jnp.cos(ang_c)], axis=-1)
    sin = jnp.concatenate([-jnp.sin(ang_r), jnp.sin(ang_r), -jnp.sin(ang_c), jnp.sin(ang_c)], axis=-1)
    cos, sin = jnp.tile(cos, (1, 2)), jnp.tile(sin, (1, 2))
    return cos, sin, cos.T, sin.T


def _rmsnorm(x, g):
    ms = jnp.mean(x * x, axis=-1, keepdims=True)
    return x * lax.rsqrt(ms + EPS) * g


def _inproj_kernel(*refs, rope, d_pool, d_attn, q_scale):
    if rope:
        (x_ref, sh_ref, sc_ref, g_ref, wuk_ref, wqv_ref, cos_ref, sin_ref, cost_ref, sint_ref,
         u_ref, qt_ref, k_ref, vt_ref) = refs
    else:
        x_ref, sh_ref, sc_ref, g_ref, wuk_ref, wqv_ref, u_ref, qt_ref, k_ref, vt_ref = refs
    h = (_rmsnorm(x_ref[0], g_ref[...]) * (1.0 + sc_ref[0]) + sh_ref[0]).astype(BF16)
    p = jnp.dot(h, wuk_ref[...], preferred_element_type=F32)
    pt = lax.dot_general(wqv_ref[...], h, (((1,), (1,)), ((), ())), preferred_element_type=F32)
    u_ref[0] = p[:, :d_pool]
    vt_ref[0] = pt[d_attn:].astype(BF16)
    half = HEAD_DIM // 4
    if rope:
        cos, sin = cos_ref[...], sin_ref[...]
        cost, sint = cost_ref[...], sint_ref[...]
        lane = lax.broadcasted_iota(jnp.int32, cos.shape, 1)
        first_half = (lane % (2 * half)) < half
    for h0 in range(0, d_attn, HEAD_PAIR):
        c = p[:, d_pool + h0: d_pool + h0 + HEAD_PAIR]
        ct = pt[h0:h0 + HEAD_PAIR]
        if rope:
            partner = jnp.where(first_half, pltpu.roll(c, HEAD_PAIR - half, 1), pltpu.roll(c, half, 1))
            c = c * cos + partner * sin
            swapped = []
            for r0 in range(0, HEAD_PAIR, 2 * half):
                swapped += [ct[r0 + half:r0 + 2 * half], ct[r0:r0 + half]]
            ct = ct * cost + jnp.concatenate(swapped, axis=0) * sint
        k_ref[0, :, h0:h0 + HEAD_PAIR] = c.astype(BF16)
        qt_ref[0, h0:h0 + HEAD_PAIR, :] = (ct * q_scale).astype(BF16)


def _inproj(x, shift, scale, g, w_uk, w_qvt, tables):
    b, l, d = x.shape
    d_pool = d // 2
    d_attn = w_qvt.shape[0] // 2
    t = min(512, l)
    rope = tables is not None
    row = lambda bi, i: (bi, i, 0)
    col = lambda bi, i: (bi, 0, i)
    per_batch = lambda bi, i: (bi, 0, 0)
    const = lambda bi, i: (0, 0)
    in_specs = [
        pl.BlockSpec((1, t, d), row),
        pl.BlockSpec((1, 1, d), per_batch),
        pl.BlockSpec((1, 1, d), per_batch),
        pl.BlockSpec((1, d), const),
        pl.BlockSpec(w_uk.shape, const),
        pl.BlockSpec(w_qvt.shape, const),
    ]
    args = [x, shift, scale, g.reshape(1, d), w_uk, w_qvt]
    if rope:
        in_specs += [pl.BlockSpec((t, HEAD_PAIR), lambda bi, i: (i, 0))] * 2
        in_specs += [pl.BlockSpec((HEAD_PAIR, t), lambda bi, i: (0, i))] * 2
        args += list(tables)
    return pl.pallas_call(
        functools.partial(_inproj_kernel, rope=rope, d_pool=d_pool, d_attn=d_attn,
                          q_scale=HEAD_DIM ** -0.5 * math.log2(math.e)),
        grid=(b, l // t),
        in_specs=in_specs,
        out_specs=[pl.BlockSpec((1, t, d_pool), row), pl.BlockSpec((1, d_attn, t), col),
                   pl.BlockSpec((1, t, d_attn), row), pl.BlockSpec((1, d_attn, t), col)],
        out_shape=[jax.ShapeDtypeStruct((b, l, d_pool), F32), jax.ShapeDtypeStruct((b, d_attn, l), BF16),
                   jax.ShapeDtypeStruct((b, l, d_attn), BF16), jax.ShapeDtypeStruct((b, d_attn, l), BF16)],
        compiler_params=_params(("arbitrary", "arbitrary")),
        name="inproj",
    )(*args)


def _attn_kernel(*refs, tq, tk, cb, n_chunks, group, has_ctx, lam_init):
    if has_ctx:
        (lam_ref, qt_ref, k_ref, vt_ref, kc_ref, vct_ref, g_ref, o_ref,
         qs_scr, m_scr, l_scr, acc_scr, s_scr) = refs
    else:
        lam_ref, qt_ref, k_ref, vt_ref, g_ref, o_ref, qs_scr, m_scr, l_scr, acc_scr, s_scr = refs
    qt = qt_ref[0]
    sub = lax.broadcasted_iota(jnp.int32, qt.shape, 0)
    zero = jnp.zeros_like(qt)
    qs_scr[:, :tq] = jnp.where(sub < HEAD_DIM, qt, zero)
    qs_scr[:, tq:] = jnp.where(sub >= HEAD_DIM, qt, zero)
    m_scr[...] = jnp.full(m_scr.shape, -jnp.inf, F32)
    l_scr[...] = jnp.zeros(l_scr.shape, F32)
    acc_scr[...] = jnp.zeros(acc_scr.shape, F32)

    n_blocks = 2 * tq // cb
    blocks = [slice(c * cb, (c + 1) * cb) for c in range(n_blocks)]

    def scores(kj, c):
        return jnp.dot(kj, qs_scr[:, blocks[c]], preferred_element_type=F32)

    def with_ones(vtj):
        return jnp.concatenate([vtj, jnp.ones((ONES_ROWS, vtj.shape[1]), BF16)], axis=0)

    def span(j):
        return pl.ds(j * tk if isinstance(j, int) else pl.multiple_of(j * tk, tk), tk)

    def keys(j):
        return k_ref[0, span(j), :]

    def values_t(j):
        return vt_ref[0, :, span(j)]

    def online_chunk(kj, vtj):
        vta = with_ones(vtj)
        s_next = scores(kj, 0)
        for c, cols in enumerate(blocks):
            s = s_next
            if c + 1 < n_blocks:
                s_next = scores(kj, c + 1)
            m_old = m_scr[:, cols]
            m_new = jnp.maximum(m_old, jnp.max(s, axis=0, keepdims=True))
            alpha = jnp.exp2(m_old - m_new)
            p = jnp.exp2(s - m_new).astype(BF16)
            r = jnp.dot(vta, p, preferred_element_type=F32)
            l_scr[:, cols] = alpha * l_scr[:, cols] + r[HEAD_PAIR:HEAD_PAIR + 1]
            acc_scr[:, cols] = alpha * acc_scr[:, cols] + r[:HEAD_PAIR]
            m_scr[:, cols] = m_new

    def group_fast(j0, j_next):
        items = [(c, jj) for c in range(n_blocks) for jj in range(group)]
        s_next = s_scr[...]
        sums, excess = [], None
        for idx, (c, jj) in enumerate(items):
            s = s_next
            if idx + 1 < len(items):
                c2, jj2 = items[idx + 1]
                s_next = scores(keys(j0 + jj2), c2)
            elif j_next is not None:
                s_scr[...] = scores(keys(j_next), 0)
            m_ref = m_scr[:, blocks[c]]
            top = jnp.max(s, axis=0, keepdims=True)
            p = jnp.exp2(s - m_ref).astype(BF16)
            r = jnp.dot(with_ones(values_t(j0 + jj)), p, preferred_element_type=F32)
            r_c, top_c = (r, top) if jj == 0 else (r_c + r, jnp.maximum(top_c, top))
            if jj == group - 1:
                sums.append(r_c)
                over = jnp.max(top_c - m_ref)
                excess = over if excess is None else jnp.maximum(excess, over)
        ok = excess <= MAX_EXCESS

        @pl.when(ok)
        def _():
            for cols, r_c in zip(blocks, sums):
                l_scr[:, cols] += r_c[HEAD_PAIR:HEAD_PAIR + 1]
                acc_scr[:, cols] += r_c[:HEAD_PAIR]

        return ok

    def group_step(j0, j_next):
        ok = group_fast(j0, j_next)

        @pl.when(jnp.logical_not(ok))
        def _():
            def redo(jj, carry):
                online_chunk(keys(j0 + jj), values_t(j0 + jj))
                return carry

            lax.fori_loop(0, group, redo, 0)

    n_groups = n_chunks // group
    if has_ctx:
        online_chunk(kc_ref[0], vct_ref[0])
    s_scr[...] = scores(keys(0), 0)
    if n_groups > 1:
        def body(g, carry):
            group_step(g * group, (g + 1) * group)
            return carry

        lax.fori_loop(0, n_groups - 1, body, 0)
    group_step((n_groups - 1) * group, None)

    o = acc_scr[...] / l_scr[...]
    lv = lam_ref[...]
    lam = (jnp.exp(jnp.sum(lv[0:1] * lv[1:2], axis=1, keepdims=True))
           - jnp.exp(jnp.sum(lv[2:3] * lv[3:4], axis=1, keepdims=True)) + lam_init)
    a = (o[:, :tq] - lam * o[:, tq:]).T
    o_ref[0] = (_rmsnorm(a, g_ref[...]) * (1.0 - lam_init)).astype(BF16)


def _attention(qt, k, vt, kc, vct, lam4, subln_g, lam_init):
    b, d_attn, lq = qt.shape
    lk = k.shape[1]
    n_heads = d_attn // HEAD_PAIR
    tq = min(1024, lq)
    tk = min(512, lk)
    cb = min(512, 2 * tq)
    has_ctx = kc is not None
    qmap = lambda bi, h, i: (bi, h, i)
    kmap = lambda bi, h, i: (bi, 0, h)
    vmap = lambda bi, h, i: (bi, h, 0)
    const = lambda bi, h, i: (0, 0)
    in_specs = [
        pl.BlockSpec(lam4.shape, const),
        pl.BlockSpec((1, HEAD_PAIR, tq), qmap),
        pl.BlockSpec((1, lk, HEAD_PAIR), kmap),
        pl.BlockSpec((1, HEAD_PAIR, lk), vmap),
    ]
    args = [lam4, qt, k, vt]
    if has_ctx:
        lc = kc.shape[1]
        in_specs += [pl.BlockSpec((1, lc, HEAD_PAIR), kmap), pl.BlockSpec((1, HEAD_PAIR, lc), vmap)]
        args += [kc, vct]
    in_specs.append(pl.BlockSpec((1, HEAD_PAIR), const))
    args.append(subln_g.reshape(1, HEAD_PAIR))
    return pl.pallas_call(
        functools.partial(_attn_kernel, tq=tq, tk=tk, cb=cb, n_chunks=lk // tk,
                          group=math.gcd(KEY_GROUP, lk // tk), has_ctx=has_ctx,
                          lam_init=lam_init),
        grid=(b, n_heads, lq // tq),
        in_specs=in_specs,
        out_specs=pl.BlockSpec((1, tq, HEAD_PAIR), lambda bi, h, i: (bi, i, h)),
        out_shape=jax.ShapeDtypeStruct((b, lq, d_attn), BF16),
        scratch_shapes=[
            pltpu.VMEM((HEAD_PAIR, 2 * tq), BF16),
            pltpu.VMEM((1, 2 * tq), F32),
            pltpu.VMEM((1, 2 * tq), F32),
            pltpu.VMEM((HEAD_PAIR, 2 * tq), F32),
            pltpu.VMEM((tk, cb), F32),
        ],
        compiler_params=_params(("arbitrary", "arbitrary", "arbitrary")),
        name="diff_attn",
    )(*args)


def _route(lt):
    row = lax.broadcasted_iota(jnp.int32, lt.shape, 0)
    big = jnp.int32(ROUTER_LANES)
    neg = jnp.float32(-jnp.inf)
    lc = jnp.where(row < N_GROUPS, lt, neg)
    mc = jnp.max(lc, axis=0, keepdims=True)
    p_group = 1.0 / jnp.sum(jnp.exp(lc - mc), axis=0, keepdims=True)
    g_idx = jnp.min(jnp.where(lc == mc, row, big), axis=0, keepdims=True)
    lo = N_GROUPS + EXPERTS_PER_GROUP * g_idx
    lf = jnp.where((row >= lo) & (row < lo + EXPERTS_PER_GROUP), lt, neg)
    v1 = jnp.max(lf, axis=0, keepdims=True)
    i1 = jnp.min(jnp.where(lf == v1, row, big), axis=0, keepdims=True)
    lf2 = jnp.where(row == i1, neg, lf)
    v2 = jnp.max(lf2, axis=0, keepdims=True)
    i2 = jnp.min(jnp.where(lf2 == v2, row, big), axis=0, keepdims=True)
    e2 = jnp.exp(v2 - v1)
    w1 = p_group / (1.0 + e2)
    w2 = p_group * e2 / (1.0 + e2)
    return row, i1, i2, w1, w2


def _split_bf16(x):
    hi = x.astype(BF16)
    return hi, (x - hi.astype(F32)).astype(BF16)


def _sorted_positions(row, i1, i2):
    n_rows, t = row.shape
    oh1 = row == i1
    oh2 = row == i2
    oh = jnp.concatenate([oh1, oh2], axis=0).astype(BF16)
    r = lax.broadcasted_iota(jnp.int32, (t, t), 0)
    c = lax.broadcasted_iota(jnp.int32, (t, t), 1)
    rank = jnp.dot(oh, (r < c).astype(BF16), preferred_element_type=F32)
    cnt1 = jnp.sum(oh1.astype(F32), axis=1, keepdims=True)
    cnt2 = jnp.sum(oh2.astype(F32), axis=1, keepdims=True)
    run = jnp.ceil((cnt1 + cnt2) * (1.0 / SLOT_ALIGN)) * SLOT_ALIGN
    run_b = jnp.broadcast_to(run, (n_rows, ROUTER_LANES))
    run_pad = jnp.concatenate([run_b, jnp.zeros((ROUTER_LANES - n_rows, ROUTER_LANES), F32)], axis=0)
    lr = lax.broadcasted_iota(jnp.int32, (n_rows, ROUTER_LANES), 0)
    lc = lax.broadcasted_iota(jnp.int32, (n_rows, ROUTER_LANES), 1)
    start = jnp.dot((lc < lr).astype(F32), run_pad, preferred_element_type=F32)[:, 0:1]
    pos1 = jnp.sum(jnp.where(oh1, start + rank[:n_rows], 0.0), axis=0, keepdims=True)
    pos2 = jnp.sum(jnp.where(oh2, start + cnt1 + rank[n_rows:], 0.0), axis=0, keepdims=True)
    return pos1, pos2, run_b


def _merge_kernel(u_ref, up_ref, un_ref, a_ref, x_ref, pw_ref, ps_ref, wo_ref, g1_ref, n2_ref, sh_ref,
                  sc_ref, wr_ref, rb_ref, xo_ref, xa_ref, tm_ref, lm_ref, run_ref, ext_scr, *, t, seq_len,
                  d_pool):
    i = pl.program_id(1)
    n_tiles = pl.num_programs(1)
    u = u_ref[0]
    ext_scr[0:HALO] = jnp.where(i > 0, up_ref[0], 0.0)
    ext_scr[HALO:HALO + t] = u
    ext_scr[HALO + t:] = jnp.where(i < n_tiles - 1, un_ref[0], 0.0)

    gc = d_pool // len(POOL_WINDOWS)
    d = x_ref.shape[2]
    tr = t // MERGE_PARTS
    parts = [slice(r0, r0 + tr) for r0 in range(0, t, tr)]

    def pool(rows):
        pos = i * t + rows.start + lax.broadcasted_iota(jnp.int32, (tr, 1), 0)
        pooled = []
        for gi, w in enumerate(POOL_WINDOWS):
            left = w // 2
            right = w - 1 - left
            cols = slice(gi * gc, (gi + 1) * gc)
            s = ext_scr[pl.ds(rows.start + HALO - left, tr), cols]
            for o in range(-left + 1, right + 1):
                s = s + ext_scr[pl.ds(rows.start + HALO + o, tr), cols]
            cnt = (jnp.minimum(pos + right + 1, seq_len) - jnp.maximum(pos - left, 0)).astype(F32)
            pooled.append((s / cnt - u_ref[0, rows, cols]).astype(BF16))
        return pooled

    def project(rows, pooled):
        mix = jnp.dot(a_ref[0, rows], wo_ref[d_pool:, :], preferred_element_type=F32)
        for k in range(pw_ref.shape[0]):
            cols = slice(2 * k * gc, (2 * k + 2) * gc)
            pair = jnp.concatenate(pooled[2 * k:2 * k + 2], axis=1)
            mixed = jnp.dot(pair, pw_ref[k], preferred_element_type=F32) * ps_ref[:, cols]
            mix = mix + jnp.dot(mixed.astype(BF16), wo_ref[cols, :], preferred_element_type=F32)
        return mix

    def renorm(rows, mix):
        x_new = x_ref[0, rows] + g1_ref[0] * mix
        xo_ref[0, rows] = x_new
        h2 = _rmsnorm(x_new, n2_ref[...]) * (1.0 + sc_ref[0]) + sh_ref[0]
        h_hi, h_lo = _split_bf16(h2)
        xa_ref[0, rows, :d] = h_hi
        return h_hi, h_lo

    def router(h_hi, h_lo):
        both = (jnp.dot(h_hi, wr_ref[...], preferred_element_type=F32)
                + jnp.dot(h_lo, wr_ref[...], preferred_element_type=F32))
        return both[:, :ROUTER_LANES] + both[:, ROUTER_LANES:]

    n_p = len(parts)
    pooled = [None] * n_p
    mix = [None] * n_p
    split = [None] * n_p
    logits = [None] * n_p
    pooled[0] = pool(parts[0])
    for k in range(n_p):
        mix[k] = project(parts[k], pooled[k])
        if k + 1 < n_p:
            pooled[k + 1] = pool(parts[k + 1])
        if k >= 1:
            logits[k - 1] = router(*split[k - 1])
        split[k] = renorm(parts[k], mix[k])
    logits[n_p - 1] = router(*split[n_p - 1])
    logits = jnp.concatenate(logits, axis=0) + rb_ref[...]
    row, i1, i2, w1, w2 = _route(logits.T[:ROUTE_ROWS])
    pos1, pos2, run = _sorted_positions(row, i1, i2)
    w1_hi = w1.astype(BF16).astype(F32)
    w2_hi = w2.astype(BF16).astype(F32)
    r8 = lax.broadcasted_iota(jnp.int32, (8, t), 0)
    slab = jnp.zeros((8, t), F32)
    for k, val in enumerate((w1_hi, w1 - w1_hi, w2_hi, w2 - w2_hi, (i1 - N_GROUPS).astype(F32), pos1, pos2)):
        slab = jnp.where(r8 == k, val, slab)
    lm_ref[0] = slab
    packed = jnp.concatenate([slab, jnp.zeros((ROUTER_LANES - 8, t), F32)], axis=0).T
    tm_ref[0] = packed
    lane = lax.broadcasted_iota(jnp.int32, packed.shape, 1)
    xa_ref[0, :, d:] = jnp.where(lane < POS_ROW, packed, 0.0).astype(BF16)
    run_ref[0, 0] = run


def _merge(u, attn, x, pool_w_bf, pool_scale, w_out_bf, g1, norm2_g, sh2, sc2, wr, rb):
    b, l, d = x.shape
    d_pool = u.shape[2]
    t = min(MOE_TOKEN_TILE, l)
    hb = t // HALO
    n_halo = l // HALO
    row = lambda bi, i: (bi, i, 0)
    per_batch = lambda bi, i: (bi, 0, 0)
    const2 = lambda bi, i: (0, 0)
    const3 = lambda bi, i: (0, 0, 0)
    return pl.pallas_call(
        functools.partial(_merge_kernel, t=t, seq_len=l, d_pool=d_pool),
        grid=(b, l // t),
        in_specs=[
            pl.BlockSpec((1, t, d_pool), row),
            pl.BlockSpec((1, HALO, d_pool), lambda bi, i: (bi, jnp.maximum(i * hb - 1, 0), 0)),
            pl.BlockSpec((1, HALO, d_pool), lambda bi, i: (bi, jnp.minimum((i + 1) * hb, n_halo - 1), 0)),
            pl.BlockSpec((1, t, attn.shape[2]), row),
            pl.BlockSpec((1, t, d), row),
            pl.BlockSpec(pool_w_bf.shape, const3),
            pl.BlockSpec((1, d_pool), const2),
            pl.BlockSpec(w_out_bf.shape, const2),
            pl.BlockSpec((1, 1, d), per_batch),
            pl.BlockSpec((1, d), const2),
            pl.BlockSpec((1, 1, d), per_batch),
            pl.BlockSpec((1, 1, d), per_batch),
            pl.BlockSpec(wr.shape, const2),
            pl.BlockSpec((1, ROUTER_LANES), const2),
        ],
        out_specs=[
            pl.BlockSpec((1, t, d), row),
            pl.BlockSpec((1, t, d + ROUTER_LANES), row),
            pl.BlockSpec((1, t, ROUTER_LANES), row),
            pl.BlockSpec((1, 8, t), lambda bi, i: (bi, 0, i)),
            pl.BlockSpec((1, 1, ROUTE_ROWS, ROUTER_LANES), lambda bi, i: (bi, i, 0, 0)),
        ],
        out_shape=[
            jax.ShapeDtypeStruct((b, l, d), F32),
            jax.ShapeDtypeStruct((b, l, d + ROUTER_LANES), BF16),
            jax.ShapeDtypeStruct((b, l, ROUTER_LANES), F32),
            jax.ShapeDtypeStruct((b, 8, l), F32),
            jax.ShapeDtypeStruct((b, l // t, ROUTE_ROWS, ROUTER_LANES), F32),
        ],
        scratch_shapes=[pltpu.VMEM((t + 2 * HALO, d_pool), F32)],
        compiler_params=_params(("arbitrary", "arbitrary")),
        name="merge",
    )(u, u, u, attn, x, pool_w_bf, pool_scale.reshape(1, d_pool), w_out_bf, g1, norm2_g.reshape(1, d),
      sh2, sc2, wr, rb)


def _moe_plan(runs, t, tm):
    counts = runs[:, N_GROUPS:N_GROUPS + N_EXPERTS].astype(jnp.int32)
    n_tiles = counts.shape[0]
    total = jnp.sum(counts, axis=0)
    region = (total + tm - 1) // tm * tm
    region_end = jnp.cumsum(region)
    start = region_end - region
    dst = start[None, :] + jnp.cumsum(counts, axis=0) - counts
    off = jnp.cumsum(counts, axis=1) - counts
    s_max = 2 * n_tiles * t + n_tiles * N_EXPERTS * SLOT_ALIGN + N_EXPERTS * tm
    n_row_tiles = -(-s_max // tm)
    first_row = jnp.arange(n_row_tiles, dtype=jnp.int32) * tm
    valid = first_row < region_end[-1]
    last_valid = jnp.maximum(region_end[-1] // tm - 1, 0)
    block = jnp.where(valid, jnp.arange(n_row_tiles, dtype=jnp.int32), last_valid)
    expert = jnp.sum((region_end[None, :] <= (block * tm)[:, None]).astype(jnp.int32), axis=1)
    expert = jnp.minimum(expert, N_EXPERTS - 1)
    gap_dst = jnp.concatenate([start + total, region_end[-1:]])
    gap_n = jnp.concatenate([(region - total) // SLOT_ALIGN, n_row_tiles - region_end[-1:] // tm])
    plan = dict(dst=dst.reshape(-1), off=off.reshape(-1), n=(counts // SLOT_ALIGN).reshape(-1),
                gap_dst=gap_dst, gap_n=gap_n, expert=expert, block=block, valid=valid.astype(jnp.int32))
    return plan, n_row_tiles


def _start_runs(n_ref, tile, src_first, dst_first, src_rows, dst_rows, sem, n_runs=N_EXPERTS):
    for e in range(n_runs):
        n = n_ref[tile * n_runs + e]
        s0, d0 = src_first(e), dst_first(e)

        def issue(j, carry, s0=s0, d0=d0):
            step = j * SLOT_ALIGN
            pltpu.make_async_copy(src_rows(s0 + step), dst_rows(d0 + step), sem).start()
            return carry

        lax.fori_loop(0, n, issue, 0)


def _wait_runs(n_ref, tile, src, dst, sem, n_runs=N_EXPERTS):
    total = 0
    for e in range(n_runs):
        total = total + n_ref[tile * n_runs + e]

    def drain(j, carry):
        pltpu.make_async_copy(src, dst, sem).wait()
        return carry

    lax.fori_loop(0, total, drain, 0)


def _span(first):
    return pl.ds(pl.multiple_of(first, SLOT_ALIGN), SLOT_ALIGN)


def _dispatch_kernel(dst_ref, off_ref, n_ref, sdst_ref, sn_ref, xa_ref, lm_ref, xs_hbm, buf, zbuf, sems, *, p,
                     tm):
    i = pl.program_id(0)
    last = pl.num_programs(0) - 1
    slot = i % 2
    t = xa_ref.shape[0]
    one_src, one_dst = buf.at[0, pl.ds(0, SLOT_ALIGN)], xs_hbm.at[pl.ds(0, SLOT_ALIGN)]

    @pl.when(i >= 2)
    def _():
        _wait_runs(n_ref, i - 2, one_src, one_dst, sems.at[slot])

    pos1 = lm_ref[0, POS_ROW:POS_ROW + 1, :].astype(jnp.int32)
    pos2 = lm_ref[0, POS_ROW + 1:POS_ROW + 2, :].astype(jnp.int32)
    xa = xa_ref[...]
    for r0 in range(0, p, PERM_CHUNK):
        row = r0 + lax.broadcasted_iota(jnp.int32, (PERM_CHUNK, t), 0)
        perm = ((row == pos1) | (row == pos2)).astype(BF16)
        buf[slot, r0:r0 + PERM_CHUNK] = jnp.dot(perm, xa, preferred_element_type=F32).astype(BF16)
    _start_runs(n_ref, i, lambda e: off_ref[i * N_EXPERTS + e], lambda e: dst_ref[i * N_EXPERTS + e],
                lambda first: buf.at[slot, _span(first)], lambda first: xs_hbm.at[_span(first)], sems.at[slot])

    @pl.when(i == last)
    def _():
        @pl.when(i >= 1)
        def _():
            _wait_runs(n_ref, i - 1, one_src, one_dst, sems.at[1 - slot])

        _wait_runs(n_ref, i, one_src, one_dst, sems.at[slot])
        zbuf[...] = jnp.zeros(zbuf.shape, BF16)
        _start_runs(sn_ref, 0, lambda e: 0, lambda e: sdst_ref[e],
                    lambda first: zbuf, lambda first: xs_hbm.at[_span(first)], sems.at[slot])
        _wait_runs(sn_ref, 0, one_src, one_dst, sems.at[slot])
        buf[slot, 0:tm] = jnp.zeros((tm, buf.shape[2]), BF16)
        tail = lambda j: pltpu.make_async_copy(
            buf.at[slot, pl.ds(0, tm)],
            xs_hbm.at[pl.ds(pl.multiple_of(sdst_ref[N_EXPERTS] + j * tm, SLOT_ALIGN), tm)], sems.at[slot])
        lax.fori_loop(0, sn_ref[N_EXPERTS], lambda j, c: (tail(j).start(), c)[1], 0)
        lax.fori_loop(0, sn_ref[N_EXPERTS], lambda j, c: (tail(j).wait(), c)[1], 0)


def _expert_kernel(e_ref, blk_ref, valid_ref, xs_ref, wg_ref, wu_ref, wd_ref, ys_ref, wg_bf, wu_bf, wd_bf, *, d):
    r = pl.program_id(0)
    e = e_ref[r]

    @pl.when(jnp.logical_or(r == 0, e != e_ref[jnp.maximum(r - 1, 0)]))
    def _():
        wg_bf[...] = wg_ref[0, 0, 0].astype(BF16)
        wu_bf[...] = wu_ref[0, 0, 0].astype(BF16)
        wd_bf[...] = wd_ref[0, 0, 0].astype(BF16)

    @pl.when(valid_ref[r] == 1)
    def _():
        x = xs_ref[:, :d]
        aux = xs_ref[:, d:].astype(F32)
        is_first = aux[:, 4:5] == e.astype(F32)
        gate = jnp.where(is_first, aux[:, 0:1] + aux[:, 1:2], aux[:, 2:3] + aux[:, 3:4])
        a = jnp.dot(x, wg_bf[...], preferred_element_type=F32)
        bu = jnp.dot(x, wu_bf[...], preferred_element_type=F32)
        hid = (a * jax.nn.sigmoid(a)) * bu * gate
        ys_ref[...] = jnp.dot(hid.astype(BF16), wd_bf[...], preferred_element_type=F32).astype(BF16)

    @pl.when(valid_ref[r] == 0)
    def _():
        ys_ref[...] = jnp.zeros(ys_ref.shape, BF16)


def _combine_kernel(*refs, final_norm):
    if final_norm:
        dst_ref, off_ref, n_ref, ys_hbm, tm_ref, x_ref, g2_ref, fg_ref, o_ref, buf, sems = refs
    else:
        dst_ref, off_ref, n_ref, ys_hbm, tm_ref, x_ref, g2_ref, o_ref, buf, sems = refs
    i = pl.program_id(0)
    slot = i % 2
    t = x_ref.shape[0]
    p = buf.shape[1]

    def fetch(tile, s):
        buf[s, 2 * t:] = jnp.zeros((p - 2 * t, buf.shape[2]), BF16)
        _start_runs(n_ref, tile, lambda e: dst_ref[tile * N_EXPERTS + e], lambda e: off_ref[tile * N_EXPERTS + e],
                    lambda first: ys_hbm.at[_span(first)], lambda first: buf.at[s, _span(first)], sems.at[s])

    @pl.when(i == 0)
    def _():
        fetch(i, slot)

    @pl.when(i + 1 < pl.num_programs(0))
    def _():
        fetch(i + 1, 1 - slot)

    _wait_runs(n_ref, i, ys_hbm.at[pl.ds(0, SLOT_ALIGN)], buf.at[0, pl.ds(0, SLOT_ALIGN)], sems.at[slot])
    pos1 = tm_ref[:, POS_ROW:POS_ROW + 1].astype(jnp.int32)
    pos2 = tm_ref[:, POS_ROW + 1:POS_ROW + 2].astype(jnp.int32)
    f = None
    for r0 in range(0, p, PERM_CHUNK):
        col = r0 + lax.broadcasted_iota(jnp.int32, (t, PERM_CHUNK), 1)
        perm_t = ((col == pos1) | (col == pos2)).astype(BF16)
        part = jnp.dot(perm_t, buf[slot, r0:r0 + PERM_CHUNK], preferred_element_type=F32)
        f = part if f is None else f + part
    y = x_ref[...] + g2_ref[0] * f
    if final_norm:
        y = _rmsnorm(y, fg_ref[...])
    o_ref[...] = y


def _moe(x, xa, info_tm, info_lm, runs, g2, w_gate, w_up, w_down, layer, final_g):
    b, l, d = x.shape
    da = xa.shape[2]
    f = w_gate.shape[-1]
    t = min(MOE_TOKEN_TILE, l)
    tiles_per_batch = l // t
    n_tiles = b * tiles_per_batch
    p = 2 * t + N_EXPERTS * SLOT_ALIGN
    n = b * l
    tm = MOE_ROW_TILE if 2 * n >= N_EXPERTS * MOE_ROW_TILE else MOE_ROW_TILE // 4
    plan, n_row_tiles = _moe_plan(runs[..., 0].reshape(n_tiles, ROUTE_ROWS), t, tm)
    s_rows = n_row_tiles * tm

    xs = pl.pallas_call(
        functools.partial(_dispatch_kernel, p=p, tm=tm),
        grid_spec=pltpu.PrefetchScalarGridSpec(
            num_scalar_prefetch=5,
            grid=(n_tiles,),
            in_specs=[
                pl.BlockSpec((t, da), lambda i, *_: (i, 0)),
                pl.BlockSpec((1, 8, t), lambda i, *_: (i // tiles_per_batch, 0, i % tiles_per_batch)),
            ],
            out_specs=pl.BlockSpec(memory_space=pl.ANY),
            scratch_shapes=[pltpu.VMEM((2, p, da), BF16), pltpu.VMEM((SLOT_ALIGN, da), BF16),
                            pltpu.SemaphoreType.DMA((2,))],
        ),
        out_shape=jax.ShapeDtypeStruct((s_rows, da), BF16),
        compiler_params=_params(("arbitrary",)),
        name="moe_dispatch",
    )(plan["dst"], plan["off"], plan["n"], plan["gap_dst"], plan["gap_n"], xa.reshape(n, da), info_lm)

    w_map = lambda r, e_ref, blk_ref, valid_ref: (
        layer, e_ref[r] // EXPERTS_PER_GROUP, e_ref[r] % EXPERTS_PER_GROUP, 0, 0)
    row_map = lambda r, e_ref, blk_ref, valid_ref: (blk_ref[r], 0)
    ys = pl.pallas_call(
        functools.partial(_expert_kernel, d=d),
        grid_spec=pltpu.PrefetchScalarGridSpec(
            num_scalar_prefetch=3,
            grid=(n_row_tiles,),
            in_specs=[
                pl.BlockSpec((tm, da), row_map),
                pl.BlockSpec((1, 1, 1, d, f), w_map),
                pl.BlockSpec((1, 1, 1, d, f), w_map),
                pl.BlockSpec((1, 1, 1, f, d), w_map),
            ],
            out_specs=pl.BlockSpec((tm, d), lambda r, *_: (r, 0)),
            scratch_shapes=[pltpu.VMEM((d, f), BF16), pltpu.VMEM((d, f), BF16), pltpu.VMEM((f, d), BF16)],
        ),
        out_shape=jax.ShapeDtypeStruct((s_rows, d), BF16),
        compiler_params=_params(("arbitrary",)),
        name="moe_experts",
    )(plan["expert"], plan["block"], plan["valid"], xs, w_gate, w_up, w_down)

    final_norm = final_g is not None
    in_specs = [
        pl.BlockSpec(memory_space=pl.ANY),
        pl.BlockSpec((t, ROUTER_LANES), lambda i, *_: (i, 0)),
        pl.BlockSpec((t, d), lambda i, *_: (i, 0)),
        pl.BlockSpec((1, 1, d), lambda i, *_: (i // tiles_per_batch, 0, 0)),
    ]
    args = [ys, info_tm.reshape(n, ROUTER_LANES), x.reshape(n, d), g2]
    if final_norm:
        in_specs.append(pl.BlockSpec((1, d), lambda i, *_: (0, 0)))
        args.append(final_g.reshape(1, d))
    out = pl.pallas_call(
        functools.partial(_combine_kernel, final_norm=final_norm),
        grid_spec=pltpu.PrefetchScalarGridSpec(
            num_scalar_prefetch=3,
            grid=(n_tiles,),
            in_specs=in_specs,
            out_specs=pl.BlockSpec((t, d), lambda i, *_: (i, 0)),
            scratch_shapes=[pltpu.VMEM((2, p, d), BF16), pltpu.SemaphoreType.DMA((2,))],
        ),
        out_shape=jax.ShapeDtypeStruct((n, d), F32),
        compiler_params=_params(("arbitrary",)),
        name="moe_combine",
    )(plan["dst"], plan["off"], plan["n"], *args)
    return out.reshape(b, l, d)


def kernel(x, c, ctx, c_ctx, ada_w, ada_b, norm1_g, norm2_g, w_in, pool_w, pool_scale, lambda_q1, lambda_k1,
           lambda_q2, lambda_k2, subln_g, w_out, router_coarse_w, router_coarse_b, router_fine_w,
           router_fine_b, w_gate, w_up, w_down, final_g):
    b, l, d = x.shape
    depth = ada_w.shape[0]
    assert b + 1 <= MOD_ROWS and l % GRID_W == 0

    tables = _rope_tables(l)
    cc = jnp.zeros((MOD_ROWS, d), F32).at[:b].set(c).at[b].set(c_ctx)
    d_pool = d // 2
    d_attn = (w_in.shape[-1] - d_pool) // 3
    w_uk = jnp.concatenate([w_in[..., :d_pool], w_in[..., d_pool + d_attn:d_pool + 2 * d_attn]],
                           axis=-1).astype(BF16)
    w_qvt = jnp.swapaxes(jnp.concatenate([w_in[..., d_pool:d_pool + d_attn], w_in[..., d_pool + 2 * d_attn:]],
                                         axis=-1), 1, 2).astype(BF16)
    zero_blk = jnp.zeros_like(pool_w[:, 0::2])
    pool_w_bf = jnp.concatenate([jnp.concatenate([pool_w[:, 0::2], zero_blk], axis=-1),
                                 jnp.concatenate([zero_blk, pool_w[:, 1::2]], axis=-1)], axis=-2).astype(BF16)
    w_out_bf = w_out.astype(BF16)
    pad = ROUTER_LANES - N_GROUPS - N_EXPERTS
    wr = jnp.concatenate([router_coarse_w, router_fine_w, jnp.zeros((depth, d, pad), F32)], axis=-1)
    wr_hi = wr.astype(BF16)
    wr = jnp.concatenate([wr_hi, (wr - wr_hi.astype(F32)).astype(BF16)], axis=-1)
    rb = jnp.concatenate([router_coarse_b, router_fine_b, jnp.zeros((depth, pad), F32)], axis=-1)

    xc = ctx
    for layer in range(depth):
        last = layer == depth - 1
        lam_init = 0.8 - 0.6 * math.exp(-0.3 * layer)
        mod = _ada(cc, ada_w, ada_b, layer)
        sh1, sc1, g1, sh2, sc2, g2 = jnp.split(mod[:b, None, :], 6, axis=-1)
        csh1, csc1, cg1, csh2, csc2, cg2 = jnp.split(
            jnp.broadcast_to(mod[b][None, None, :], (b, 1, mod.shape[1])), 6, axis=-1)
        lam4 = jnp.stack([lambda_q1[layer], lambda_k1[layer], lambda_q2[layer], lambda_k2[layer]])

        u, q, k, v = _inproj(x, sh1, sc1, norm1_g[layer], w_uk[layer], w_qvt[layer], tables)
        uc, qc, kc, vc = _inproj(xc, csh1, csc1, norm1_g[layer], w_uk[layer], w_qvt[layer], None)
        attn = _attention(q, k, v, kc, vc, lam4, subln_g[layer], lam_init)
        merge_w = (pool_w_bf[layer], pool_scale[layer], w_out_bf[layer])
        route_w = (wr[layer], rb[layer].reshape(1, ROUTER_LANES))
        x, *routed = _merge(u, attn, x, *merge_w, g1, norm2_g[layer], sh2, sc2, *route_w)
        if not last:
            attn_c = _attention(qc, kc, vc, None, None, lam4, subln_g[layer], lam_init)
            xc, *routed_c = _merge(uc, attn_c, xc, *merge_w, cg1, norm2_g[layer], csh2, csc2, *route_w)
            xc = _moe(xc, *routed_c, cg2, w_gate, w_up, w_down, layer, None)
        x = _moe(x, *routed, g2, w_gate, w_up, w_down, layer, final_g if last else None)
    return x
```
